```python
import functools
import jax, jax.numpy as jnp
from jax import lax
import numpy as np

D_MODEL = 1024
BATCH = 16
SEQ = 256
DEPTH = 2
DEC_BATCH = 2
DEC_SEQ = 2048
PAST_LEN = 512

GRID_W = 64
H_A = 8
DH_A = 64
D_A = H_A * DH_A
WIN_H_MAX = 8
WIN_W = 16
KEY_BLOCK_W = 2 * WIN_W
D_CONV = 512
CONV_W = 3
H_C = 4
DK_C = 128
DV_C = 128
D_CK = H_C * DK_C
D_CV = H_C * DV_C
CHUNK = 16
N_EXPERTS = 64
N_GROUPS = 8
EXPERTS_PER_GROUP = N_EXPERTS // N_GROUPS
TOPK_GROUPS = 4
TOP_K = 8
D_EXPERT = 256
D_SHARED = 256
ROUTED_SCALE = 2.5
N_MOD = 6
EPS = 1e-6
NEG_BIG = -1e30
LOG_FLOOR = 1e-30
SPLIT_SIZES = (D_A, D_A, D_A, D_CONV, D_CONV, D_CONV, D_CK, D_CK, D_CK, D_CV, D_CV, D_MODEL, D_MODEL, D_MODEL)
D_IN = 3 * D_A + 3 * D_CONV + 3 * D_CK + 2 * D_CV + 3 * D_MODEL

kernel_name = 'hybrid_diffusion_na_conv_hgrn2_moe_step'


def rmsnorm(x, g):
    xf = x.astype(jnp.float32)
    y = xf * lax.rsqrt(jnp.mean(xf * xf, axis=-1, keepdims=True) + EPS)
    return (y * g.astype(jnp.float32)).astype(x.dtype)


def lower_bounds(p):
    sm = jax.nn.softmax(p.astype(jnp.float32), axis=0)
    return jnp.cumsum(sm, axis=0) - sm[0:1]


def ctx_attention(q, k, v):
    bsz, t, h, dh = q.shape
    s = jnp.einsum('bqhd,bkhd->bhqk', q, k).astype(jnp.float32) * (dh ** -0.5)
    p = jax.nn.softmax(s, axis=-1).astype(v.dtype)
    return jnp.einsum('bhqk,bkhd->bqhd', p, v).reshape(bsz, t, h * dh)


def na_latent(q, k, v, ck, cv, rpb_l):
    bsz, t, h, dh = q.shape
    rows = t // GRID_W
    kh = min(WIN_H_MAX, rows)
    nbw = GRID_W // WIN_W
    r = np.arange(rows)
    row_idx = np.clip(r - kh // 2, 0, rows - kh)[:, None] + np.arange(kh)[None, :]
    cols = np.arange(GRID_W)
    col_start = np.clip(cols - WIN_W // 2, 0, GRID_W - WIN_W)
    blk_start = np.minimum(col_start[::WIN_W], GRID_W - KEY_BLOCK_W)
    col_idx = blk_start[:, None] + np.arange(KEY_BLOCK_W)[None, :]
    q_cols = cols.reshape(nbw, WIN_W)
    q_start = col_start.reshape(nbw, WIN_W)
    kc = col_idx[:, None, :]
    valid = (kc >= q_start[:, :, None]) & (kc < q_start[:, :, None] + WIN_W)
    dc_idx = np.clip(kc - q_cols[:, :, None] + WIN_W - 1, 0, 2 * WIN_W - 2)
    dr_idx = row_idx - r[:, None] + WIN_H_MAX - 1
    bias = rpb_l[:, dr_idx[:, None, None, :, None], dc_idx[None, :, :, None, :]]
    bias = jnp.where(valid[None, None, :, :, None, :], bias.astype(jnp.float32), NEG_BIG)
    bias = bias.transpose(1, 2, 0, 3, 4, 5).reshape(rows, nbw, h, WIN_W, kh * KEY_BLOCK_W)
    ri = row_idx[:, None, :, None]
    ci = col_idx[None, :, None, :]
    kg = k.reshape(bsz, rows, GRID_W, h, dh)
    vg = v.reshape(bsz, rows, GRID_W, h, dh)
    n_lat = kh * KEY_BLOCK_W
    k_blk = kg[:, ri, ci].reshape(bsz, rows, nbw, n_lat, h, dh)
    v_blk = vg[:, ri, ci].reshape(bsz, rows, nbw, n_lat, h, dh)
    qb = q.reshape(bsz, rows, nbw, WIN_W, h, dh)
    scale = dh ** -0.5
    s_lat = jnp.einsum('brjqhd,brjnhd->brjhqn', qb, k_blk).astype(jnp.float32) * scale + bias
    s_ctx = jnp.einsum('brjqhd,bchd->brjhqc', qb, ck).astype(jnp.float32) * scale
    p = jax.nn.softmax(jnp.concatenate([s_lat, s_ctx], axis=-1), axis=-1).astype(v.dtype)
    o = (jnp.einsum('brjhqn,brjnhd->brjqhd', p[..., :n_lat], v_blk)
         + jnp.einsum('brjhqc,bchd->brjqhd', p[..., n_lat:], cv))
    return o.reshape(bsz, t, h * dh)


def conv_branch(u, gb, gc, w):
    z = gc * u
    zp = jnp.pad(z, ((0, 0), (1, 1), (0, 0)))
    y = w[0] * zp[:, :-2] + w[1] * zp[:, 1:-1] + w[2] * zp[:, 2:]
    return gb * y


def gla_chunk(q, k, v, logf, s0):
    bsz, t, h, dk = q.shape
    dv = v.shape[-1]
    n = t // CHUNK

    def blk(a):
        return a.reshape(bsz, n, CHUNK, h, a.shape[-1]).transpose(0, 3, 1, 2, 4)

    q, k, v, logf = blk(q), blk(k), blk(v), blk(logf)
    b = jnp.cumsum(logf, axis=3)
    causal = np.tril(np.ones((CHUNK, CHUNK), dtype=bool))
    diff = b[:, :, :, :, None, :] - b[:, :, :, None, :, :]
    decay = jnp.exp(jnp.where(causal[:, :, None], diff, NEG_BIG))
    attn = jnp.einsum('bhntd,bhnsd,bhntsd->bhnts', q, k, decay)
    o_intra = jnp.einsum('bhnts,bhnsv->bhntv', attn, v)
    b_last = b[:, :, :, -1:, :]
    kv = jnp.einsum('bhnsd,bhnsv->bhndv', k * jnp.exp(b_last - b), v)
    chunk_decay = jnp.exp(b_last[:, :, :, 0, :])

    def step(s, inp):
        dec, kv_n = inp
        return dec[..., None] * s + kv_n, s

    s_final, s_prev = lax.scan(step, s0, (jnp.moveaxis(chunk_decay, 2, 0), jnp.moveaxis(kv, 2, 0)))
    o_inter = jnp.einsum('bhntd,nbhdv->bhntv', q * jnp.exp(b), s_prev)
    o = (o_intra + o_inter).transpose(0, 2, 3, 1, 4).reshape(bsz, t, h, dv)
    return o, s_final


def hgrn_branch(q, zf, zb, i, og, lb_f, lb_b, norm_g, s0f, s0b):
    bsz, t, _ = q.shape
    f32 = jnp.float32
    qh = q.astype(f32).reshape(bsz, t, H_C, DK_C)
    vh = i.astype(f32).reshape(bsz, t, H_C, DV_C)

    def gates(z, lb):
        z = z.astype(f32).reshape(bsz, t, H_C, DK_C)
        lb = lb.reshape(H_C, DK_C)
        logf = jnp.logaddexp(jnp.log1p(-lb) + jax.nn.log_sigmoid(z), jnp.log(jnp.maximum(lb, LOG_FLOOR)))
        key = (1.0 - lb) * jax.nn.sigmoid(-z)
        return logf, key

    lf_f, k_f = gates(zf, lb_f)
    lf_b, k_b = gates(zb, lb_b)
    o_f, s_f = gla_chunk(qh, k_f, vh, lf_f, s0f)
    rev = lambda a: jnp.flip(a, axis=1)
    o_b, s_b = gla_chunk(rev(qh), rev(k_b), rev(vh), rev(lf_b), s0b)
    o = o_f + rev(o_b)
    o = o * lax.rsqrt(jnp.mean(o * o, axis=-1, keepdims=True) + EPS) * norm_g.astype(f32).reshape(H_C, DV_C)
    o = o.reshape(bsz, t, D_CV) * jax.nn.silu(og.astype(f32))
    return o.astype(q.dtype), s_f, s_b


def moe(x, lp):
    n_tok = x.shape[0]
    s = jax.nn.sigmoid((x @ lp['w_router']).astype(jnp.float32))
    sel = s + lp['b_router'].astype(jnp.float32)
    grp_score = lax.top_k(sel.reshape(n_tok, N_GROUPS, EXPERTS_PER_GROUP), 2)[0].sum(-1)
    _, gidx = lax.top_k(grp_score, TOPK_GROUPS)
    gmask = jax.nn.one_hot(gidx, N_GROUPS, dtype=jnp.float32).sum(-2) > 0
    emask = jnp.repeat(gmask, EXPERTS_PER_GROUP, axis=-1)
    _, eidx = lax.top_k(jnp.where(emask, sel, NEG_BIG), TOP_K)
    w = jnp.take_along_axis(s, eidx, axis=-1)
    w = w / jnp.sum(w, axis=-1, keepdims=True) * ROUTED_SCALE
    combine = jnp.einsum('nk,nke->ne', w, jax.nn.one_hot(eidx, N_EXPERTS, dtype=jnp.float32)).astype(x.dtype)
    hg = jnp.einsum('nd,edf->nef', x, lp['w_gate'])
    hu = jnp.einsum('nd,edf->nef', x, lp['w_up'])
    hid = jax.nn.silu(hg) * hu * combine[:, :, None]
    y = jnp.einsum('nef,efd->nd', hid, lp['w_down'])
    sh = (jax.nn.silu(x @ lp['w_sh_gate']) * (x @ lp['w_sh_up'])) @ lp['w_sh_down']
    return y + sh


def trunk_layer(h, ada, lp, attend, s0f, s0b):
    bsz, t, _ = h.shape
    sh1, sc1, g1, sh2, sc2, g2 = jnp.split(ada, N_MOD, axis=-1)
    n = rmsnorm(h, lp['norm1_g']) * (1 + sc1) + sh1
    split_idx = np.cumsum(SPLIT_SIZES)[:-1].tolist()
    (qa, ka, va, u, gb, gc, qc, zf, zb, ic, oc, ga, gbr, gcr) = jnp.split(n @ lp['w_in'], split_idx, axis=-1)
    heads = lambda a: a.reshape(bsz, t, H_A, DH_A)
    ka, va = heads(ka), heads(va)
    o_a = attend(heads(qa), ka, va)
    o_b = conv_branch(u, gb, gc, lp['w_conv'])
    o_c, s_f, s_b = hgrn_branch(qc, zf, zb, ic, oc, lp['lb_f'], lp['lb_b'], lp['hgrn_g'], s0f, s0b)
    mix = (jax.nn.sigmoid(ga) * (o_a @ lp['w_br_a'])
           + jax.nn.sigmoid(gbr) * (o_b @ lp['w_br_b'])
           + jax.nn.sigmoid(gcr) * (o_c @ lp['w_br_c'])) @ lp['w_out']
    h = h + g1 * mix
    n2 = rmsnorm(h, lp['norm2_g']) * (1 + sc2) + sh2
    y = moe(n2.reshape(bsz * t, D_MODEL), lp).reshape(bsz, t, D_MODEL)
    h = h + g2 * y
    return h, ka, va, s_f, s_b


def setup_inputs(seed: int = 0) -> dict:
    key = jax.random.key(seed)
    ks = jax.random.split(key, 32)
    f32 = jnp.float32
    nrm = lambda kk, shape, sc: jax.random.normal(kk, shape, f32) * sc
    D = D_MODEL
    return {
        'x_prompt': nrm(ks[0], (BATCH, SEQ, D), 1.0),
        'x_sample': nrm(ks[1], (DEC_BATCH, DEC_SEQ, D), 1.0),
        'cache_k': nrm(ks[2], (DEC_BATCH, DEPTH, PAST_LEN, H_A, DH_A), 1.0),
        'cache_v': nrm(ks[3], (DEC_BATCH, DEPTH, PAST_LEN, H_A, DH_A), 1.0),
        'state_hgrn': nrm(ks[4], (DEC_BATCH, DEPTH, 2, H_C, DK_C, DV_C), 0.5),
        'c': nrm(ks[5], (DEC_BATCH, D), 1.0),
        'c_ctx': nrm(ks[6], (D,), 1.0),
        'norm1_g': 1.0 + nrm(ks[7], (DEPTH, D), 0.01),
        'norm2_g': 1.0 + nrm(ks[8], (DEPTH, D), 0.01),
        'w_ada': nrm(ks[9], (DEPTH, D, N_MOD * D), 0.3 * D ** -0.5),
        'b_ada': nrm(ks[10], (DEPTH, N_MOD * D), 0.01),
        'w_in': nrm(ks[11], (DEPTH, D, D_IN), D ** -0.5),
        'rpb': nrm(ks[12], (DEPTH, H_A, 2 * WIN_H_MAX - 1, 2 * WIN_W - 1), 0.1),
        'w_conv': nrm(ks[13], (DEPTH, CONV_W, D_CONV), CONV_W ** -0.5),
        'lb_fwd': nrm(ks[14], (DEPTH, D_CK), 0.1),
        'lb_bwd': nrm(ks[15], (DEPTH, D_CK), 0.1),
        'hgrn_norm_g': 1.0 + nrm(ks[16], (DEPTH, D_CV), 0.01),
        'w_br_a': nrm(ks[17], (DEPTH, D_A, D), D_A ** -0.5),
        'w_br_b': nrm(ks[18], (DEPTH, D_CONV, D), D_CONV ** -0.5),
        'w_br_c': nrm(ks[19], (DEPTH, D_CV, D), D_CV ** -0.5),
        'w_out': nrm(ks[20], (DEPTH, D, D), D ** -0.5),
        'w_router': nrm(ks[21], (DEPTH, D, N_EXPERTS), D ** -0.5),
        'b_router': nrm(ks[22], (DEPTH, N_EXPERTS), 0.01),
        'w_gate': nrm(ks[23], (DEPTH, N_EXPERTS, D, D_EXPERT), D ** -0.5),
        'w_up': nrm(ks[24], (DEPTH, N_EXPERTS, D, D_EXPERT), D ** -0.5),
        'w_down': nrm(ks[25], (DEPTH, N_EXPERTS, D_EXPERT, D), D_EXPERT ** -0.5),
        'w_sh_gate': nrm(ks[26], (DEPTH, D, D_SHARED), D ** -0.5),
        'w_sh_up': nrm(ks[27], (DEPTH, D, D_SHARED), D ** -0.5),
        'w_sh_down': nrm(ks[28], (DEPTH, D_SHARED, D), D_SHARED ** -0.5),
        'final_norm_g': 1.0 + nrm(ks[29], (D,), 0.01),
    }


def reference(x_prompt, x_sample, cache_k, cache_v, state_hgrn, c, c_ctx, norm1_g, norm2_g, w_ada, b_ada,
              w_in, rpb, w_conv, lb_fwd, lb_bwd, hgrn_norm_g, w_br_a, w_br_b, w_br_c, w_out, w_router,
              b_router, w_gate, w_up, w_down, w_sh_gate, w_sh_up, w_sh_down, final_norm_g):
    f32 = jnp.float32
    lbs_f = lower_bounds(lb_fwd)
    lbs_b = lower_bounds(lb_bwd)

    def layer_params(l):
        return {'norm1_g': norm1_g[l], 'norm2_g': norm2_g[l], 'w_in': w_in[l], 'w_conv': w_conv[l],
                'lb_f': lbs_f[l], 'lb_b': lbs_b[l], 'hgrn_g': hgrn_norm_g[l],
                'w_br_a': w_br_a[l], 'w_br_b': w_br_b[l], 'w_br_c': w_br_c[l], 'w_out': w_out[l],
                'w_router': w_router[l], 'b_router': b_router[l], 'w_gate': w_gate[l], 'w_up': w_up[l],
                'w_down': w_down[l], 'w_sh_gate': w_sh_gate[l], 'w_sh_up': w_sh_up[l], 'w_sh_down': w_sh_down[l]}

    h = x_prompt
    zero_state = jnp.zeros((x_prompt.shape[0], H_C, DK_C, DV_C), f32)
    ks_list, vs_list, st_list = [], [], []
    for l in range(DEPTH):
        ada = jax.nn.silu(c_ctx) @ w_ada[l] + b_ada[l]
        h, k_l, v_l, s_f, s_b = trunk_layer(h, ada, layer_params(l), ctx_attention, zero_state, zero_state)
        ks_list.append(k_l)
        vs_list.append(v_l)
        st_list.append(jnp.stack([s_f, s_b], axis=1))
    y_prompt = rmsnorm(h, final_norm_g)
    new_cache_k = jnp.stack(ks_list, axis=1)
    new_cache_v = jnp.stack(vs_list, axis=1)
    new_state_hgrn = jnp.stack(st_list, axis=1)

    g = x_sample
    for l in range(DEPTH):
        ada = (jax.nn.silu(c) @ w_ada[l] + b_ada[l])[:, None, :]
        attend = functools.partial(na_latent, ck=cache_k[:, l], cv=cache_v[:, l], rpb_l=rpb[l])
        g, _, _, _, _ = trunk_layer(g, ada, layer_params(l), attend,
                                    state_hgrn[:, l, 0].astype(f32), state_hgrn[:, l, 1].astype(f32))
    y_sample = rmsnorm(g, final_norm_g)
    return (y_prompt, y_sample, new_cache_k, new_cache_v, new_state_hgrn)
```

```python
import functools

import numpy as np
import jax
import jax.numpy as jnp
from jax import lax
from jax.experimental import pallas as pl
from jax.experimental.pallas import tpu as pltpu

F32 = jnp.float32
BF16 = jnp.bfloat16

D_MODEL = 1024
DEPTH = 2
GRID_W = 64
H_A = 8
DH_A = 64
D_A = H_A * DH_A
WIN_H = 8
WIN_W = 16
D_CONV = 512
H_C = 4
DK_C = 128
CHUNK = 16
N_EXPERTS = 64
N_GROUPS = 8
GROUP_SIZE = N_EXPERTS // N_GROUPS
TOPK_GROUPS = 4
TOP_K = 8
D_EXPERT = 256
ROUTED_SCALE = 2.5
COMB_W = 2 * N_EXPERTS
N_MOD = 6
EPS = 1e-6
NEG_BIG = -1e30
LOG_FLOOR = 1e-30

D_IN = 8704
COL_BLK = 512
N_COL_BLKS = D_IN // COL_BLK
SRC_GATE_BLK = 11
N_GATE_BLKS = 6
B_GA, B_GBR, B_GCR = 0, 2, 4
B_QA, B_KA, B_VA, B_U, B_GB, B_GC, B_QC, B_ZF, B_ZB, B_IC, B_OC = range(6, 17)

NA_QROWS = 4
NA_KROWS = 12
NA_TQ = NA_QROWS * GRID_W
NA_TK = NA_KROWS * GRID_W

TOK_TILE = 256
VMEM_LIMIT = 56 * 1024 * 1024


def _cparams(sem):
    return pltpu.CompilerParams(dimension_semantics=sem, vmem_limit_bytes=VMEM_LIMIT)


def _sigmoid(x):
    return 1.0 / (1.0 + jnp.exp(-x))


def _silu(x):
    return x * _sigmoid(x)


def _dot(a, b):
    return jnp.dot(a, b, preferred_element_type=F32)


def _dot_nt(a, b, precision=None):
    return lax.dot_general(a, b, (((1,), (1,)), ((), ())), preferred_element_type=F32,
                           precision=precision)


def _ada_kernel(c_ref, w_ref, b_ref, o_ref):
    a = _silu(c_ref[...])
    o_ref[0] = _dot(a, w_ref[0]) + b_ref[0]


def _ada_call(c8, w_ada, b_ada):
    tn = 1536
    n_out = N_MOD * D_MODEL
    return pl.pallas_call(
        _ada_kernel,
        grid=(DEPTH, n_out // tn),
        in_specs=[
            pl.BlockSpec((8, D_MODEL), lambda l, j: (0, 0)),
            pl.BlockSpec((1, D_MODEL, tn), lambda l, j: (l, 0, j)),
            pl.BlockSpec((1, 1, tn), lambda l, j: (l, 0, j)),
        ],
        out_specs=pl.BlockSpec((1, 8, tn), lambda l, j: (l, 0, j)),
        out_shape=jax.ShapeDtypeStruct((DEPTH, 8, n_out), F32),
        compiler_params=_cparams(("arbitrary", "arbitrary")),
        name="ada",
    )(c8, w_ada, b_ada.reshape(DEPTH, 1, n_out))


def _rms(x):
    return x * lax.rsqrt(jnp.mean(x * x, axis=-1, keepdims=True) + EPS)


def _norm_mod_kernel(h_ref, g_ref, sc_ref, sh_ref, o_ref):
    y = _rms(h_ref[...]) * g_ref[...]
    o_ref[...] = (y * (1.0 + sc_ref[0]) + sh_ref[0]).astype(o_ref.dtype)


def _ada_spec(mod_idx, row0, tiles_per_row):
    return pl.BlockSpec((1, 1, D_MODEL), lambda i: (row0 + i // tiles_per_row, 0, mod_idx))


def _norm_mod_call(h, g, ada, sc_idx, sh_idx, row0, tiles_per_row):
    m = h.shape[0]
    return pl.pallas_call(
        _norm_mod_kernel,
        grid=(m // TOK_TILE,),
        in_specs=[
            pl.BlockSpec((TOK_TILE, D_MODEL), lambda i: (i, 0)),
            pl.BlockSpec((1, D_MODEL), lambda i: (0, 0)),
            _ada_spec(sc_idx, row0, tiles_per_row),
            _ada_spec(sh_idx, row0, tiles_per_row),
        ],
        out_specs=pl.BlockSpec((TOK_TILE, D_MODEL), lambda i: (i, 0)),
        out_shape=jax.ShapeDtypeStruct((m, D_MODEL), BF16),
        compiler_params=_cparams(("arbitrary",)),
        name="norm_mod",
    )(h, g, ada, ada)


def _final_norm_kernel(h_ref, g_ref, o_ref):
    o_ref[...] = _rms(h_ref[...]) * g_ref[...]


def _final_norm_call(h, g):
    m = h.shape[0]
    return pl.pallas_call(
        _final_norm_kernel,
        grid=(m // TOK_TILE,),
        in_specs=[
            pl.BlockSpec((TOK_TILE, D_MODEL), lambda i: (i, 0)),
            pl.BlockSpec((1, D_MODEL), lambda i: (0, 0)),
        ],
        out_specs=pl.BlockSpec((TOK_TILE, D_MODEL), lambda i: (i, 0)),
        out_shape=jax.ShapeDtypeStruct((m, D_MODEL), F32),
        compiler_params=_cparams(("arbitrary",)),
        name="final_norm",
    )(h, g)


def _in_proj_kernel(a_ref, w_ref, o_ref):
    o_ref[...] = _dot(a_ref[...], w_ref[...].astype(BF16))


def _src_col_blk(j):
    return jnp.where(j < N_GATE_BLKS, j + SRC_GATE_BLK, j - N_GATE_BLKS)


def _in_proj_call(n, w_in, layer):
    m = n.shape[0]
    tm = 1024
    return pl.pallas_call(
        _in_proj_kernel,
        grid=(N_COL_BLKS, m // tm),
        in_specs=[
            pl.BlockSpec((tm, D_MODEL), lambda j, i: (i, 0)),
            pl.BlockSpec((None, D_MODEL, COL_BLK), lambda j, i: (layer, 0, _src_col_blk(j))),
        ],
        out_specs=pl.BlockSpec((tm, COL_BLK), lambda j, i: (i, j)),
        out_shape=jax.ShapeDtypeStruct((m, D_IN), F32),
        compiler_params=_cparams(("arbitrary", "arbitrary")),
        name="in_proj",
    )(n, w_in)


def _softmax_pv(s_list, v_list):
    m = s_list[0].max(axis=-1, keepdims=True)
    for s in s_list[1:]:
        m = jnp.maximum(m, s.max(axis=-1, keepdims=True))
    num = None
    den = None
    for s, v in zip(s_list, v_list):
        p = jnp.exp(s - m)
        d = p.sum(axis=-1, keepdims=True)
        o = _dot(p.astype(BF16), v.astype(BF16))
        num = o if num is None else num + o
        den = d if den is None else den + d
    return num / den


def _ctx_attn_kernel(q_ref, k_ref, v_ref, o_ref):
    scale = DH_A ** -0.5
    for h in range(H_A):
        sl = slice(h * DH_A, (h + 1) * DH_A)
        s = _dot_nt(q_ref[:, sl].astype(BF16), k_ref[:, sl].astype(BF16)) * scale
        o_ref[:, sl] = _softmax_pv([s], [v_ref[:, sl]])


def _ctx_attn_call(z, seq):
    m = z.shape[0]
    spec = lambda blk: pl.BlockSpec((seq, D_A), lambda b: (b, blk))
    return pl.pallas_call(
        _ctx_attn_kernel,
        grid=(m // seq,),
        in_specs=[spec(B_QA), spec(B_KA), spec(B_VA)],
        out_specs=pl.BlockSpec((seq, D_A), lambda b: (b, 0)),
        out_shape=jax.ShapeDtypeStruct((m, D_A), F32),
        compiler_params=_cparams(("arbitrary",)),
        name="ctx_attn",
    )(z, z, z)


def _na_key_row0(rb, rows):
    return jnp.clip(NA_QROWS * rb - (NA_KROWS - WIN_H) , 0, rows - NA_KROWS)


def _na_attn_kernel(q_ref, k_ref, v_ref, ck_ref, cv_ref, bias_ref, o_ref, *, rows):
    scale = DH_A ** -0.5
    rb = pl.program_id(1)
    k0 = pl.multiple_of(_na_key_row0(rb, rows) * GRID_W, GRID_W)
    for h in range(H_A):
        sl = slice(h * DH_A, (h + 1) * DH_A)
        q = q_ref[:, sl].astype(BF16)
        kw = k_ref[pl.ds(k0, NA_TK), sl]
        vw = v_ref[pl.ds(k0, NA_TK), sl]
        s_lat = _dot_nt(q, kw.astype(BF16)) * scale + bias_ref[h]
        s_ctx = _dot_nt(q, ck_ref[:, sl].astype(BF16)) * scale
        o_ref[:, sl] = _softmax_pv([s_lat, s_ctx], [vw, cv_ref[:, sl]])


def _na_bias_pattern(rb, n_rb):
    return jnp.where(rb == 0, 0, jnp.where(rb == n_rb - 1, 2, 1))


def _na_bias_tables(rpb_l, rows):
    n_rb = rows // NA_QROWS
    qc = np.arange(GRID_W)
    q_start = np.clip(qc - WIN_W // 2, 0, GRID_W - WIN_W)
    kc = np.arange(GRID_W)
    valid_c = (kc[None, :] >= q_start[:, None]) & (kc[None, :] < q_start[:, None] + WIN_W)
    dc = np.clip(kc[None, :] - qc[:, None] + WIN_W - 1, 0, 2 * WIN_W - 2)
    tabs = []
    for rb in (0, 1, n_rb - 1):
        k_row0 = int(np.clip(NA_QROWS * rb - (NA_KROWS - WIN_H), 0, rows - NA_KROWS))
        r = NA_QROWS * rb + np.arange(NA_QROWS)
        w0 = np.clip(r - WIN_H // 2, 0, rows - WIN_H)
        kr = k_row0 + np.arange(NA_KROWS)
        valid_r = (kr[None, :] >= w0[:, None]) & (kr[None, :] < w0[:, None] + WIN_H)
        dr = np.clip(kr[None, :] - r[:, None] + WIN_H - 1, 0, 2 * WIN_H - 2)
        valid = valid_r[:, None, :, None] & valid_c[None, :, None, :]
        dr_b = np.broadcast_to(dr[:, None, :, None], valid.shape)
        dc_b = np.broadcast_to(dc[None, :, None, :], valid.shape)
        b = rpb_l[:, dr_b, dc_b].astype(F32)
        b = jnp.where(valid[None], b, NEG_BIG)
        tabs.append(b.reshape(H_A, NA_TQ, NA_TK))
    return jnp.stack(tabs)


def _na_attn_call(z, ck, cv, bias, bsz, seq):
    rows = seq // GRID_W
    n_rb = rows // NA_QROWS
    m = z.shape[0]
    kv_spec = lambda blk: pl.BlockSpec((seq, D_A), lambda b, r: (b, blk))
    c_spec = pl.BlockSpec((None, ck.shape[1], D_A), lambda b, r: (b, 0, 0))
    return pl.pallas_call(
        functools.partial(_na_attn_kernel, rows=rows),
        grid=(bsz, n_rb),
        in_specs=[
            pl.BlockSpec((NA_TQ, D_A), lambda b, r: (b * n_rb + r, B_QA)),
            kv_spec(B_KA), kv_spec(B_VA), c_spec, c_spec,
            pl.BlockSpec((None, H_A, NA_TQ, NA_TK), lambda b, r: (_na_bias_pattern(r, n_rb), 0, 0, 0)),
        ],
        out_specs=pl.BlockSpec((NA_TQ, D_A), lambda b, r: (b * n_rb + r, 0)),
        out_shape=jax.ShapeDtypeStruct((m, D_A), F32),
        compiler_params=_cparams(("arbitrary", "arbitrary")),
        name="na_attn",
    )(z, z, z, ck, cv, bias)


def _log1p(x):
    return jnp.log1p(x)


def _hgrn_gates(z, lb):
    log_sig = jnp.minimum(z, 0.0) - _log1p(jnp.exp(-jnp.abs(z)))
    a = _log1p(-lb) + log_sig
    b = jnp.log(jnp.maximum(lb, LOG_FLOOR))
    logf = jnp.maximum(a, b) + _log1p(jnp.exp(-jnp.abs(a - b)))
    key = (1.0 - lb) * _sigmoid(-z)
    return logf, key


def _lower_bound(p_ref, layer):
    p = p_ref[...]
    e = jnp.exp(p - jnp.max(p, axis=0, keepdims=True))
    sm = e / jnp.sum(e, axis=0, keepdims=True)
    lb = jnp.zeros((1, DK_C), F32)
    for j in range(1, layer + 1):
        lb = lb + sm[j:j + 1]
    return lb


def _chunk_cumsum(x, seq, reverse):
    pos = lax.broadcasted_iota(jnp.int32, x.shape, 0) % CHUNK
    s = 1
    while s < CHUNK:
        if reverse:
            x = x + jnp.where(pos < CHUNK - s, pltpu.roll(x, seq - s, axis=0), 0.0)
        else:
            x = x + jnp.where(pos >= s, pltpu.roll(x, s, axis=0), 0.0)
        s *= 2
    return x


def _hgrn_direction(q_ref, v_ref, z_ref, lb, st0, o_ref, b_s, k_s, *, seq, reverse, accumulate):
    n_chunks = seq // CHUNK
    logf, key = _hgrn_gates(z_ref[...], lb)
    b_s[...] = _chunk_cumsum(logf, seq, reverse)
    k_s[...] = key
    edge = 0 if reverse else CHUNK - 1
    s_iota = lax.broadcasted_iota(jnp.int32, (CHUNK, 1), 0)

    def body(it, st):
        n = (n_chunks - 1 - it) if reverse else it
        r0 = pl.multiple_of(n * CHUNK, CHUNK)
        b = b_s[pl.ds(r0, CHUNK), :]
        q = q_ref[pl.ds(r0, CHUNK), :]
        k = k_s[pl.ds(r0, CHUNK), :]
        v = v_ref[pl.ds(r0, CHUNK), :]
        o = _dot_nt(q * jnp.exp(b), st)
        rows = []
        for t in range(CHUNK):
            d = jnp.exp(jnp.minimum(b[t:t + 1, :] - b, 0.0))
            a = jnp.sum((q[t:t + 1, :] * d) * k, axis=-1, keepdims=True)
            keep = (s_iota >= t) if reverse else (s_iota <= t)
            a = jnp.where(keep, a, 0.0)
            rows.append(jnp.sum(a * v, axis=0, keepdims=True))
        o = o + jnp.concatenate(rows, axis=0)
        if accumulate:
            o_ref[pl.ds(r0, CHUNK), :] += o
        else:
            o_ref[pl.ds(r0, CHUNK), :] = o
        b_edge = b[edge:edge + 1, :]
        kh = k * jnp.exp(b_edge - b)
        kv_t = lax.dot_general(v, kh, (((0,), (0,)), ((), ())), preferred_element_type=F32)
        return st * jnp.exp(b_edge) + kv_t

    return lax.fori_loop(0, n_chunks, body, st0)


def _hgrn_kernel(q_ref, zf_ref, zb_ref, i_ref, og_ref, lbf_ref, lbb_ref, g_ref, s0_ref,
                 o_ref, sfin_ref, acc_s, b_s, k_s, *, seq, layer):
    lb_f = _lower_bound(lbf_ref, layer)
    lb_b = _lower_bound(lbb_ref, layer)
    st_f = _hgrn_direction(q_ref, i_ref, zf_ref, lb_f, s0_ref[0].T, acc_s, b_s, k_s,
                           seq=seq, reverse=False, accumulate=False)
    st_b = _hgrn_direction(q_ref, i_ref, zb_ref, lb_b, s0_ref[1].T, acc_s, b_s, k_s,
                           seq=seq, reverse=True, accumulate=True)
    sfin_ref[0] = st_f.T
    sfin_ref[1] = st_b.T
    og = og_ref[...]
    o_ref[...] = _rms(acc_s[...]) * g_ref[...] * _silu(og)


def _hgrn_call(z, lb_fwd, lb_bwd, norm_g, s0, bsz, seq, layer):
    m = z.shape[0]
    w = DK_C
    per = COL_BLK // w
    zspec = lambda blk: pl.BlockSpec((seq, w), lambda b, h: (b, blk * per + h))
    st_spec = pl.BlockSpec((None, 2, None, DK_C, DK_C), lambda b, h: (b, 0, h, 0, 0))
    return pl.pallas_call(
        functools.partial(_hgrn_kernel, seq=seq, layer=layer),
        grid=(bsz, H_C),
        in_specs=[
            zspec(B_QC), zspec(B_ZF), zspec(B_ZB), zspec(B_IC), zspec(B_OC),
            pl.BlockSpec((DEPTH, w), lambda b, h: (0, h)),
            pl.BlockSpec((DEPTH, w), lambda b, h: (0, h)),
            pl.BlockSpec((1, w), lambda b, h: (0, h)),
            st_spec,
        ],
        out_specs=[pl.BlockSpec((seq, w), lambda b, h: (b, h)), st_spec],
        out_shape=[jax.ShapeDtypeStruct((m, H_C * w), F32),
                   jax.ShapeDtypeStruct((bsz, 2, H_C, DK_C, DK_C), F32)],
        scratch_shapes=[pltpu.VMEM((seq, w), F32)] * 3,
        compiler_params=_cparams(("arbitrary", "arbitrary")),
        name="hgrn",
    )(z, z, z, z, z, lb_fwd, lb_bwd, norm_g, s0)


def _mix_kernel(oa_ref, oc_ref, u_ref, gb_ref, gc_ref, up_ref, gcp_ref, un_ref, gcn_ref,
                ga_ref, gbr_ref, gcr_ref, h_ref, wconv_ref, wa_ref, wb_ref, wc_ref, wout_ref,
                wrt_ref, g1_ref, n2g_ref, sc2_ref, sh2_ref,
                hout_ref, n2_ref, lgt_ref, *, tiles_per_seq):
    i = pl.program_id(0)
    tm = u_ref.shape[0]
    pos = i % tiles_per_seq
    zc = gc_ref[...] * u_ref[...]
    z_prev = jnp.where(pos == 0, 0.0, gcp_ref[7:8, :] * up_ref[7:8, :])
    z_next = jnp.where(pos == tiles_per_seq - 1, 0.0, gcn_ref[0:1, :] * un_ref[0:1, :])
    row = lax.broadcasted_iota(jnp.int32, zc.shape, 0)
    zp = jnp.where(row == 0, z_prev, pltpu.roll(zc, 1, axis=0))
    zn = jnp.where(row == tm - 1, z_next, pltpu.roll(zc, tm - 1, axis=0))
    w = wconv_ref[...]
    ob = gb_ref[...] * (w[0:1] * zp + w[1:2] * zc + w[2:3] * zn)
    ya = _dot(oa_ref[...].astype(BF16), wa_ref[...])
    yb = _dot(ob.astype(BF16), wb_ref[...])
    yc = _dot(oc_ref[...].astype(BF16), wc_ref[...])
    pre = _sigmoid(ga_ref[...]) * ya + _sigmoid(gbr_ref[...]) * yb + _sigmoid(gcr_ref[...]) * yc
    hn = h_ref[...] + g1_ref[0] * _dot(pre.astype(BF16), wout_ref[...])
    hout_ref[...] = hn
    n2 = _rms(hn) * n2g_ref[...] * (1.0 + sc2_ref[0]) + sh2_ref[0]
    n2_ref[...] = n2.astype(BF16)
    lgt_ref[...] = _dot_nt(wrt_ref[...], n2, precision=lax.Precision.HIGHEST)


def _mix_call(o_a, o_c, z, h, w_conv, wa, wb, wc, wout, wrt, n2g, ada, row0, tiles_per_row, tiles_per_seq):
    m = h.shape[0]
    tm = TOK_TILE
    nt = m // tm
    r8 = tm // 8
    zblk = lambda blk: pl.BlockSpec((tm, COL_BLK), lambda i: (i, blk))
    zprev = lambda blk: pl.BlockSpec((8, COL_BLK), lambda i: (jnp.maximum(i * r8 - 1, 0), blk))
    znext = lambda blk: pl.BlockSpec((8, COL_BLK), lambda i: (jnp.minimum((i + 1) * r8, m // 8 - 1), blk))
    zgate = lambda blk: pl.BlockSpec((tm, D_MODEL), lambda i: (i, blk // 2))
    full = lambda a: pl.BlockSpec(a.shape, lambda i: (0,) * a.ndim)
    tok = lambda wdt: pl.BlockSpec((tm, wdt), lambda i: (i, 0))
    return pl.pallas_call(
        functools.partial(_mix_kernel, tiles_per_seq=tiles_per_seq),
        grid=(nt,),
        in_specs=[
            tok(D_A), tok(D_A), zblk(B_U), zblk(B_GB), zblk(B_GC),
            zprev(B_U), zprev(B_GC), znext(B_U), znext(B_GC),
            zgate(B_GA), zgate(B_GBR), zgate(B_GCR), tok(D_MODEL),
            full(w_conv), full(wa), full(wb), full(wc), full(wout), full(wrt),
            _ada_spec(2, row0, tiles_per_row), full(n2g),
            _ada_spec(4, row0, tiles_per_row), _ada_spec(3, row0, tiles_per_row),
        ],
        out_specs=[tok(D_MODEL), tok(D_MODEL), pl.BlockSpec((N_EXPERTS, tm), lambda i: (0, i))],
        out_shape=[jax.ShapeDtypeStruct((m, D_MODEL), F32),
                   jax.ShapeDtypeStruct((m, D_MODEL), BF16),
                   jax.ShapeDtypeStruct((N_EXPERTS, m), F32)],
        compiler_params=_cparams(("arbitrary",)),
        name="mix",
    )(o_a, o_c, z, z, z, z, z, z, z, z, z, z, h, w_conv, wa, wb, wc, wout, wrt, ada, n2g, ada, ada)


def _first_max_mask(x, axes_iota, size):
    m = x
    for ax in range(x.ndim - 1):
        m = jnp.max(m, axis=ax, keepdims=True)
    first = jnp.where(x == m, axes_iota, size)
    f = first
    for ax in range(x.ndim - 1):
        f = jnp.min(f, axis=ax, keepdims=True)
    return axes_iota == f


def _routing_kernel(lgt_ref, bias_ref, o_ref):
    tn = lgt_ref.shape[-1]
    s = _sigmoid(lgt_ref[...])
    sel = (s + bias_ref[...]).reshape(N_GROUPS, GROUP_SIZE, tn)
    s = s.reshape(N_GROUPS, GROUP_SIZE, tn)
    neg_inf = -jnp.inf
    mem_iota = lax.broadcasted_iota(jnp.int32, sel.shape, 1)
    m1 = jnp.max(sel, axis=1, keepdims=True)
    first = jnp.min(jnp.where(sel == m1, mem_iota, GROUP_SIZE), axis=1, keepdims=True)
    m2 = jnp.max(jnp.where(mem_iota == first, neg_inf, sel), axis=1, keepdims=True)
    grp = (m1 + m2).reshape(N_GROUPS, tn)
    g_iota = lax.broadcasted_iota(jnp.int32, grp.shape, 0)
    gmask = jnp.zeros(grp.shape, jnp.bool_)
    for _ in range(TOPK_GROUPS):
        pick = _first_max_mask(grp, g_iota, N_GROUPS)
        gmask = gmask | pick
        grp = jnp.where(pick, neg_inf, grp)
    cand = jnp.where(gmask.reshape(N_GROUPS, 1, tn), sel, NEG_BIG)
    e_iota = lax.broadcasted_iota(jnp.int32, sel.shape, 0) * GROUP_SIZE + mem_iota
    chosen = jnp.zeros(sel.shape, jnp.bool_)
    for _ in range(TOP_K):
        pick = _first_max_mask(cand, e_iota, N_EXPERTS)
        chosen = chosen | pick
        cand = jnp.where(pick, neg_inf, cand)
    w = jnp.where(chosen, s, 0.0)
    tot = jnp.sum(jnp.sum(w, axis=1, keepdims=True), axis=0, keepdims=True)
    w = (w / tot * ROUTED_SCALE).reshape(N_EXPERTS, tn)
    o_ref[...] = jnp.concatenate([w, jnp.zeros_like(w)], axis=0).T


def _routing_call(lgt, b_router_l):
    m = lgt.shape[1]
    tn = 256
    return pl.pallas_call(
        _routing_kernel,
        grid=(m // tn,),
        in_specs=[pl.BlockSpec((N_EXPERTS, tn), lambda i: (0, i)),
                  pl.BlockSpec((N_EXPERTS, tn), lambda i: (0, 0))],
        out_specs=pl.BlockSpec((tn, COMB_W), lambda i: (i, 0)),
        out_shape=jax.ShapeDtypeStruct((m, COMB_W), F32),
        compiler_params=_cparams(("arbitrary",)),
        name="routing",
    )(lgt, jnp.broadcast_to(b_router_l[:, None], (N_EXPERTS, tn)))


def _moe_kernel(x_ref, comb_ref, wg_ref, wu_ref, wd_ref, wsg_ref, wsu_ref, wsd_ref, h_ref, g2_ref,
                o_ref, acc_ref):
    e = pl.program_id(1)
    x = x_ref[...]

    @pl.when(e == 0)
    def _():
        sg = _dot(x, wsg_ref[...])
        su = _dot(x, wsu_ref[...])
        acc_ref[...] = _dot((_silu(sg) * su).astype(BF16), wsd_ref[...])

    hg = _dot(x, wg_ref[...].astype(BF16))
    hu = _dot(x, wu_ref[...].astype(BF16))
    comb = comb_ref[...]
    lane = lax.broadcasted_iota(jnp.int32, comb.shape, 1)
    c = jnp.sum(jnp.where(lane == e, comb, 0.0), axis=1, keepdims=True)
    hid = (_silu(hg) * hu * c).astype(BF16)
    acc_ref[...] += _dot(hid, wd_ref[...].astype(BF16))

    @pl.when(e == N_EXPERTS - 1)
    def _():
        o_ref[...] = h_ref[...] + g2_ref[0] * acc_ref[...]


def _moe_call(x, comb, w_gate, w_up, w_down, wsg, wsu, wsd, h, ada, layer, row0, toks_per_row):
    m = x.shape[0]
    tm = 1024
    tiles_per_row = toks_per_row // tm
    wspec = lambda k, n: pl.BlockSpec((None, None, k, n), lambda i, e: (layer, e, 0, 0))
    full = lambda a: pl.BlockSpec(a.shape, lambda i, e: (0,) * a.ndim)
    tok = lambda wdt: pl.BlockSpec((tm, wdt), lambda i, e: (i, 0))
    return pl.pallas_call(
        _moe_kernel,
        grid=(m // tm, N_EXPERTS),
        in_specs=[
            tok(D_MODEL), tok(COMB_W),
            wspec(D_MODEL, D_EXPERT), wspec(D_MODEL, D_EXPERT), wspec(D_EXPERT, D_MODEL),
            full(wsg), full(wsu), full(wsd), tok(D_MODEL),
            pl.BlockSpec((1, 1, D_MODEL), lambda i, e: (row0 + i // tiles_per_row, 0, 5)),
        ],
        out_specs=tok(D_MODEL),
        out_shape=jax.ShapeDtypeStruct((m, D_MODEL), F32),
        scratch_shapes=[pltpu.VMEM((tm, D_MODEL), F32)],
        compiler_params=_cparams(("arbitrary", "arbitrary")),
        name="moe",
    )(x, comb, w_gate, w_up, w_down, wsg, wsu, wsd, h, ada)


def _stream(h, p, ada, row0, bsz, seq, attend, states):
    toks_per_row = seq if row0 > 0 else bsz * seq
    tiles_per_row = toks_per_row // TOK_TILE
    tiles_per_seq = seq // TOK_TILE
    keys, vals, sts = [], [], []
    for l in range(DEPTH):
        ada_l = ada[l].reshape(8, 1, N_MOD * D_MODEL)
        n = _norm_mod_call(h, p['norm1_g'][l:l + 1], ada_l, 1, 0, row0, tiles_per_row)
        z = _in_proj_call(n, p['w_in'], l)
        o_a = attend(z, l)
        o_c, st = _hgrn_call(z, p['lb_fwd'], p['lb_bwd'], p['hgrn_norm_g'][l:l + 1], states[l], bsz, seq, l)
        h, n2, lgt = _mix_call(o_a, o_c, z, h, p['w_conv'][l], p['w_br_a'][l], p['w_br_b'][l], p['w_br_c'][l],
                               p['w_out'][l], p['w_router_t'][l], p['norm2_g'][l:l + 1], ada_l,
                               row0, tiles_per_row, tiles_per_seq)
        comb = _routing_call(lgt, p['b_router'][l])
        h = _moe_call(n2, comb, p['w_gate'], p['w_up'], p['w_down'], p['w_sh_gate'][l], p['w_sh_up'][l],
                      p['w_sh_down'][l], h, ada_l, l, row0, toks_per_row)
        keys.append(z[:, B_KA * COL_BLK:(B_KA + 1) * COL_BLK])
        vals.append(z[:, B_VA * COL_BLK:(B_VA + 1) * COL_BLK])
        sts.append(st)
    return h, keys, vals, sts


def kernel(x_prompt, x_sample, cache_k, cache_v, state_hgrn, c, c_ctx, norm1_g, norm2_g, w_ada, b_ada,
           w_in, rpb, w_conv, lb_fwd, lb_bwd, hgrn_norm_g, w_br_a, w_br_b, w_br_c, w_out, w_router,
           b_router, w_gate, w_up, w_down, w_sh_gate, w_sh_up, w_sh_down, final_norm_g):
    batch, seq, _ = x_prompt.shape
    dec_batch, dec_seq, _ = x_sample.shape
    past = cache_k.shape[2]
    p = dict(norm1_g=norm1_g, norm2_g=norm2_g, w_in=w_in, w_conv=w_conv, lb_fwd=lb_fwd, lb_bwd=lb_bwd,
             hgrn_norm_g=hgrn_norm_g,
             w_br_a=w_br_a.astype(BF16), w_br_b=w_br_b.astype(BF16), w_br_c=w_br_c.astype(BF16),
             w_out=w_out.astype(BF16), w_router_t=jnp.swapaxes(w_router, 1, 2), b_router=b_router,
             w_gate=w_gate, w_up=w_up, w_down=w_down,
             w_sh_gate=w_sh_gate.astype(BF16), w_sh_up=w_sh_up.astype(BF16), w_sh_down=w_sh_down.astype(BF16))

    c8 = jnp.zeros((8, D_MODEL), F32).at[0].set(c_ctx).at[1:1 + dec_batch].set(c)
    ada = _ada_call(c8, w_ada, b_ada)

    zero_states = [jnp.zeros((batch, 2, H_C, DK_C, DK_C), F32)] * DEPTH
    ctx_attend = lambda z, l: _ctx_attn_call(z, seq)
    h_ctx, keys, vals, sts = _stream(x_prompt.reshape(batch * seq, D_MODEL), p, ada, 0, batch, seq,
                                     ctx_attend, zero_states)
    y_prompt = _final_norm_call(h_ctx, final_norm_g.reshape(1, D_MODEL)).reshape(batch, seq, D_MODEL)
    new_cache_k = jnp.stack([k.reshape(batch, seq, H_A, DH_A) for k in keys], axis=1)
    new_cache_v = jnp.stack([v.reshape(batch, seq, H_A, DH_A) for v in vals], axis=1)
    new_state = jnp.stack(sts, axis=1)

    rows = dec_seq // GRID_W
    ck = cache_k.reshape(dec_batch, DEPTH, past, D_A)
    cv = cache_v.reshape(dec_batch, DEPTH, past, D_A)
    lat_states = [state_hgrn[:, l].astype(F32) for l in range(DEPTH)]

    def lat_attend(z, l):
        bias = _na_bias_tables(rpb[l], rows)
        return _na_attn_call(z, ck[:, l], cv[:, l], bias, dec_batch, dec_seq)

    h_lat, _, _, _ = _stream(x_sample.reshape(dec_batch * dec_seq, D_MODEL), p, ada, 1, dec_batch, dec_seq,
                             lat_attend, lat_states)
    y_sample = _final_norm_call(h_lat, final_norm_g.reshape(1, D_MODEL)).reshape(dec_batch, dec_seq, D_MODEL)
    return (y_prompt, y_sample, new_cache_k, new_cache_v, new_state)
```

```python
import functools

import numpy as np
import jax
import jax.numpy as jnp
from jax import lax
from jax.experimental import pallas as pl
from jax.experimental.pallas import tpu as pltpu

F32 = jnp.float32
BF16 = jnp.bfloat16

D_MODEL = 1024
DEPTH = 2
GRID_W = 64
H_A = 8
DH_A = 64
D_A = H_A * DH_A
WIN_H = 8
WIN_W = 16
D_CONV = 512
H_C = 4
DK_C = 128
CHUNK = 16
N_EXPERTS = 64
N_GROUPS = 8
GROUP_SIZE = N_EXPERTS // N_GROUPS
TOPK_GROUPS = 4
TOP_K = 8
D_EXPERT = 256
ROUTED_SCALE = 2.5
COMB_W = 2 * N_EXPERTS
N_MOD = 6
EPS = 1e-6
NEG_BIG = -1e30
LOG_FLOOR = 1e-30

D_IN = 8704
COL_BLK = 512
N_COL_BLKS = D_IN // COL_BLK
SRC_GATE_BLK = 11
N_GATE_BLKS = 6
B_GA, B_GBR, B_GCR = 0, 2, 4
B_QA, B_KA, B_VA, B_U, B_GB, B_GC, B_QC, B_ZF, B_ZB, B_IC, B_OC = range(6, 17)

NA_QROWS = 4
NA_KROWS = 12
NA_TQ = NA_QROWS * GRID_W
NA_TK = NA_KROWS * GRID_W

TOK_TILE = 256
VMEM_LIMIT = 56 * 1024 * 1024


def _cparams(sem):
    return pltpu.CompilerParams(dimension_semantics=sem, vmem_limit_bytes=VMEM_LIMIT)


def _sigmoid(x):
    return 1.0 / (1.0 + jnp.exp(-x))


def _silu(x):
    return x * _sigmoid(x)


def _dot(a, b):
    return jnp.dot(a, b, preferred_element_type=F32)


def _dot_nt(a, b, precision=None):
    return lax.dot_general(a, b, (((1,), (1,)), ((), ())), preferred_element_type=F32,
                           precision=precision)


def _ada_kernel(c_ref, w_ref, b_ref, o_ref):
    a = _silu(c_ref[...])
    o_ref[0] = _dot(a, w_ref[0]) + b_ref[0]


def _ada_call(c8, w_ada, b_ada):
    tn = 1536
    n_out = N_MOD * D_MODEL
    return pl.pallas_call(
        _ada_kernel,
        grid=(DEPTH, n_out // tn),
        in_specs=[
            pl.BlockSpec((8, D_MODEL), lambda l, j: (0, 0)),
            pl.BlockSpec((1, D_MODEL, tn), lambda l, j: (l, 0, j)),
            pl.BlockSpec((1, 1, tn), lambda l, j: (l, 0, j)),
        ],
        out_specs=pl.BlockSpec((1, 8, tn), lambda l, j: (l, 0, j)),
        out_shape=jax.ShapeDtypeStruct((DEPTH, 8, n_out), F32),
        compiler_params=_cparams(("arbitrary", "arbitrary")),
        name="ada",
    )(c8, w_ada, b_ada.reshape(DEPTH, 1, n_out))


def _rms(x):
    return x * lax.rsqrt(jnp.mean(x * x, axis=-1, keepdims=True) + EPS)


def _norm_mod_kernel(h_ref, g_ref, sc_ref, sh_ref, o_ref):
    y = _rms(h_ref[...]) * g_ref[...]
    o_ref[...] = (y * (1.0 + sc_ref[0]) + sh_ref[0]).astype(o_ref.dtype)


def _ada_spec(mod_idx, row0, tiles_per_row):
    return pl.BlockSpec((1, 1, D_MODEL), lambda i: (row0 + i // tiles_per_row, 0, mod_idx))


def _norm_mod_call(h, g, ada, sc_idx, sh_idx, row0, tiles_per_row):
    m = h.shape[0]
    return pl.pallas_call(
        _norm_mod_kernel,
        grid=(m // TOK_TILE,),
        in_specs=[
            pl.BlockSpec((TOK_TILE, D_MODEL), lambda i: (i, 0)),
            pl.BlockSpec((1, D_MODEL), lambda i: (0, 0)),
            _ada_spec(sc_idx, row0, tiles_per_row),
            _ada_spec(sh_idx, row0, tiles_per_row),
        ],
        out_specs=pl.BlockSpec((TOK_TILE, D_MODEL), lambda i: (i, 0)),
        out_shape=jax.ShapeDtypeStruct((m, D_MODEL), BF16),
        compiler_params=_cparams(("arbitrary",)),
        name="norm_mod",
    )(h, g, ada, ada)


def _final_norm_kernel(h_ref, g_ref, o_ref):
    o_ref[...] = _rms(h_ref[...]) * g_ref[...]


def _final_norm_call(h, g):
    m = h.shape[0]
    return pl.pallas_call(
        _final_norm_kernel,
        grid=(m // TOK_TILE,),
        in_specs=[
            pl.BlockSpec((TOK_TILE, D_MODEL), lambda i: (i, 0)),
            pl.BlockSpec((1, D_MODEL), lambda i: (0, 0)),
        ],
        out_specs=pl.BlockSpec((TOK_TILE, D_MODEL), lambda i: (i, 0)),
        out_shape=jax.ShapeDtypeStruct((m, D_MODEL), F32),
        compiler_params=_cparams(("arbitrary",)),
        name="final_norm",
    )(h, g)


def _in_proj_kernel(a_ref, w_ref, o_ref):
    o_ref[...] = _dot(a_ref[...], w_ref[...].astype(BF16))


def _src_col_blk(j):
    return jnp.where(j < N_GATE_BLKS, j + SRC_GATE_BLK, j - N_GATE_BLKS)


def _in_proj_call(n, w_in, layer):
    m = n.shape[0]
    tm = 1024
    return pl.pallas_call(
        _in_proj_kernel,
        grid=(N_COL_BLKS, m // tm),
        in_specs=[
            pl.BlockSpec((tm, D_MODEL), lambda j, i: (i, 0)),
            pl.BlockSpec((None, D_MODEL, COL_BLK), lambda j, i: (layer, 0, _src_col_blk(j))),
        ],
        out_specs=pl.BlockSpec((tm, COL_BLK), lambda j, i: (i, j)),
        out_shape=jax.ShapeDtypeStruct((m, D_IN), F32),
        compiler_params=_cparams(("arbitrary", "arbitrary")),
        name="in_proj",
    )(n, w_in)


def _softmax_pv(s_list, v_list):
    m = s_list[0].max(axis=-1, keepdims=True)
    for s in s_list[1:]:
        m = jnp.maximum(m, s.max(axis=-1, keepdims=True))
    num = None
    den = None
    for s, v in zip(s_list, v_list):
        p = jnp.exp(s - m)
        d = p.sum(axis=-1, keepdims=True)
        o = _dot(p.astype(BF16), v.astype(BF16))
        num = o if num is None else num + o
        den = d if den is None else den + d
    return num / den


def _ctx_attn_kernel(q_ref, k_ref, v_ref, o_ref):
    scale = DH_A ** -0.5
    for h in range(H_A):
        sl = slice(h * DH_A, (h + 1) * DH_A)
        s = _dot_nt(q_ref[:, sl].astype(BF16), k_ref[:, sl].astype(BF16)) * scale
        o_ref[:, sl] = _softmax_pv([s], [v_ref[:, sl]])


def _ctx_attn_call(z, seq):
    m = z.shape[0]
    spec = lambda blk: pl.BlockSpec((seq, D_A), lambda b: (b, blk))
    return pl.pallas_call(
        _ctx_attn_kernel,
        grid=(m // seq,),
        in_specs=[spec(B_QA), spec(B_KA), spec(B_VA)],
        out_specs=pl.BlockSpec((seq, D_A), lambda b: (b, 0)),
        out_shape=jax.ShapeDtypeStruct((m, D_A), F32),
        compiler_params=_cparams(("arbitrary",)),
        name="ctx_attn",
    )(z, z, z)


def _na_key_row0(rb, rows):
    return jnp.clip(NA_QROWS * rb - (NA_KROWS - WIN_H) , 0, rows - NA_KROWS)


def _na_attn_kernel(q_ref, k_ref, v_ref, ck_ref, cv_ref, bias_ref, o_ref, *, rows):
    scale = DH_A ** -0.5
    rb = pl.program_id(1)
    k0 = pl.multiple_of(_na_key_row0(rb, rows) * GRID_W, GRID_W)
    for h in range(H_A):
        sl = slice(h * DH_A, (h + 1) * DH_A)
        q = q_ref[:, sl].astype(BF16)
        kw = k_ref[pl.ds(k0, NA_TK), sl]
        vw = v_ref[pl.ds(k0, NA_TK), sl]
        s_lat = _dot_nt(q, kw.astype(BF16)) * scale + bias_ref[h]
        s_ctx = _dot_nt(q, ck_ref[:, sl].astype(BF16)) * scale
        o_ref[:, sl] = _softmax_pv([s_lat, s_ctx], [vw, cv_ref[:, sl]])


def _na_bias_pattern(rb, n_rb):
    return jnp.where(rb == 0, 0, jnp.where(rb == n_rb - 1, 2, 1))


def _na_bias_tables(rpb_l, rows):
    n_rb = rows // NA_QROWS
    qc = np.arange(GRID_W)
    q_start = np.clip(qc - WIN_W // 2, 0, GRID_W - WIN_W)
    kc = np.arange(GRID_W)
    valid_c = (kc[None, :] >= q_start[:, None]) & (kc[None, :] < q_start[:, None] + WIN_W)
    pad = GRID_W - WIN_W
    rpb_pad = jnp.pad(rpb_l.astype(F32), ((0, 0), (0, 0), (pad, pad)))
    col_tab = jnp.stack([rpb_pad[:, :, GRID_W - 1 - c:2 * GRID_W - 1 - c] for c in range(GRID_W)], axis=2)
    col_tab = jnp.where(valid_c, col_tab, NEG_BIG)
    masked = jnp.full((H_A, GRID_W, GRID_W), NEG_BIG, F32)
    tabs = []
    for rb in (0, 1, n_rb - 1):
        k_row0 = int(np.clip(NA_QROWS * rb - (NA_KROWS - WIN_H), 0, rows - NA_KROWS))
        q_rows = []
        for i in range(NA_QROWS):
            r = NA_QROWS * rb + i
            w0 = int(np.clip(r - WIN_H // 2, 0, rows - WIN_H))
            blocks = []
            for j in range(NA_KROWS):
                kr = k_row0 + j
                in_window = w0 <= kr < w0 + WIN_H
                blocks.append(col_tab[:, kr - r + WIN_H - 1] if in_window else masked)
            q_rows.append(jnp.concatenate(blocks, axis=2))
        tabs.append(jnp.concatenate(q_rows, axis=1))
    return jnp.stack(tabs)


def _na_attn_call(z, ck, cv, bias, bsz, seq):
    rows = seq // GRID_W
    n_rb = rows // NA_QROWS
    m = z.shape[0]
    kv_spec = lambda blk: pl.BlockSpec((seq, D_A), lambda b, r: (b, blk))
    c_spec = pl.BlockSpec((None, ck.shape[1], D_A), lambda b, r: (b, 0, 0))
    return pl.pallas_call(
        functools.partial(_na_attn_kernel, rows=rows),
        grid=(bsz, n_rb),
        in_specs=[
            pl.BlockSpec((NA_TQ, D_A), lambda b, r: (b * n_rb + r, B_QA)),
            kv_spec(B_KA), kv_spec(B_VA), c_spec, c_spec,
            pl.BlockSpec((None, H_A, NA_TQ, NA_TK), lambda b, r: (_na_bias_pattern(r, n_rb), 0, 0, 0)),
        ],
        out_specs=pl.BlockSpec((NA_TQ, D_A), lambda b, r: (b * n_rb + r, 0)),
        out_shape=jax.ShapeDtypeStruct((m, D_A), F32),
        compiler_params=_cparams(("arbitrary", "arbitrary")),
        name="na_attn",
    )(z, z, z, ck, cv, bias)


def _log1p(x):
    return jnp.log1p(x)


def _hgrn_gates(z, lb):
    log_sig = jnp.minimum(z, 0.0) - _log1p(jnp.exp(-jnp.abs(z)))
    a = _log1p(-lb) + log_sig
    b = jnp.log(jnp.maximum(lb, LOG_FLOOR))
    logf = jnp.maximum(a, b) + _log1p(jnp.exp(-jnp.abs(a - b)))
    key = (1.0 - lb) * _sigmoid(-z)
    return logf, key


def _lower_bound(p_ref, layer):
    p = p_ref[...]
    e = jnp.exp(p - jnp.max(p, axis=0, keepdims=True))
    sm = e / jnp.sum(e, axis=0, keepdims=True)
    lb = jnp.zeros((1, DK_C), F32)
    for j in range(1, layer + 1):
        lb = lb + sm[j:j + 1]
    return lb


def _chunk_cumsum(x, seq, reverse):
    pos = lax.broadcasted_iota(jnp.int32, x.shape, 0) % CHUNK
    s = 1
    while s < CHUNK:
        if reverse:
            x = x + jnp.where(pos < CHUNK - s, pltpu.roll(x, seq - s, axis=0), 0.0)
        else:
            x = x + jnp.where(pos >= s, pltpu.roll(x, s, axis=0), 0.0)
        s *= 2
    return x


HALF = CHUNK // 2
HGRN_UNROLL = 4


def _hgrn_chunk(q_ref, v_ref, b_s, k_s, o_s, r0, st, reverse):
    b = b_s[pl.ds(r0, CHUNK), :]
    q = q_ref[pl.ds(r0, CHUNK), :]
    k = k_s[pl.ds(r0, CHUNK), :]
    v = v_ref[pl.ds(r0, CHUNK), :]
    o = _dot_nt(q * jnp.exp(b), st)
    t_iota = lax.broadcasted_iota(jnp.int32, (HALF, 1), 0)
    o_half = [o[:HALF], o[HALF:]]
    for s in range(CHUNK):
        b_row = b_s[pl.ds(r0 + s, 1), :]
        k_row = k_s[pl.ds(r0 + s, 1), :]
        v_row = v_ref[pl.ds(r0 + s, 1), :]
        for half in range(2):
            t0 = half * HALF
            if (t0 + HALF - 1 < s) if not reverse else (t0 > s):
                continue
            sl = slice(t0, t0 + HALF)
            decay = jnp.exp(jnp.minimum(b[sl] - b_row, 0.0))
            col = jnp.sum(q[sl] * (k_row * decay), axis=-1, keepdims=True)
            if not ((t0 >= s) if not reverse else (t0 + HALF - 1 <= s)):
                keep = (t_iota + t0 >= s) if not reverse else (t_iota + t0 <= s)
                col = jnp.where(keep, col, 0.0)
            o_half[half] = o_half[half] + col * v_row
    o_s[pl.ds(r0, CHUNK), :] = jnp.concatenate(o_half, axis=0)
    edge = 0 if reverse else CHUNK - 1
    b_edge = b[edge:edge + 1, :]
    kh = k * jnp.exp(b_edge - b)
    kv_t = lax.dot_general(v, kh, (((0,), (0,)), ((), ())), preferred_element_type=F32)
    return st * jnp.exp(b_edge) + kv_t


def _hgrn_kernel(q_ref, zf_ref, zb_ref, i_ref, og_ref, lbf_ref, lbb_ref, g_ref, s0_ref,
                 o_ref, sfin_ref, of_s, ob_s, bf_s, bb_s, kf_s, kb_s, *, seq, layer):
    n_chunks = seq // CHUNK
    logf, key = _hgrn_gates(zf_ref[...], _lower_bound(lbf_ref, layer))
    bf_s[...] = _chunk_cumsum(logf, seq, False)
    kf_s[...] = key
    logf, key = _hgrn_gates(zb_ref[...], _lower_bound(lbb_ref, layer))
    bb_s[...] = _chunk_cumsum(logf, seq, True)
    kb_s[...] = key

    def body(it, carry):
        st_f, st_b = carry
        for u in range(HGRN_UNROLL):
            n = it * HGRN_UNROLL + u
            st_f = _hgrn_chunk(q_ref, i_ref, bf_s, kf_s, of_s, pl.multiple_of(n * CHUNK, CHUNK), st_f, False)
            st_b = _hgrn_chunk(q_ref, i_ref, bb_s, kb_s, ob_s,
                               pl.multiple_of((n_chunks - 1 - n) * CHUNK, CHUNK), st_b, True)
        return st_f, st_b

    st_f, st_b = lax.fori_loop(0, n_chunks // HGRN_UNROLL, body, (s0_ref[0].T, s0_ref[1].T))
    sfin_ref[0] = st_f.T
    sfin_ref[1] = st_b.T
    o_ref[...] = _rms(of_s[...] + ob_s[...]) * g_ref[...] * _silu(og_ref[...])


def _hgrn_call(z, lb_fwd, lb_bwd, norm_g, s0, bsz, seq, layer):
    m = z.shape[0]
    w = DK_C
    per = COL_BLK // w
    zspec = lambda blk: pl.BlockSpec((seq, w), lambda b, h: (b, blk * per + h))
    st_spec = pl.BlockSpec((None, 2, None, DK_C, DK_C), lambda b, h: (b, 0, h, 0, 0))
    return pl.pallas_call(
        functools.partial(_hgrn_kernel, seq=seq, layer=layer),
        grid=(bsz, H_C),
        in_specs=[
            zspec(B_QC), zspec(B_ZF), zspec(B_ZB), zspec(B_IC), zspec(B_OC),
            pl.BlockSpec((DEPTH, w), lambda b, h: (0, h)),
            pl.BlockSpec((DEPTH, w), lambda b, h: (0, h)),
            pl.BlockSpec((1, w), lambda b, h: (0, h)),
            st_spec,
        ],
        out_specs=[pl.BlockSpec((seq, w), lambda b, h: (b, h)), st_spec],
        out_shape=[jax.ShapeDtypeStruct((m, H_C * w), F32),
                   jax.ShapeDtypeStruct((bsz, 2, H_C, DK_C, DK_C), F32)],
        scratch_shapes=[pltpu.VMEM((seq, w), F32)] * 6,
        compiler_params=_cparams(("arbitrary", "arbitrary")),
        name="hgrn",
    )(z, z, z, z, z, lb_fwd, lb_bwd, norm_g, s0)


def _mix_kernel(oa_ref, oc_ref, u_ref, gb_ref, gc_ref, up_ref, gcp_ref, un_ref, gcn_ref,
                ga_ref, gbr_ref, gcr_ref, h_ref, wconv_ref, wa_ref, wb_ref, wc_ref, wout_ref,
                wrt_ref, g1_ref, n2g_ref, sc2_ref, sh2_ref,
                hout_ref, n2_ref, lgt_ref, *, tiles_per_seq):
    i = pl.program_id(0)
    tm = u_ref.shape[0]
    pos = i % tiles_per_seq
    zc = gc_ref[...] * u_ref[...]
    z_prev = jnp.where(pos == 0, 0.0, gcp_ref[7:8, :] * up_ref[7:8, :])
    z_next = jnp.where(pos == tiles_per_seq - 1, 0.0, gcn_ref[0:1, :] * un_ref[0:1, :])
    row = lax.broadcasted_iota(jnp.int32, zc.shape, 0)
    zp = jnp.where(row == 0, z_prev, pltpu.roll(zc, 1, axis=0))
    zn = jnp.where(row == tm - 1, z_next, pltpu.roll(zc, tm - 1, axis=0))
    w = wconv_ref[...]
    ob = gb_ref[...] * (w[0:1] * zp + w[1:2] * zc + w[2:3] * zn)
    ya = _dot(oa_ref[...].astype(BF16), wa_ref[...])
    yb = _dot(ob.astype(BF16), wb_ref[...])
    yc = _dot(oc_ref[...].astype(BF16), wc_ref[...])
    pre = _sigmoid(ga_ref[...]) * ya + _sigmoid(gbr_ref[...]) * yb + _sigmoid(gcr_ref[...]) * yc
    hn = h_ref[...] + g1_ref[0] * _dot(pre.astype(BF16), wout_ref[...])
    hout_ref[...] = hn
    n2 = _rms(hn) * n2g_ref[...] * (1.0 + sc2_ref[0]) + sh2_ref[0]
    n2_ref[...] = n2.astype(BF16)
    lgt_ref[...] = _dot_nt(wrt_ref[...], n2, precision=lax.Precision.HIGHEST)


def _mix_call(o_a, o_c, z, h, w_conv, wa, wb, wc, wout, wrt, n2g, ada, row0, tiles_per_row, tiles_per_seq):
    m = h.shape[0]
    tm = TOK_TILE
    nt = m // tm
    r8 = tm // 8
    zblk = lambda blk: pl.BlockSpec((tm, COL_BLK), lambda i: (i, blk))
    zprev = lambda blk: pl.BlockSpec((8, COL_BLK), lambda i: (jnp.maximum(i * r8 - 1, 0), blk))
    znext = lambda blk: pl.BlockSpec((8, COL_BLK), lambda i: (jnp.minimum((i + 1) * r8, m // 8 - 1), blk))
    zgate = lambda blk: pl.BlockSpec((tm, D_MODEL), lambda i: (i, blk // 2))
    full = lambda a: pl.BlockSpec(a.shape, lambda i: (0,) * a.ndim)
    tok = lambda wdt: pl.BlockSpec((tm, wdt), lambda i: (i, 0))
    return pl.pallas_call(
        functools.partial(_mix_kernel, tiles_per_seq=tiles_per_seq),
        grid=(nt,),
        in_specs=[
            tok(D_A), tok(D_A), zblk(B_U), zblk(B_GB), zblk(B_GC),
            zprev(B_U), zprev(B_GC), znext(B_U), znext(B_GC),
            zgate(B_GA), zgate(B_GBR), zgate(B_GCR), tok(D_MODEL),
            full(w_conv), full(wa), full(wb), full(wc), full(wout), full(wrt),
            _ada_spec(2, row0, tiles_per_row), full(n2g),
            _ada_spec(4, row0, tiles_per_row), _ada_spec(3, row0, tiles_per_row),
        ],
        out_specs=[tok(D_MODEL), tok(D_MODEL), pl.BlockSpec((N_EXPERTS, tm), lambda i: (0, i))],
        out_shape=[jax.ShapeDtypeStruct((m, D_MODEL), F32),
                   jax.ShapeDtypeStruct((m, D_MODEL), BF16),
                   jax.ShapeDtypeStruct((N_EXPERTS, m), F32)],
        compiler_params=_cparams(("arbitrary",)),
        name="mix",
    )(o_a, o_c, z, z, z, z, z, z, z, z, z, z, h, w_conv, wa, wb, wc, wout, wrt, ada, n2g, ada, ada)


def _first_max_mask(x, axes_iota, size):
    m = x
    for ax in range(x.ndim - 1):
        m = jnp.max(m, axis=ax, keepdims=True)
    first = jnp.where(x == m, axes_iota, size)
    f = first
    for ax in range(x.ndim - 1):
        f = jnp.min(f, axis=ax, keepdims=True)
    return axes_iota == f


def _routing_kernel(lgt_ref, bias_ref, o_ref):
    tn = lgt_ref.shape[-1]
    s = _sigmoid(lgt_ref[...])
    sel = (s + bias_ref[...]).reshape(N_GROUPS, GROUP_SIZE, tn)
    s = s.reshape(N_GROUPS, GROUP_SIZE, tn)
    neg_inf = -jnp.inf
    mem_iota = lax.broadcasted_iota(jnp.int32, sel.shape, 1)
    m1 = jnp.max(sel, axis=1, keepdims=True)
    first = jnp.min(jnp.where(sel == m1, mem_iota, GROUP_SIZE), axis=1, keepdims=True)
    m2 = jnp.max(jnp.where(mem_iota == first, neg_inf, sel), axis=1, keepdims=True)
    grp = (m1 + m2).reshape(N_GROUPS, tn)
    g_iota = lax.broadcasted_iota(jnp.int32, grp.shape, 0)
    gmask = jnp.zeros(grp.shape, jnp.bool_)
    for _ in range(TOPK_GROUPS):
        pick = _first_max_mask(grp, g_iota, N_GROUPS)
        gmask = gmask | pick
        grp = jnp.where(pick, neg_inf, grp)
    cand = jnp.where(gmask.reshape(N_GROUPS, 1, tn), sel, NEG_BIG)
    e_iota = lax.broadcasted_iota(jnp.int32, sel.shape, 0) * GROUP_SIZE + mem_iota
    chosen = jnp.zeros(sel.shape, jnp.bool_)
    for _ in range(TOP_K):
        pick = _first_max_mask(cand, e_iota, N_EXPERTS)
        chosen = chosen | pick
        cand = jnp.where(pick, neg_inf, cand)
    w = jnp.where(chosen, s, 0.0)
    tot = jnp.sum(jnp.sum(w, axis=1, keepdims=True), axis=0, keepdims=True)
    w = (w / tot * ROUTED_SCALE).reshape(N_EXPERTS, tn)
    o_ref[...] = jnp.concatenate([w, jnp.zeros_like(w)], axis=0).T


def _routing_call(lgt, b_router_l):
    m = lgt.shape[1]
    tn = 256
    return pl.pallas_call(
        _routing_kernel,
        grid=(m // tn,),
        in_specs=[pl.BlockSpec((N_EXPERTS, tn), lambda i: (0, i)),
                  pl.BlockSpec((N_EXPERTS, tn), lambda i: (0, 0))],
        out_specs=pl.BlockSpec((tn, COMB_W), lambda i: (i, 0)),
        out_shape=jax.ShapeDtypeStruct((m, COMB_W), F32),
        compiler_params=_cparams(("arbitrary",)),
        name="routing",
    )(lgt, jnp.broadcast_to(b_router_l[:, None], (N_EXPERTS, tn)))


def _moe_kernel(x_ref, comb_ref, wg_ref, wu_ref, wd_ref, wsg_ref, wsu_ref, wsd_ref, h_ref, g2_ref,
                o_ref, acc_ref):
    e = pl.program_id(1)
    x = x_ref[...]

    @pl.when(e == 0)
    def _():
        sg = _dot(x, wsg_ref[...])
        su = _dot(x, wsu_ref[...])
        acc_ref[...] = _dot((_silu(sg) * su).astype(BF16), wsd_ref[...])

    hg = _dot(x, wg_ref[...].astype(BF16))
    hu = _dot(x, wu_ref[...].astype(BF16))
    comb = comb_ref[...]
    lane = lax.broadcasted_iota(jnp.int32, comb.shape, 1)
    c = jnp.sum(jnp.where(lane == e, comb, 0.0), axis=1, keepdims=True)
    hid = (_silu(hg) * hu * c).astype(BF16)
    acc_ref[...] += _dot(hid, wd_ref[...].astype(BF16))

    @pl.when(e == N_EXPERTS - 1)
    def _():
        o_ref[...] = h_ref[...] + g2_ref[0] * acc_ref[...]


def _moe_call(x, comb, w_gate, w_up, w_down, wsg, wsu, wsd, h, ada, layer, row0, toks_per_row):
    m = x.shape[0]
    tm = 1024
    tiles_per_row = toks_per_row // tm
    wspec = lambda k, n: pl.BlockSpec((None, None, k, n), lambda i, e: (layer, e, 0, 0))
    full = lambda a: pl.BlockSpec(a.shape, lambda i, e: (0,) * a.ndim)
    tok = lambda wdt: pl.BlockSpec((tm, wdt), lambda i, e: (i, 0))
    return pl.pallas_call(
        _moe_kernel,
        grid=(m // tm, N_EXPERTS),
        in_specs=[
            tok(D_MODEL), tok(COMB_W),
            wspec(D_MODEL, D_EXPERT), wspec(D_MODEL, D_EXPERT), wspec(D_EXPERT, D_MODEL),
            full(wsg), full(wsu), full(wsd), tok(D_MODEL),
            pl.BlockSpec((1, 1, D_MODEL), lambda i, e: (row0 + i // tiles_per_row, 0, 5)),
        ],
        out_specs=tok(D_MODEL),
        out_shape=jax.ShapeDtypeStruct((m, D_MODEL), F32),
        scratch_shapes=[pltpu.VMEM((tm, D_MODEL), F32)],
        compiler_params=_cparams(("arbitrary", "arbitrary")),
        name="moe",
    )(x, comb, w_gate, w_up, w_down, wsg, wsu, wsd, h, ada)


def _stream(h, p, ada, row0, bsz, seq, attend, states):
    toks_per_row = seq if row0 > 0 else bsz * seq
    tiles_per_row = toks_per_row // TOK_TILE
    tiles_per_seq = seq // TOK_TILE
    keys, vals, sts = [], [], []
    for l in range(DEPTH):
        ada_l = ada[l].reshape(8, 1, N_MOD * D_MODEL)
        n = _norm_mod_call(h, p['norm1_g'][l:l + 1], ada_l, 1, 0, row0, tiles_per_row)
        z = _in_proj_call(n, p['w_in'], l)
        o_a = attend(z, l)
        o_c, st = _hgrn_call(z, p['lb_fwd'], p['lb_bwd'], p['hgrn_norm_g'][l:l + 1], states[l], bsz, seq, l)
        h, n2, lgt = _mix_call(o_a, o_c, z, h, p['w_conv'][l], p['w_br_a'][l], p['w_br_b'][l], p['w_br_c'][l],
                               p['w_out'][l], p['w_router_t'][l], p['norm2_g'][l:l + 1], ada_l,
                               row0, tiles_per_row, tiles_per_seq)
        comb = _routing_call(lgt, p['b_router'][l])
        h = _moe_call(n2, comb, p['w_gate'], p['w_up'], p['w_down'], p['w_sh_gate'][l], p['w_sh_up'][l],
                      p['w_sh_down'][l], h, ada_l, l, row0, toks_per_row)
        keys.append(z[:, B_KA * COL_BLK:(B_KA + 1) * COL_BLK])
        vals.append(z[:, B_VA * COL_BLK:(B_VA + 1) * COL_BLK])
        sts.append(st)
    return h, keys, vals, sts


def kernel(x_prompt, x_sample, cache_k, cache_v, state_hgrn, c, c_ctx, norm1_g, norm2_g, w_ada, b_ada,
           w_in, rpb, w_conv, lb_fwd, lb_bwd, hgrn_norm_g, w_br_a, w_br_b, w_br_c, w_out, w_router,
           b_router, w_gate, w_up, w_down, w_sh_gate, w_sh_up, w_sh_down, final_norm_g):
    batch, seq, _ = x_prompt.shape
    dec_batch, dec_seq, _ = x_sample.shape
    past = cache_k.shape[2]
    p = dict(norm1_g=norm1_g, norm2_g=norm2_g, w_in=w_in, w_conv=w_conv, lb_fwd=lb_fwd, lb_bwd=lb_bwd,
             hgrn_norm_g=hgrn_norm_g,
             w_br_a=w_br_a.astype(BF16), w_br_b=w_br_b.astype(BF16), w_br_c=w_br_c.astype(BF16),
             w_out=w_out.astype(BF16), w_router_t=jnp.swapaxes(w_router, 1, 2), b_router=b_router,
             w_gate=w_gate, w_up=w_up, w_down=w_down,
             w_sh_gate=w_sh_gate.astype(BF16), w_sh_up=w_sh_up.astype(BF16), w_sh_down=w_sh_down.astype(BF16))

    c8 = jnp.zeros((8, D_MODEL), F32).at[0].set(c_ctx).at[1:1 + dec_batch].set(c)
    ada = _ada_call(c8, w_ada, b_ada)

    zero_states = [jnp.zeros((batch, 2, H_C, DK_C, DK_C), F32)] * DEPTH
    ctx_attend = lambda z, l: _ctx_attn_call(z, seq)
    h_ctx, keys, vals, sts = _stream(x_prompt.reshape(batch * seq, D_MODEL), p, ada, 0, batch, seq,
                                     ctx_attend, zero_states)
    y_prompt = _final_norm_call(h_ctx, final_norm_g.reshape(1, D_MODEL)).reshape(batch, seq, D_MODEL)
    new_cache_k = jnp.stack([k.reshape(batch, seq, H_A, DH_A) for k in keys], axis=1)
    new_cache_v = jnp.stack([v.reshape(batch, seq, H_A, DH_A) for v in vals], axis=1)
    new_state = jnp.stack(sts, axis=1)

    rows = dec_seq // GRID_W
    ck = cache_k.reshape(dec_batch, DEPTH, past, D_A)
    cv = cache_v.reshape(dec_batch, DEPTH, past, D_A)
    lat_states = [state_hgrn[:, l].astype(F32) for l in range(DEPTH)]

    def lat_attend(z, l):
        bias = _na_bias_tables(rpb[l], rows)
        return _na_attn_call(z, ck[:, l], cv[:, l], bias, dec_batch, dec_seq)

    h_lat, _, _, _ = _stream(x_sample.reshape(dec_batch * dec_seq, D_MODEL), p, ada, 1, dec_batch, dec_seq,
                             lat_attend, lat_states)
    y_sample = _final_norm_call(h_lat, final_norm_g.reshape(1, D_MODEL)).reshape(dec_batch, dec_seq, D_MODEL)
    return (y_prompt, y_sample, new_cache_k, new_cache_v, new_state)
```

```python
import functools

import numpy as np
import jax
import jax.numpy as jnp
from jax import lax
from jax.experimental import pallas as pl
from jax.experimental.pallas import tpu as pltpu

F32 = jnp.float32
BF16 = jnp.bfloat16

D_MODEL = 1024
DEPTH = 2
GRID_W = 64
H_A = 8
DH_A = 64
D_A = H_A * DH_A
WIN_H = 8
WIN_W = 16
D_CONV = 512
H_C = 4
DK_C = 128
CHUNK = 16
N_EXPERTS = 64
N_GROUPS = 8
GROUP_SIZE = N_EXPERTS // N_GROUPS
TOPK_GROUPS = 4
TOP_K = 8
D_EXPERT = 256
ROUTED_SCALE = 2.5
N_MOD = 6
EPS = 1e-6
NEG_BIG = -1e30
LOG_FLOOR = 1e-30

D_IN = 8704
COL_BLK = 512
N_COL_BLKS = D_IN // COL_BLK
SRC_GATE_BLK = 11
N_GATE_BLKS = 6
B_GA, B_GBR, B_GCR = 0, 2, 4
B_QA, B_KA, B_VA, B_U, B_GB, B_GC, B_QC, B_ZF, B_ZB, B_IC, B_OC = range(6, 17)

NA_QROWS = 4
NA_KROWS = 12
NA_TQ = NA_QROWS * GRID_W
NA_TK = NA_KROWS * GRID_W

TOK_TILE = 256
VMEM_LIMIT = 56 * 1024 * 1024


def _cparams(sem):
    return pltpu.CompilerParams(dimension_semantics=sem, vmem_limit_bytes=VMEM_LIMIT)


def _sigmoid(x):
    return 1.0 / (1.0 + jnp.exp(-x))


def _silu(x):
    return x * _sigmoid(x)


def _dot(a, b):
    return jnp.dot(a, b, preferred_element_type=F32)


def _dot_nt(a, b, precision=None):
    return lax.dot_general(a, b, (((1,), (1,)), ((), ())), preferred_element_type=F32,
                           precision=precision)


def _ada_kernel(c_ref, w_ref, b_ref, o_ref):
    a = _silu(c_ref[...])
    o_ref[0] = _dot(a, w_ref[0]) + b_ref[0]


def _ada_call(c8, w_ada, b_ada):
    tn = 1536
    n_out = N_MOD * D_MODEL
    return pl.pallas_call(
        _ada_kernel,
        grid=(DEPTH, n_out // tn),
        in_specs=[
            pl.BlockSpec((8, D_MODEL), lambda l, j: (0, 0)),
            pl.BlockSpec((1, D_MODEL, tn), lambda l, j: (l, 0, j)),
            pl.BlockSpec((1, 1, tn), lambda l, j: (l, 0, j)),
        ],
        out_specs=pl.BlockSpec((1, 8, tn), lambda l, j: (l, 0, j)),
        out_shape=jax.ShapeDtypeStruct((DEPTH, 8, n_out), F32),
        compiler_params=_cparams(("arbitrary", "arbitrary")),
        name="ada",
    )(c8, w_ada, b_ada.reshape(DEPTH, 1, n_out))


def _rms(x):
    return x * lax.rsqrt(jnp.mean(x * x, axis=-1, keepdims=True) + EPS)


def _norm_mod_kernel(h_ref, g_ref, sc_ref, sh_ref, o_ref):
    y = _rms(h_ref[...]) * g_ref[...]
    o_ref[...] = (y * (1.0 + sc_ref[0]) + sh_ref[0]).astype(o_ref.dtype)


def _ada_spec(mod_idx, row0, tiles_per_row):
    return pl.BlockSpec((1, 1, D_MODEL), lambda i: (row0 + i // tiles_per_row, 0, mod_idx))


def _norm_mod_call(h, g, ada, sc_idx, sh_idx, row0, tiles_per_row):
    m = h.shape[0]
    return pl.pallas_call(
        _norm_mod_kernel,
        grid=(m // TOK_TILE,),
        in_specs=[
            pl.BlockSpec((TOK_TILE, D_MODEL), lambda i: (i, 0)),
            pl.BlockSpec((1, D_MODEL), lambda i: (0, 0)),
            _ada_spec(sc_idx, row0, tiles_per_row),
            _ada_spec(sh_idx, row0, tiles_per_row),
        ],
        out_specs=pl.BlockSpec((TOK_TILE, D_MODEL), lambda i: (i, 0)),
        out_shape=jax.ShapeDtypeStruct((m, D_MODEL), BF16),
        compiler_params=_cparams(("arbitrary",)),
        name="norm_mod",
    )(h, g, ada, ada)


def _final_norm_kernel(h_ref, g_ref, o_ref):
    o_ref[...] = _rms(h_ref[...]) * g_ref[...]


def _final_norm_call(h, g):
    m = h.shape[0]
    return pl.pallas_call(
        _final_norm_kernel,
        grid=(m // TOK_TILE,),
        in_specs=[
            pl.BlockSpec((TOK_TILE, D_MODEL), lambda i: (i, 0)),
            pl.BlockSpec((1, D_MODEL), lambda i: (0, 0)),
        ],
        out_specs=pl.BlockSpec((TOK_TILE, D_MODEL), lambda i: (i, 0)),
        out_shape=jax.ShapeDtypeStruct((m, D_MODEL), F32),
        compiler_params=_cparams(("arbitrary",)),
        name="final_norm",
    )(h, g)


def _in_proj_kernel(a_ref, w_ref, o_ref):
    o_ref[...] = _dot(a_ref[...], w_ref[...].astype(BF16))


def _src_col_blk(j):
    return jnp.where(j < N_GATE_BLKS, j + SRC_GATE_BLK, j - N_GATE_BLKS)


def _in_proj_call(n, w_in, layer):
    m = n.shape[0]
    tm = 1024
    return pl.pallas_call(
        _in_proj_kernel,
        grid=(N_COL_BLKS, m // tm),
        in_specs=[
            pl.BlockSpec((tm, D_MODEL), lambda j, i: (i, 0)),
            pl.BlockSpec((None, D_MODEL, COL_BLK), lambda j, i: (layer, 0, _src_col_blk(j))),
        ],
        out_specs=pl.BlockSpec((tm, COL_BLK), lambda j, i: (i, j)),
        out_shape=jax.ShapeDtypeStruct((m, D_IN), F32),
        compiler_params=_cparams(("arbitrary", "arbitrary")),
        name="in_proj",
    )(n, w_in)


def _softmax_pv(s_list, v_list):
    m = s_list[0].max(axis=-1, keepdims=True)
    for s in s_list[1:]:
        m = jnp.maximum(m, s.max(axis=-1, keepdims=True))
    num = None
    den = None
    for s, v in zip(s_list, v_list):
        p = jnp.exp(s - m)
        d = p.sum(axis=-1, keepdims=True)
        o = _dot(p.astype(BF16), v.astype(BF16))
        num = o if num is None else num + o
        den = d if den is None else den + d
    return num / den


def _ctx_attn_kernel(q_ref, k_ref, v_ref, o_ref):
    scale = DH_A ** -0.5
    for h in range(H_A):
        sl = slice(h * DH_A, (h + 1) * DH_A)
        s = _dot_nt(q_ref[:, sl].astype(BF16), k_ref[:, sl].astype(BF16)) * scale
        o_ref[:, sl] = _softmax_pv([s], [v_ref[:, sl]])


def _ctx_attn_call(z, seq):
    m = z.shape[0]
    spec = lambda blk: pl.BlockSpec((seq, D_A), lambda b: (b, blk))
    return pl.pallas_call(
        _ctx_attn_kernel,
        grid=(m // seq,),
        in_specs=[spec(B_QA), spec(B_KA), spec(B_VA)],
        out_specs=pl.BlockSpec((seq, D_A), lambda b: (b, 0)),
        out_shape=jax.ShapeDtypeStruct((m, D_A), F32),
        compiler_params=_cparams(("arbitrary",)),
        name="ctx_attn",
    )(z, z, z)


def _na_key_row0(rb, rows):
    return jnp.clip(NA_QROWS * rb - (NA_KROWS - WIN_H) , 0, rows - NA_KROWS)


def _na_attn_kernel(q_ref, k_ref, v_ref, ck_ref, cv_ref, bias_ref, o_ref, *, rows):
    scale = DH_A ** -0.5
    rb = pl.program_id(1)
    k0 = pl.multiple_of(_na_key_row0(rb, rows) * GRID_W, GRID_W)
    for h in range(H_A):
        sl = slice(h * DH_A, (h + 1) * DH_A)
        q = q_ref[:, sl].astype(BF16)
        kw = k_ref[pl.ds(k0, NA_TK), sl]
        vw = v_ref[pl.ds(k0, NA_TK), sl]
        s_lat = _dot_nt(q, kw.astype(BF16)) * scale + bias_ref[h]
        s_ctx = _dot_nt(q, ck_ref[:, sl].astype(BF16)) * scale
        o_ref[:, sl] = _softmax_pv([s_lat, s_ctx], [vw, cv_ref[:, sl]])


def _na_bias_pattern(rb, n_rb):
    return jnp.where(rb == 0, 0, jnp.where(rb == n_rb - 1, 2, 1))


def _na_bias_tables(rpb_l, rows):
    n_rb = rows // NA_QROWS
    qc = np.arange(GRID_W)
    q_start = np.clip(qc - WIN_W // 2, 0, GRID_W - WIN_W)
    kc = np.arange(GRID_W)
    valid_c = (kc[None, :] >= q_start[:, None]) & (kc[None, :] < q_start[:, None] + WIN_W)
    pad = GRID_W - WIN_W
    rpb_pad = jnp.pad(rpb_l.astype(F32), ((0, 0), (0, 0), (pad, pad)))
    col_tab = jnp.stack([rpb_pad[:, :, GRID_W - 1 - c:2 * GRID_W - 1 - c] for c in range(GRID_W)], axis=2)
    col_tab = jnp.where(valid_c, col_tab, NEG_BIG)
    masked = jnp.full((H_A, GRID_W, GRID_W), NEG_BIG, F32)
    tabs = []
    for rb in (0, 1, n_rb - 1):
        k_row0 = int(np.clip(NA_QROWS * rb - (NA_KROWS - WIN_H), 0, rows - NA_KROWS))
        q_rows = []
        for i in range(NA_QROWS):
            r = NA_QROWS * rb + i
            w0 = int(np.clip(r - WIN_H // 2, 0, rows - WIN_H))
            blocks = []
            for j in range(NA_KROWS):
                kr = k_row0 + j
                in_window = w0 <= kr < w0 + WIN_H
                blocks.append(col_tab[:, kr - r + WIN_H - 1] if in_window else masked)
            q_rows.append(jnp.concatenate(blocks, axis=2))
        tabs.append(jnp.concatenate(q_rows, axis=1))
    return jnp.stack(tabs)


def _na_attn_call(z, ck, cv, bias, bsz, seq):
    rows = seq // GRID_W
    n_rb = rows // NA_QROWS
    m = z.shape[0]
    kv_spec = lambda blk: pl.BlockSpec((seq, D_A), lambda b, r: (b, blk))
    c_spec = pl.BlockSpec((None, ck.shape[1], D_A), lambda b, r: (b, 0, 0))
    return pl.pallas_call(
        functools.partial(_na_attn_kernel, rows=rows),
        grid=(bsz, n_rb),
        in_specs=[
            pl.BlockSpec((NA_TQ, D_A), lambda b, r: (b * n_rb + r, B_QA)),
            kv_spec(B_KA), kv_spec(B_VA), c_spec, c_spec,
            pl.BlockSpec((None, H_A, NA_TQ, NA_TK), lambda b, r: (_na_bias_pattern(r, n_rb), 0, 0, 0)),
        ],
        out_specs=pl.BlockSpec((NA_TQ, D_A), lambda b, r: (b * n_rb + r, 0)),
        out_shape=jax.ShapeDtypeStruct((m, D_A), F32),
        compiler_params=_cparams(("arbitrary", "arbitrary")),
        name="na_attn",
    )(z, z, z, ck, cv, bias)


def _log1p(x):
    return jnp.log1p(x)


def _hgrn_gates(z, lb):
    log_sig = jnp.minimum(z, 0.0) - _log1p(jnp.exp(-jnp.abs(z)))
    a = _log1p(-lb) + log_sig
    b = jnp.log(jnp.maximum(lb, LOG_FLOOR))
    logf = jnp.maximum(a, b) + _log1p(jnp.exp(-jnp.abs(a - b)))
    key = (1.0 - lb) * _sigmoid(-z)
    return logf, key


def _lower_bound(p_ref, layer):
    p = p_ref[...]
    e = jnp.exp(p - jnp.max(p, axis=0, keepdims=True))
    sm = e / jnp.sum(e, axis=0, keepdims=True)
    lb = jnp.zeros((1, DK_C), F32)
    for j in range(1, layer + 1):
        lb = lb + sm[j:j + 1]
    return lb


def _chunk_cumsum(x, seq, reverse):
    pos = lax.broadcasted_iota(jnp.int32, x.shape, 0) % CHUNK
    s = 1
    while s < CHUNK:
        if reverse:
            x = x + jnp.where(pos < CHUNK - s, pltpu.roll(x, seq - s, axis=0), 0.0)
        else:
            x = x + jnp.where(pos >= s, pltpu.roll(x, s, axis=0), 0.0)
        s *= 2
    return x


HALF = CHUNK // 2
HGRN_UNROLL = 4


def _hgrn_chunk(q_ref, v_ref, b_s, k_s, o_s, r0, st, reverse):
    b = b_s[pl.ds(r0, CHUNK), :]
    q = q_ref[pl.ds(r0, CHUNK), :]
    k = k_s[pl.ds(r0, CHUNK), :]
    v = v_ref[pl.ds(r0, CHUNK), :]
    o = _dot_nt(q * jnp.exp(b), st)
    t_iota = lax.broadcasted_iota(jnp.int32, (HALF, 1), 0)
    o_half = [o[:HALF], o[HALF:]]
    for s in range(CHUNK):
        b_row = b_s[pl.ds(r0 + s, 1), :]
        k_row = k_s[pl.ds(r0 + s, 1), :]
        v_row = v_ref[pl.ds(r0 + s, 1), :]
        for half in range(2):
            t0 = half * HALF
            if (t0 + HALF - 1 < s) if not reverse else (t0 > s):
                continue
            sl = slice(t0, t0 + HALF)
            decay = jnp.exp(jnp.minimum(b[sl] - b_row, 0.0))
            col = jnp.sum(q[sl] * (k_row * decay), axis=-1, keepdims=True)
            if not ((t0 >= s) if not reverse else (t0 + HALF - 1 <= s)):
                keep = (t_iota + t0 >= s) if not reverse else (t_iota + t0 <= s)
                col = jnp.where(keep, col, 0.0)
            o_half[half] = o_half[half] + col * v_row
    o_s[pl.ds(r0, CHUNK), :] = jnp.concatenate(o_half, axis=0)
    edge = 0 if reverse else CHUNK - 1
    b_edge = b[edge:edge + 1, :]
    kh = k * jnp.exp(b_edge - b)
    kv_t = lax.dot_general(v, kh, (((0,), (0,)), ((), ())), preferred_element_type=F32)
    return st * jnp.exp(b_edge) + kv_t


def _hgrn_kernel(q_ref, zf_ref, zb_ref, i_ref, og_ref, lbf_ref, lbb_ref, g_ref, s0_ref,
                 o_ref, sfin_ref, of_s, ob_s, bf_s, bb_s, kf_s, kb_s, *, seq, layer):
    n_chunks = seq // CHUNK
    logf, key = _hgrn_gates(zf_ref[...], _lower_bound(lbf_ref, layer))
    bf_s[...] = _chunk_cumsum(logf, seq, False)
    kf_s[...] = key
    logf, key = _hgrn_gates(zb_ref[...], _lower_bound(lbb_ref, layer))
    bb_s[...] = _chunk_cumsum(logf, seq, True)
    kb_s[...] = key

    def body(it, carry):
        st_f, st_b = carry
        for u in range(HGRN_UNROLL):
            n = it * HGRN_UNROLL + u
            st_f = _hgrn_chunk(q_ref, i_ref, bf_s, kf_s, of_s, pl.multiple_of(n * CHUNK, CHUNK), st_f, False)
            st_b = _hgrn_chunk(q_ref, i_ref, bb_s, kb_s, ob_s,
                               pl.multiple_of((n_chunks - 1 - n) * CHUNK, CHUNK), st_b, True)
        return st_f, st_b

    st_f, st_b = lax.fori_loop(0, n_chunks // HGRN_UNROLL, body, (s0_ref[0].T, s0_ref[1].T))
    sfin_ref[0] = st_f.T
    sfin_ref[1] = st_b.T
    o_ref[...] = _rms(of_s[...] + ob_s[...]) * g_ref[...] * _silu(og_ref[...])


def _hgrn_call(z, lb_fwd, lb_bwd, norm_g, s0, bsz, seq, layer):
    m = z.shape[0]
    w = DK_C
    per = COL_BLK // w
    zspec = lambda blk: pl.BlockSpec((seq, w), lambda b, h: (b, blk * per + h))
    st_spec = pl.BlockSpec((None, 2, None, DK_C, DK_C), lambda b, h: (b, 0, h, 0, 0))
    return pl.pallas_call(
        functools.partial(_hgrn_kernel, seq=seq, layer=layer),
        grid=(bsz, H_C),
        in_specs=[
            zspec(B_QC), zspec(B_ZF), zspec(B_ZB), zspec(B_IC), zspec(B_OC),
            pl.BlockSpec((DEPTH, w), lambda b, h: (0, h)),
            pl.BlockSpec((DEPTH, w), lambda b, h: (0, h)),
            pl.BlockSpec((1, w), lambda b, h: (0, h)),
            st_spec,
        ],
        out_specs=[pl.BlockSpec((seq, w), lambda b, h: (b, h)), st_spec],
        out_shape=[jax.ShapeDtypeStruct((m, H_C * w), F32),
                   jax.ShapeDtypeStruct((bsz, 2, H_C, DK_C, DK_C), F32)],
        scratch_shapes=[pltpu.VMEM((seq, w), F32)] * 6,
        compiler_params=_cparams(("arbitrary", "arbitrary")),
        name="hgrn",
    )(z, z, z, z, z, lb_fwd, lb_bwd, norm_g, s0)


def _mix_kernel(oa_ref, oc_ref, u_ref, gb_ref, gc_ref, up_ref, gcp_ref, un_ref, gcn_ref,
                ga_ref, gbr_ref, gcr_ref, h_ref, wconv_ref, wa_ref, wb_ref, wc_ref, wout_ref,
                wrt_ref, g1_ref, n2g_ref, sc2_ref, sh2_ref,
                hout_ref, n2_ref, lgt_ref, *, tiles_per_seq):
    i = pl.program_id(0)
    tm = u_ref.shape[0]
    pos = i % tiles_per_seq
    zc = gc_ref[...] * u_ref[...]
    z_prev = jnp.where(pos == 0, 0.0, gcp_ref[7:8, :] * up_ref[7:8, :])
    z_next = jnp.where(pos == tiles_per_seq - 1, 0.0, gcn_ref[0:1, :] * un_ref[0:1, :])
    row = lax.broadcasted_iota(jnp.int32, zc.shape, 0)
    zp = jnp.where(row == 0, z_prev, pltpu.roll(zc, 1, axis=0))
    zn = jnp.where(row == tm - 1, z_next, pltpu.roll(zc, tm - 1, axis=0))
    w = wconv_ref[...]
    ob = gb_ref[...] * (w[0:1] * zp + w[1:2] * zc + w[2:3] * zn)
    ya = _dot(oa_ref[...].astype(BF16), wa_ref[...])
    yb = _dot(ob.astype(BF16), wb_ref[...])
    yc = _dot(oc_ref[...].astype(BF16), wc_ref[...])
    pre = _sigmoid(ga_ref[...]) * ya + _sigmoid(gbr_ref[...]) * yb + _sigmoid(gcr_ref[...]) * yc
    hn = h_ref[...] + g1_ref[0] * _dot(pre.astype(BF16), wout_ref[...])
    hout_ref[...] = hn
    n2 = _rms(hn) * n2g_ref[...] * (1.0 + sc2_ref[0]) + sh2_ref[0]
    n2_ref[...] = n2
    lgt_ref[...] = _dot_nt(wrt_ref[...], n2, precision=lax.Precision.HIGHEST)


def _mix_call(o_a, o_c, z, h, w_conv, wa, wb, wc, wout, wrt, n2g, ada, row0, tiles_per_row, tiles_per_seq):
    m = h.shape[0]
    tm = TOK_TILE
    nt = m // tm
    r8 = tm // 8
    zblk = lambda blk: pl.BlockSpec((tm, COL_BLK), lambda i: (i, blk))
    zprev = lambda blk: pl.BlockSpec((8, COL_BLK), lambda i: (jnp.maximum(i * r8 - 1, 0), blk))
    znext = lambda blk: pl.BlockSpec((8, COL_BLK), lambda i: (jnp.minimum((i + 1) * r8, m // 8 - 1), blk))
    zgate = lambda blk: pl.BlockSpec((tm, D_MODEL), lambda i: (i, blk // 2))
    full = lambda a: pl.BlockSpec(a.shape, lambda i: (0,) * a.ndim)
    tok = lambda wdt: pl.BlockSpec((tm, wdt), lambda i: (i, 0))
    return pl.pallas_call(
        functools.partial(_mix_kernel, tiles_per_seq=tiles_per_seq),
        grid=(nt,),
        in_specs=[
            tok(D_A), tok(D_A), zblk(B_U), zblk(B_GB), zblk(B_GC),
            zprev(B_U), zprev(B_GC), znext(B_U), znext(B_GC),
            zgate(B_GA), zgate(B_GBR), zgate(B_GCR), tok(D_MODEL),
            full(w_conv), full(wa), full(wb), full(wc), full(wout), full(wrt),
            _ada_spec(2, row0, tiles_per_row), full(n2g),
            _ada_spec(4, row0, tiles_per_row), _ada_spec(3, row0, tiles_per_row),
        ],
        out_specs=[tok(D_MODEL), tok(D_MODEL), pl.BlockSpec((N_EXPERTS, tm), lambda i: (0, i))],
        out_shape=[jax.ShapeDtypeStruct((m, D_MODEL), F32),
                   jax.ShapeDtypeStruct((m, D_MODEL), F32),
                   jax.ShapeDtypeStruct((N_EXPERTS, m), F32)],
        compiler_params=_cparams(("arbitrary",)),
        name="mix",
    )(o_a, o_c, z, z, z, z, z, z, z, z, z, z, h, w_conv, wa, wb, wc, wout, wrt, ada, n2g, ada, ada)


def _first_max(x, axes_iota, size):
    m = x
    for ax in range(x.ndim - 1):
        m = jnp.max(m, axis=ax, keepdims=True)
    first = jnp.where(x == m, axes_iota, size)
    f = first
    for ax in range(x.ndim - 1):
        f = jnp.min(f, axis=ax, keepdims=True)
    return axes_iota == f, f


def _first_max_mask(x, axes_iota, size):
    return _first_max(x, axes_iota, size)[0]


def _routing_kernel(lgt_ref, bias_ref, wpick_ref, eidx_ref, rank_ref, cnt_ref, carry_s):
    tn = lgt_ref.shape[-1]

    @pl.when(pl.program_id(0) == 0)
    def _():
        carry_s[...] = jnp.zeros(carry_s.shape, F32)

    s = _sigmoid(lgt_ref[...])
    sel = (s + bias_ref[...]).reshape(N_GROUPS, GROUP_SIZE, tn)
    s = s.reshape(N_GROUPS, GROUP_SIZE, tn)
    neg_inf = -jnp.inf
    mem_iota = lax.broadcasted_iota(jnp.int32, sel.shape, 1)
    m1 = jnp.max(sel, axis=1, keepdims=True)
    first = jnp.min(jnp.where(sel == m1, mem_iota, GROUP_SIZE), axis=1, keepdims=True)
    m2 = jnp.max(jnp.where(mem_iota == first, neg_inf, sel), axis=1, keepdims=True)
    grp = (m1 + m2).reshape(N_GROUPS, tn)
    g_iota = lax.broadcasted_iota(jnp.int32, grp.shape, 0)
    gmask = jnp.zeros(grp.shape, jnp.bool_)
    for _ in range(TOPK_GROUPS):
        pick = _first_max_mask(grp, g_iota, N_GROUPS)
        gmask = gmask | pick
        grp = jnp.where(pick, neg_inf, grp)
    cand = jnp.where(gmask.reshape(N_GROUPS, 1, tn), sel, NEG_BIG)
    e_iota = lax.broadcasted_iota(jnp.int32, sel.shape, 0) * GROUP_SIZE + mem_iota
    chosen = jnp.zeros(sel.shape, jnp.bool_)
    picked = []
    for _ in range(TOP_K):
        pick, idx = _first_max(cand, e_iota, N_EXPERTS)
        chosen = chosen | pick
        cand = jnp.where(pick, neg_inf, cand)
        picked.append(idx)
    w = jnp.where(chosen, s, 0.0)
    tot = jnp.sum(jnp.sum(w, axis=1, keepdims=True), axis=0, keepdims=True)
    w = w / tot * ROUTED_SCALE
    chosen_f = jnp.where(chosen, 1.0, 0.0).reshape(N_EXPERTS, tn)
    earlier = (lax.broadcasted_iota(jnp.int32, (tn, tn), 0) < lax.broadcasted_iota(jnp.int32, (tn, tn), 1))
    rank = _dot(chosen_f.astype(BF16), jnp.where(earlier, 1.0, 0.0).astype(BF16)) + carry_s[:, 0:1]
    rank = rank.reshape(N_GROUPS, GROUP_SIZE, tn)
    of_pick = lambda v, idx: jnp.sum(jnp.sum(jnp.where(e_iota == idx, v, 0.0), axis=1, keepdims=True),
                                     axis=0, keepdims=True).reshape(1, tn)
    eidx_ref[...] = jnp.concatenate([idx.reshape(1, tn) for idx in picked], axis=0)
    rank_ref[...] = jnp.concatenate([of_pick(rank, idx) for idx in picked], axis=0).astype(jnp.int32)
    wpick_ref[...] = jnp.concatenate([of_pick(w, idx) for idx in picked], axis=0)
    carry_s[...] = carry_s[...] + jnp.sum(chosen_f, axis=1, keepdims=True)
    cnt_ref[...] = carry_s[...]


def _routing_call(lgt, b_router_l):
    m = lgt.shape[1]
    tn = 256
    pick_spec = pl.BlockSpec((TOP_K, tn), lambda i: (0, i))
    cnt_spec = pl.BlockSpec((N_EXPERTS, 128), lambda i: (0, 0))
    return pl.pallas_call(
        _routing_kernel,
        grid=(m // tn,),
        in_specs=[pl.BlockSpec((N_EXPERTS, tn), lambda i: (0, i)),
                  pl.BlockSpec((N_EXPERTS, tn), lambda i: (0, 0))],
        out_specs=[pick_spec, pick_spec, pick_spec, cnt_spec],
        out_shape=[jax.ShapeDtypeStruct((TOP_K, m), F32),
                   jax.ShapeDtypeStruct((TOP_K, m), jnp.int32),
                   jax.ShapeDtypeStruct((TOP_K, m), jnp.int32),
                   jax.ShapeDtypeStruct((N_EXPERTS, 128), F32)],
        scratch_shapes=[pltpu.VMEM((N_EXPERTS, 128), F32)],
        compiler_params=_cparams(("arbitrary",)),
        name="routing",
    )(lgt, jnp.broadcast_to(b_router_l[:, None], (N_EXPERTS, tn)))


MOE_TM = 256
LANES = 128
CHUNKS = D_MODEL // LANES
ROW_GROUP = 16
assert TOP_K == CHUNKS and TOP_K & (TOP_K - 1) == 0


def _token_row(idx):
    return pl.multiple_of(idx & ~(TOP_K - 1), CHUNKS)


def _dispatch_plan(eidx, rank, cnt, n_tiles_max):
    counts = cnt[:, 0].astype(jnp.int32)
    tiles = (counts + MOE_TM - 1) // MOE_TM
    tile_end = jnp.cumsum(tiles)
    tile_start = tile_end - tiles
    n_tiles = tile_end[-1]
    experts = jnp.arange(N_EXPERTS, dtype=jnp.int32)
    base = jnp.sum(jnp.where(eidx[:, :, None] == experts, tile_start * MOE_TM, 0), axis=-1)
    pos = (rank + base).T.reshape(-1)
    ti = jnp.arange(n_tiles_max, dtype=jnp.int32)
    tile_e = jnp.sum((ti[:, None] >= tile_end[None, :]).astype(jnp.int32), axis=1)
    onehot = jnp.minimum(tile_e, N_EXPERTS - 1)[:, None] == experts
    pick = lambda v: jnp.sum(jnp.where(onehot, v, 0), axis=1)
    valid = jnp.clip(pick(counts) - (ti - pick(tile_start)) * MOE_TM, 0, MOE_TM)
    valid = jnp.where(ti < n_tiles, valid, 0).astype(jnp.int32)
    last_e = jnp.sum(jnp.where(ti == n_tiles - 1, tile_e, 0))
    tile_e = jnp.where(ti < n_tiles, tile_e, last_e).astype(jnp.int32)
    return pos, tile_e, valid


def _moe_kernel(te_ref, tv_ref, pos_ref, wpick_ref, x_hbm, fill_hbm, wg_ref, wu_ref, wd_ref, out_hbm,
                rowidx_s, x_v, acc_v, g_s, y_s, wg_bf, wu_bf, wd_bf, sem):
    i = pl.program_id(0)
    n_tok = x_hbm.shape[0] // CHUNKS
    zero_blk = 512

    @pl.when(i == 0)
    def _prologue():
        x_copy = pltpu.make_async_copy(x_hbm, x_v.at[pl.ds(0, n_tok * CHUNKS)], sem.at[0])
        fill_copy = pltpu.make_async_copy(fill_hbm, rowidx_s, sem.at[1])
        x_copy.start()
        fill_copy.start()

        def zero(j, carry):
            acc_v[pl.ds(pl.multiple_of(j * zero_blk, zero_blk), zero_blk), :] = jnp.zeros((zero_blk, LANES), F32)
            return carry

        lax.fori_loop(0, n_tok * CHUNKS // zero_blk, zero, 0)
        acc_v[pl.ds(n_tok * CHUNKS, CHUNKS), :] = jnp.zeros((CHUNKS, LANES), F32)
        x_v[pl.ds(n_tok * CHUNKS, CHUNKS), :] = jnp.zeros((CHUNKS, LANES), F32)
        fill_copy.wait()

        def invert(j, carry):
            for u in range(ROW_GROUP):
                idx = j * ROW_GROUP + u
                rowidx_s[pos_ref[idx]] = idx
            return carry

        lax.fori_loop(0, n_tok * TOP_K // ROW_GROUP, invert, 0)
        x_copy.wait()

    @pl.when(tv_ref[i] > 0)
    def _tile():
        e = te_ref[i]
        prev_e = te_ref[jnp.maximum(i - 1, 0)]

        @pl.when((i == 0) | (prev_e != e))
        def _cast():
            wg_bf[...] = wg_ref[...].astype(BF16)
            wu_bf[...] = wu_ref[...].astype(BF16)
            wd_bf[...] = wd_ref[...].astype(BF16)

        row0 = i * MOE_TM

        def gather(j, carry):
            for u in range(ROW_GROUP):
                r = j * ROW_GROUP + u
                g_s[pl.ds(pl.multiple_of(r * CHUNKS, CHUNKS), CHUNKS), :] = \
                    x_v[pl.ds(_token_row(rowidx_s[row0 + r]), CHUNKS), :]
            return carry

        lax.fori_loop(0, MOE_TM // ROW_GROUP, gather, 0)
        x = jnp.concatenate([g_s[pl.ds(c, MOE_TM, stride=CHUNKS), :] for c in range(CHUNKS)], axis=1)
        x = x.astype(BF16)
        hid = _silu(_dot(x, wg_bf[...])) * _dot(x, wu_bf[...])
        y = _dot(hid.astype(BF16), wd_bf[...])
        blocks_per_group = ROW_GROUP // 8
        for rb in range(MOE_TM // 8):
            for c in range(CHUNKS):
                y_s[rb // blocks_per_group, pl.ds(((rb % blocks_per_group) * CHUNKS + c) * 8, 8), :] = \
                    y[rb * 8:(rb + 1) * 8, c * LANES:(c + 1) * LANES]

        def scatter(j, carry):
            slots, rows_v = [], []
            for u in range(ROW_GROUP):
                idx = rowidx_s[row0 + j * ROW_GROUP + u]
                slots.append(_token_row(idx))
                start = (u // 8) * (8 * CHUNKS) + (u % 8)
                rows_v.append(y_s[j, pl.ds(start, CHUNKS, stride=8), :] * wpick_ref[idx])
            olds = [acc_v[pl.ds(s, CHUNKS), :] for s in slots]
            for s, o, v in zip(slots, olds, rows_v):
                acc_v[pl.ds(s, CHUNKS), :] = o + v
            return carry

        lax.fori_loop(0, MOE_TM // ROW_GROUP, scatter, 0)

    @pl.when(i == pl.num_programs(0) - 1)
    def _epilogue():
        out_copy = pltpu.make_async_copy(acc_v.at[pl.ds(0, n_tok * CHUNKS)], out_hbm, sem.at[2])
        out_copy.start()
        out_copy.wait()


def _moe_call(x, wpick, eidx, rank, cnt, w_gate, w_up, w_down, layer):
    m = x.shape[0]
    n_rows = TOP_K * m
    n_tiles_max = n_rows // MOE_TM + N_EXPERTS
    n_slots = n_tiles_max * MOE_TM
    pos, tile_e, valid = _dispatch_plan(eidx, rank, cnt, n_tiles_max)
    wflat = jnp.concatenate([wpick.T.reshape(-1), jnp.zeros((TOP_K,), F32)])
    wspec = lambda k, n: pl.BlockSpec((None, None, k, n), lambda i, te, tv: (layer, te[i], 0, 0))
    anyspec = pl.BlockSpec(memory_space=pl.ANY)
    smem = pl.BlockSpec(memory_space=pltpu.SMEM)
    grid_spec = pltpu.PrefetchScalarGridSpec(
        num_scalar_prefetch=2,
        grid=(n_tiles_max,),
        in_specs=[
            smem, smem, anyspec, anyspec,
            wspec(D_MODEL, D_EXPERT), wspec(D_MODEL, D_EXPERT), wspec(D_EXPERT, D_MODEL),
        ],
        out_specs=anyspec,
        scratch_shapes=[
            pltpu.SMEM((n_slots,), jnp.int32),
            pltpu.VMEM(((m + 1) * CHUNKS, LANES), F32),
            pltpu.VMEM(((m + 1) * CHUNKS, LANES), F32),
            pltpu.VMEM((MOE_TM * CHUNKS, LANES), F32),
            pltpu.VMEM((MOE_TM // ROW_GROUP, ROW_GROUP * CHUNKS, LANES), F32),
            pltpu.VMEM((D_MODEL, D_EXPERT), BF16),
            pltpu.VMEM((D_MODEL, D_EXPERT), BF16),
            pltpu.VMEM((D_EXPERT, D_MODEL), BF16),
            pltpu.SemaphoreType.DMA((3,)),
        ],
    )
    out = pl.pallas_call(
        _moe_kernel,
        grid_spec=grid_spec,
        out_shape=jax.ShapeDtypeStruct((m * CHUNKS, LANES), F32),
        compiler_params=_cparams(("arbitrary",)),
        name="moe",
    )(tile_e, valid, pos, wflat, x.reshape(m * CHUNKS, LANES),
      jnp.full((n_slots,), n_rows, jnp.int32), w_gate, w_up, w_down)
    return out.reshape(m, D_MODEL)


def _moe_finish_kernel(x_ref, routed_ref, h_ref, wsg_ref, wsu_ref, wsd_ref, g2_ref, o_ref):
    x = x_ref[...].astype(BF16)
    hid = _silu(_dot(x, wsg_ref[...])) * _dot(x, wsu_ref[...])
    shared = _dot(hid.astype(BF16), wsd_ref[...])
    o_ref[...] = h_ref[...] + g2_ref[0] * (routed_ref[...] + shared)


def _moe_finish_call(x, routed, h, wsg, wsu, wsd, ada, row0, tiles_per_row):
    m = x.shape[0]
    tok = pl.BlockSpec((TOK_TILE, D_MODEL), lambda i: (i, 0))
    full = lambda a: pl.BlockSpec(a.shape, lambda i: (0,) * a.ndim)
    return pl.pallas_call(
        _moe_finish_kernel,
        grid=(m // TOK_TILE,),
        in_specs=[tok, tok, tok, full(wsg), full(wsu), full(wsd), _ada_spec(5, row0, tiles_per_row)],
        out_specs=tok,
        out_shape=jax.ShapeDtypeStruct((m, D_MODEL), F32),
        compiler_params=_cparams(("arbitrary",)),
        name="moe_finish",
    )(x, routed, h, wsg, wsu, wsd, ada)


def _stream(h, p, ada, row0, bsz, seq, attend, states):
    toks_per_row = seq if row0 > 0 else bsz * seq
    tiles_per_row = toks_per_row // TOK_TILE
    tiles_per_seq = seq // TOK_TILE
    keys, vals, sts = [], [], []
    for l in range(DEPTH):
        ada_l = ada[l].reshape(8, 1, N_MOD * D_MODEL)
        n = _norm_mod_call(h, p['norm1_g'][l:l + 1], ada_l, 1, 0, row0, tiles_per_row)
        z = _in_proj_call(n, p['w_in'], l)
        o_a = attend(z, l)
        o_c, st = _hgrn_call(z, p['lb_fwd'], p['lb_bwd'], p['hgrn_norm_g'][l:l + 1], states[l], bsz, seq, l)
        h, n2, lgt = _mix_call(o_a, o_c, z, h, p['w_conv'][l], p['w_br_a'][l], p['w_br_b'][l], p['w_br_c'][l],
                               p['w_out'][l], p['w_router_t'][l], p['norm2_g'][l:l + 1], ada_l,
                               row0, tiles_per_row, tiles_per_seq)
        wpick, eidx, rank, cnt = _routing_call(lgt, p['b_router'][l])
        routed = _moe_call(n2, wpick, eidx, rank, cnt, p['w_gate'], p['w_up'], p['w_down'], l)
        h = _moe_finish_call(n2, routed, h, p['w_sh_gate'][l], p['w_sh_up'][l], p['w_sh_down'][l],
                             ada_l, row0, tiles_per_row)
        keys.append(z[:, B_KA * COL_BLK:(B_KA + 1) * COL_BLK])
        vals.append(z[:, B_VA * COL_BLK:(B_VA + 1) * COL_BLK])
        sts.append(st)
    return h, keys, vals, sts


def kernel(x_prompt, x_sample, cache_k, cache_v, state_hgrn, c, c_ctx, norm1_g, norm2_g, w_ada, b_ada,
           w_in, rpb, w_conv, lb_fwd, lb_bwd, hgrn_norm_g, w_br_a, w_br_b, w_br_c, w_out, w_router,
           b_router, w_gate, w_up, w_down, w_sh_gate, w_sh_up, w_sh_down, final_norm_g):
    batch, seq, _ = x_prompt.shape
    dec_batch, dec_seq, _ = x_sample.shape
    past = cache_k.shape[2]
    p = dict(norm1_g=norm1_g, norm2_g=norm2_g, w_in=w_in, w_conv=w_conv, lb_fwd=lb_fwd, lb_bwd=lb_bwd,
             hgrn_norm_g=hgrn_norm_g,
             w_br_a=w_br_a.astype(BF16), w_br_b=w_br_b.astype(BF16), w_br_c=w_br_c.astype(BF16),
             w_out=w_out.astype(BF16), w_router_t=jnp.swapaxes(w_router, 1, 2), b_router=b_router,
             w_gate=w_gate, w_up=w_up, w_down=w_down,
             w_sh_gate=w_sh_gate.astype(BF16), w_sh_up=w_sh_up.astype(BF16), w_sh_down=w_sh_down.astype(BF16))

    c8 = jnp.zeros((8, D_MODEL), F32).at[0].set(c_ctx).at[1:1 + dec_batch].set(c)
    ada = _ada_call(c8, w_ada, b_ada)

    zero_states = [jnp.zeros((batch, 2, H_C, DK_C, DK_C), F32)] * DEPTH
    ctx_attend = lambda z, l: _ctx_attn_call(z, seq)
    h_ctx, keys, vals, sts = _stream(x_prompt.reshape(batch * seq, D_MODEL), p, ada, 0, batch, seq,
                                     ctx_attend, zero_states)
    y_prompt = _final_norm_call(h_ctx, final_norm_g.reshape(1, D_MODEL)).reshape(batch, seq, D_MODEL)
    new_cache_k = jnp.stack([k.reshape(batch, seq, H_A, DH_A) for k in keys], axis=1)
    new_cache_v = jnp.stack([v.reshape(batch, seq, H_A, DH_A) for v in vals], axis=1)
    new_state = jnp.stack(sts, axis=1)

    rows = dec_seq // GRID_W
    ck = cache_k.reshape(dec_batch, DEPTH, past, D_A)
    cv = cache_v.reshape(dec_batch, DEPTH, past, D_A)
    lat_states = [state_hgrn[:, l].astype(F32) for l in range(DEPTH)]

    def lat_attend(z, l):
        bias = _na_bias_tables(rpb[l], rows)
        return _na_attn_call(z, ck[:, l], cv[:, l], bias, dec_batch, dec_seq)

    h_lat, _, _, _ = _stream(x_sample.reshape(dec_batch * dec_seq, D_MODEL), p, ada, 1, dec_batch, dec_seq,
                             lat_attend, lat_states)
    y_sample = _final_norm_call(h_lat, final_norm_g.reshape(1, D_MODEL)).reshape(dec_batch, dec_seq, D_MODEL)
    return (y_prompt, y_sample, new_cache_k, new_cache_v, new_state)
```

```python
import functools

import numpy as np
import jax
import jax.numpy as jnp
from jax import lax
from jax.experimental import pallas as pl
from jax.experimental.pallas import tpu as pltpu

F32 = jnp.float32
BF16 = jnp.bfloat16

D_MODEL = 1024
DEPTH = 2
GRID_W = 64
H_A = 8
DH_A = 64
D_A = H_A * DH_A
WIN_H = 8
WIN_W = 16
D_CONV = 512
H_C = 4
DK_C = 128
CHUNK = 16
N_EXPERTS = 64
N_GROUPS = 8
GROUP_SIZE = N_EXPERTS // N_GROUPS
TOPK_GROUPS = 4
TOP_K = 8
D_EXPERT = 256
ROUTED_SCALE = 2.5
N_MOD = 6
EPS = 1e-6
NEG_BIG = -1e30
LOG_FLOOR = 1e-30

D_IN = 8704
COL_BLK = 512
N_COL_BLKS = D_IN // COL_BLK
SRC_GATE_BLK = 11
N_GATE_BLKS = 6
B_GA, B_GBR, B_GCR = 0, 2, 4
B_QA, B_KA, B_VA, B_U, B_GB, B_GC, B_QC, B_ZF, B_ZB, B_IC, B_OC = range(6, 17)

NA_QROWS = 4
NA_KROWS = 12
NA_TQ = NA_QROWS * GRID_W
NA_TK = NA_KROWS * GRID_W

TOK_TILE = 256
VMEM_LIMIT = 56 * 1024 * 1024


def _cparams(sem):
    return pltpu.CompilerParams(dimension_semantics=sem, vmem_limit_bytes=VMEM_LIMIT)


def _sigmoid(x):
    return 1.0 / (1.0 + jnp.exp(-x))


def _silu(x):
    return x * _sigmoid(x)


def _dot(a, b):
    return jnp.dot(a, b, preferred_element_type=F32)


def _dot_nt(a, b, precision=None):
    return lax.dot_general(a, b, (((1,), (1,)), ((), ())), preferred_element_type=F32,
                           precision=precision)


LANES = 128
CHUNKS = D_MODEL // LANES


def _store_token_tiles(ref, val):
    for rb in range(val.shape[0] // 8):
        for c in range(CHUNKS):
            ref[pl.ds(rb * 8 * CHUNKS + c, 8, stride=CHUNKS), :] = \
                val[rb * 8:(rb + 1) * 8, c * LANES:(c + 1) * LANES]


def _load_token_tiles(ref, rows):
    return jnp.concatenate([ref[pl.ds(c, rows, stride=CHUNKS), :] for c in range(CHUNKS)], axis=1)


def _ada_kernel(c_ref, w_ref, b_ref, o_ref):
    a = _silu(c_ref[...])
    o_ref[0] = _dot(a, w_ref[0]) + b_ref[0]


def _ada_call(c8, w_ada, b_ada):
    tn = 1536
    n_out = N_MOD * D_MODEL
    return pl.pallas_call(
        _ada_kernel,
        grid=(DEPTH, n_out // tn),
        in_specs=[
            pl.BlockSpec((8, D_MODEL), lambda l, j: (0, 0)),
            pl.BlockSpec((1, D_MODEL, tn), lambda l, j: (l, 0, j)),
            pl.BlockSpec((1, 1, tn), lambda l, j: (l, 0, j)),
        ],
        out_specs=pl.BlockSpec((1, 8, tn), lambda l, j: (l, 0, j)),
        out_shape=jax.ShapeDtypeStruct((DEPTH, 8, n_out), F32),
        compiler_params=_cparams(("arbitrary", "arbitrary")),
        name="ada",
    )(c8, w_ada, b_ada.reshape(DEPTH, 1, n_out))


def _rms(x):
    return x * lax.rsqrt(jnp.mean(x * x, axis=-1, keepdims=True) + EPS)


def _norm_mod_kernel(h_ref, g_ref, sc_ref, sh_ref, o_ref):
    y = _rms(h_ref[...]) * g_ref[...]
    o_ref[...] = (y * (1.0 + sc_ref[0]) + sh_ref[0]).astype(o_ref.dtype)


def _ada_spec(mod_idx, row0, tiles_per_row):
    return pl.BlockSpec((1, 1, D_MODEL), lambda i: (row0 + i // tiles_per_row, 0, mod_idx))


def _norm_mod_call(h, g, ada, sc_idx, sh_idx, row0, tiles_per_row):
    m = h.shape[0]
    return pl.pallas_call(
        _norm_mod_kernel,
        grid=(m // TOK_TILE,),
        in_specs=[
            pl.BlockSpec((TOK_TILE, D_MODEL), lambda i: (i, 0)),
            pl.BlockSpec((1, D_MODEL), lambda i: (0, 0)),
            _ada_spec(sc_idx, row0, tiles_per_row),
            _ada_spec(sh_idx, row0, tiles_per_row),
        ],
        out_specs=pl.BlockSpec((TOK_TILE, D_MODEL), lambda i: (i, 0)),
        out_shape=jax.ShapeDtypeStruct((m, D_MODEL), BF16),
        compiler_params=_cparams(("arbitrary",)),
        name="norm_mod",
    )(h, g, ada, ada)


def _final_norm_kernel(h_ref, g_ref, o_ref):
    o_ref[...] = _rms(h_ref[...]) * g_ref[...]


def _final_norm_call(h, g):
    m = h.shape[0]
    return pl.pallas_call(
        _final_norm_kernel,
        grid=(m // TOK_TILE,),
        in_specs=[
            pl.BlockSpec((TOK_TILE, D_MODEL), lambda i: (i, 0)),
            pl.BlockSpec((1, D_MODEL), lambda i: (0, 0)),
        ],
        out_specs=pl.BlockSpec((TOK_TILE, D_MODEL), lambda i: (i, 0)),
        out_shape=jax.ShapeDtypeStruct((m, D_MODEL), F32),
        compiler_params=_cparams(("arbitrary",)),
        name="final_norm",
    )(h, g)


IN_PROJ_ROWS = 1024


def _in_proj_kernel(a_ref, w_ref, o_ref):
    w = w_ref[...].astype(BF16)
    for r in range(a_ref.shape[0] // IN_PROJ_ROWS):
        rows = slice(r * IN_PROJ_ROWS, (r + 1) * IN_PROJ_ROWS)
        o_ref[rows, :] = _dot(a_ref[rows, :], w)


def _src_col_blk(j):
    return jnp.where(j < N_GATE_BLKS, j + SRC_GATE_BLK, j - N_GATE_BLKS)


def _in_proj_call(n, w_in, layer):
    m = n.shape[0]
    return pl.pallas_call(
        _in_proj_kernel,
        grid=(N_COL_BLKS,),
        in_specs=[
            pl.BlockSpec((m, D_MODEL), lambda j: (0, 0)),
            pl.BlockSpec((None, D_MODEL, COL_BLK), lambda j: (layer, 0, _src_col_blk(j))),
        ],
        out_specs=pl.BlockSpec((m, COL_BLK), lambda j: (0, j)),
        out_shape=jax.ShapeDtypeStruct((m, D_IN), F32),
        compiler_params=_cparams(("arbitrary",)),
        name="in_proj",
    )(n, w_in)


def _softmax_pv(s_list, v_list):
    m = s_list[0].max(axis=-1, keepdims=True)
    for s in s_list[1:]:
        m = jnp.maximum(m, s.max(axis=-1, keepdims=True))
    num = None
    den = None
    for s, v in zip(s_list, v_list):
        p = jnp.exp(s - m)
        d = p.sum(axis=-1, keepdims=True)
        o = _dot(p.astype(BF16), v.astype(BF16))
        num = o if num is None else num + o
        den = d if den is None else den + d
    return num / den


def _ctx_attn_kernel(q_ref, k_ref, v_ref, o_ref):
    scale = DH_A ** -0.5
    for h in range(H_A):
        sl = slice(h * DH_A, (h + 1) * DH_A)
        s = _dot_nt(q_ref[:, sl].astype(BF16), k_ref[:, sl].astype(BF16)) * scale
        o_ref[:, sl] = _softmax_pv([s], [v_ref[:, sl]])


def _ctx_attn_call(z, seq):
    m = z.shape[0]
    spec = lambda blk: pl.BlockSpec((seq, D_A), lambda b: (b, blk))
    return pl.pallas_call(
        _ctx_attn_kernel,
        grid=(m // seq,),
        in_specs=[spec(B_QA), spec(B_KA), spec(B_VA)],
        out_specs=pl.BlockSpec((seq, D_A), lambda b: (b, 0)),
        out_shape=jax.ShapeDtypeStruct((m, D_A), F32),
        compiler_params=_cparams(("arbitrary",)),
        name="ctx_attn",
    )(z, z, z)


def _na_key_row0(rb, rows):
    return jnp.clip(NA_QROWS * rb - (NA_KROWS - WIN_H) , 0, rows - NA_KROWS)


def _na_attn_kernel(q_ref, k_ref, v_ref, ck_ref, cv_ref, bias_ref, o_ref, *, rows):
    scale = DH_A ** -0.5
    rb = pl.program_id(1)
    k0 = pl.multiple_of(_na_key_row0(rb, rows) * GRID_W, GRID_W)
    for h in range(H_A):
        sl = slice(h * DH_A, (h + 1) * DH_A)
        q = q_ref[:, sl].astype(BF16)
        kw = k_ref[pl.ds(k0, NA_TK), sl]
        vw = v_ref[pl.ds(k0, NA_TK), sl]
        s_lat = _dot_nt(q, kw.astype(BF16)) * scale + bias_ref[h]
        s_ctx = _dot_nt(q, ck_ref[:, sl].astype(BF16)) * scale
        o_ref[:, sl] = _softmax_pv([s_lat, s_ctx], [vw, cv_ref[:, sl]])


def _na_bias_pattern(rb, n_rb):
    return jnp.where(rb == 0, 0, jnp.where(rb == n_rb - 1, 2, 1))


def _na_bias_tables(rpb_l, rows):
    n_rb = rows // NA_QROWS
    qc = np.arange(GRID_W)
    q_start = np.clip(qc - WIN_W // 2, 0, GRID_W - WIN_W)
    kc = np.arange(GRID_W)
    valid_c = (kc[None, :] >= q_start[:, None]) & (kc[None, :] < q_start[:, None] + WIN_W)
    pad = GRID_W - WIN_W
    rpb_pad = jnp.pad(rpb_l.astype(F32), ((0, 0), (0, 0), (pad, pad)))
    col_tab = jnp.stack([rpb_pad[:, :, GRID_W - 1 - c:2 * GRID_W - 1 - c] for c in range(GRID_W)], axis=2)
    col_tab = jnp.where(valid_c, col_tab, NEG_BIG)
    masked = jnp.full((H_A, GRID_W, GRID_W), NEG_BIG, F32)
    tabs = []
    for rb in (0, 1, n_rb - 1):
        k_row0 = int(np.clip(NA_QROWS * rb - (NA_KROWS - WIN_H), 0, rows - NA_KROWS))
        q_rows = []
        for i in range(NA_QROWS):
            r = NA_QROWS * rb + i
            w0 = int(np.clip(r - WIN_H // 2, 0, rows - WIN_H))
            blocks = []
            for j in range(NA_KROWS):
                kr = k_row0 + j
                in_window = w0 <= kr < w0 + WIN_H
                blocks.append(col_tab[:, kr - r + WIN_H - 1] if in_window else masked)
            q_rows.append(jnp.concatenate(blocks, axis=2))
        tabs.append(jnp.concatenate(q_rows, axis=1))
    return jnp.stack(tabs)


def _na_attn_call(z, ck, cv, bias, bsz, seq):
    rows = seq // GRID_W
    n_rb = rows // NA_QROWS
    m = z.shape[0]
    kv_spec = lambda blk: pl.BlockSpec((seq, D_A), lambda b, r: (b, blk))
    c_spec = pl.BlockSpec((None, ck.shape[1], D_A), lambda b, r: (b, 0, 0))
    return pl.pallas_call(
        functools.partial(_na_attn_kernel, rows=rows),
        grid=(bsz, n_rb),
        in_specs=[
            pl.BlockSpec((NA_TQ, D_A), lambda b, r: (b * n_rb + r, B_QA)),
            kv_spec(B_KA), kv_spec(B_VA), c_spec, c_spec,
            pl.BlockSpec((None, H_A, NA_TQ, NA_TK), lambda b, r: (_na_bias_pattern(r, n_rb), 0, 0, 0)),
        ],
        out_specs=pl.BlockSpec((NA_TQ, D_A), lambda b, r: (b * n_rb + r, 0)),
        out_shape=jax.ShapeDtypeStruct((m, D_A), F32),
        compiler_params=_cparams(("arbitrary", "arbitrary")),
        name="na_attn",
    )(z, z, z, ck, cv, bias)


def _log1p(x):
    return jnp.log1p(x)


def _hgrn_gates(z, lb):
    log_sig = jnp.minimum(z, 0.0) - _log1p(jnp.exp(-jnp.abs(z)))
    a = _log1p(-lb) + log_sig
    b = jnp.log(jnp.maximum(lb, LOG_FLOOR))
    logf = jnp.maximum(a, b) + _log1p(jnp.exp(-jnp.abs(a - b)))
    key = (1.0 - lb) * _sigmoid(-z)
    return logf, key


def _lower_bound(p_ref, layer):
    p = p_ref[...]
    e = jnp.exp(p - jnp.max(p, axis=0, keepdims=True))
    sm = e / jnp.sum(e, axis=0, keepdims=True)
    lb = jnp.zeros((1, DK_C), F32)
    for j in range(1, layer + 1):
        lb = lb + sm[j:j + 1]
    return lb


def _chunk_cumsum(x, seq, reverse):
    pos = lax.broadcasted_iota(jnp.int32, x.shape, 0) % CHUNK
    s = 1
    while s < CHUNK:
        if reverse:
            x = x + jnp.where(pos < CHUNK - s, pltpu.roll(x, seq - s, axis=0), 0.0)
        else:
            x = x + jnp.where(pos >= s, pltpu.roll(x, s, axis=0), 0.0)
        s *= 2
    return x


HALF = CHUNK // 2
HGRN_UNROLL = 4


def _hgrn_chunk(q_ref, v_ref, b_s, k_s, o_s, r0, st, reverse):
    b = b_s[pl.ds(r0, CHUNK), :]
    q = q_ref[pl.ds(r0, CHUNK), :]
    k = k_s[pl.ds(r0, CHUNK), :]
    v = v_ref[pl.ds(r0, CHUNK), :]
    o = _dot_nt(q * jnp.exp(b), st)
    t_iota = lax.broadcasted_iota(jnp.int32, (HALF, 1), 0)
    o_half = [o[:HALF], o[HALF:]]
    for s in range(CHUNK):
        b_row = b_s[pl.ds(r0 + s, 1), :]
        k_row = k_s[pl.ds(r0 + s, 1), :]
        v_row = v_ref[pl.ds(r0 + s, 1), :]
        for half in range(2):
            t0 = half * HALF
            if (t0 + HALF - 1 < s) if not reverse else (t0 > s):
                continue
            sl = slice(t0, t0 + HALF)
            decay = jnp.exp(jnp.minimum(b[sl] - b_row, 0.0))
            col = jnp.sum(q[sl] * (k_row * decay), axis=-1, keepdims=True)
            if not ((t0 >= s) if not reverse else (t0 + HALF - 1 <= s)):
                keep = (t_iota + t0 >= s) if not reverse else (t_iota + t0 <= s)
                col = jnp.where(keep, col, 0.0)
            o_half[half] = o_half[half] + col * v_row
    o_s[pl.ds(r0, CHUNK), :] = jnp.concatenate(o_half, axis=0)
    edge = 0 if reverse else CHUNK - 1
    b_edge = b[edge:edge + 1, :]
    kh = k * jnp.exp(b_edge - b)
    kv_t = lax.dot_general(v, kh, (((0,), (0,)), ((), ())), preferred_element_type=F32)
    return st * jnp.exp(b_edge) + kv_t


def _hgrn_kernel(q_ref, zf_ref, zb_ref, i_ref, og_ref, lbf_ref, lbb_ref, g_ref, s0_ref,
                 o_ref, sfin_ref, of_s, ob_s, bf_s, bb_s, kf_s, kb_s, *, seq, layer):
    n_chunks = seq // CHUNK
    logf, key = _hgrn_gates(zf_ref[...], _lower_bound(lbf_ref, layer))
    bf_s[...] = _chunk_cumsum(logf, seq, False)
    kf_s[...] = key
    logf, key = _hgrn_gates(zb_ref[...], _lower_bound(lbb_ref, layer))
    bb_s[...] = _chunk_cumsum(logf, seq, True)
    kb_s[...] = key

    def body(it, carry):
        st_f, st_b = carry
        for u in range(HGRN_UNROLL):
            n = it * HGRN_UNROLL + u
            st_f = _hgrn_chunk(q_ref, i_ref, bf_s, kf_s, of_s, pl.multiple_of(n * CHUNK, CHUNK), st_f, False)
            st_b = _hgrn_chunk(q_ref, i_ref, bb_s, kb_s, ob_s,
                               pl.multiple_of((n_chunks - 1 - n) * CHUNK, CHUNK), st_b, True)
        return st_f, st_b

    st_f, st_b = lax.fori_loop(0, n_chunks // HGRN_UNROLL, body, (s0_ref[0].T, s0_ref[1].T))
    sfin_ref[0] = st_f.T
    sfin_ref[1] = st_b.T
    o_ref[...] = _rms(of_s[...] + ob_s[...]) * g_ref[...] * _silu(og_ref[...])


def _hgrn_call(z, lb_fwd, lb_bwd, norm_g, s0, bsz, seq, layer):
    m = z.shape[0]
    w = DK_C
    per = COL_BLK // w
    zspec = lambda blk: pl.BlockSpec((seq, w), lambda b, h: (b, blk * per + h))
    st_spec = pl.BlockSpec((None, 2, None, DK_C, DK_C), lambda b, h: (b, 0, h, 0, 0))
    return pl.pallas_call(
        functools.partial(_hgrn_kernel, seq=seq, layer=layer),
        grid=(bsz, H_C),
        in_specs=[
            zspec(B_QC), zspec(B_ZF), zspec(B_ZB), zspec(B_IC), zspec(B_OC),
            pl.BlockSpec((DEPTH, w), lambda b, h: (0, h)),
            pl.BlockSpec((DEPTH, w), lambda b, h: (0, h)),
            pl.BlockSpec((1, w), lambda b, h: (0, h)),
            st_spec,
        ],
        out_specs=[pl.BlockSpec((seq, w), lambda b, h: (b, h)), st_spec],
        out_shape=[jax.ShapeDtypeStruct((m, H_C * w), F32),
                   jax.ShapeDtypeStruct((bsz, 2, H_C, DK_C, DK_C), F32)],
        scratch_shapes=[pltpu.VMEM((seq, w), F32)] * 6,
        compiler_params=_cparams(("arbitrary", "arbitrary")),
        name="hgrn",
    )(z, z, z, z, z, lb_fwd, lb_bwd, norm_g, s0)


def _mix_kernel(oa_ref, oc_ref, u_ref, gb_ref, gc_ref, up_ref, gcp_ref, un_ref, gcn_ref,
                ga_ref, gbr_ref, gcr_ref, h_ref, wconv_ref, wa_ref, wb_ref, wc_ref, wout_ref,
                wrt_ref, g1_ref, n2g_ref, sc2_ref, sh2_ref,
                hout_ref, n2_ref, lgt_ref, *, tiles_per_seq):
    i = pl.program_id(0)
    tm = u_ref.shape[0]
    pos = i % tiles_per_seq
    zc = gc_ref[...] * u_ref[...]
    z_prev = jnp.where(pos == 0, 0.0, gcp_ref[7:8, :] * up_ref[7:8, :])
    z_next = jnp.where(pos == tiles_per_seq - 1, 0.0, gcn_ref[0:1, :] * un_ref[0:1, :])
    row = lax.broadcasted_iota(jnp.int32, zc.shape, 0)
    zp = jnp.where(row == 0, z_prev, pltpu.roll(zc, 1, axis=0))
    zn = jnp.where(row == tm - 1, z_next, pltpu.roll(zc, tm - 1, axis=0))
    w = wconv_ref[...]
    ob = gb_ref[...] * (w[0:1] * zp + w[1:2] * zc + w[2:3] * zn)
    ya = _dot(oa_ref[...].astype(BF16), wa_ref[...])
    yb = _dot(ob.astype(BF16), wb_ref[...])
    yc = _dot(oc_ref[...].astype(BF16), wc_ref[...])
    pre = _sigmoid(ga_ref[...]) * ya + _sigmoid(gbr_ref[...]) * yb + _sigmoid(gcr_ref[...]) * yc
    hn = h_ref[...] + g1_ref[0] * _dot(pre.astype(BF16), wout_ref[...])
    hout_ref[...] = hn
    n2 = _rms(hn) * n2g_ref[...] * (1.0 + sc2_ref[0]) + sh2_ref[0]
    _store_token_tiles(n2_ref, n2)
    lgt_ref[...] = _dot_nt(wrt_ref[...], n2, precision=lax.Precision.HIGHEST)


def _mix_call(o_a, o_c, z, h, w_conv, wa, wb, wc, wout, wrt, n2g, ada, row0, tiles_per_row, tiles_per_seq):
    m = h.shape[0]
    tm = TOK_TILE
    nt = m // tm
    r8 = tm // 8
    zblk = lambda blk: pl.BlockSpec((tm, COL_BLK), lambda i: (i, blk))
    zprev = lambda blk: pl.BlockSpec((8, COL_BLK), lambda i: (jnp.maximum(i * r8 - 1, 0), blk))
    znext = lambda blk: pl.BlockSpec((8, COL_BLK), lambda i: (jnp.minimum((i + 1) * r8, m // 8 - 1), blk))
    zgate = lambda blk: pl.BlockSpec((tm, D_MODEL), lambda i: (i, blk // 2))
    full = lambda a: pl.BlockSpec(a.shape, lambda i: (0,) * a.ndim)
    tok = lambda wdt: pl.BlockSpec((tm, wdt), lambda i: (i, 0))
    return pl.pallas_call(
        functools.partial(_mix_kernel, tiles_per_seq=tiles_per_seq),
        grid=(nt,),
        in_specs=[
            tok(D_A), tok(D_A), zblk(B_U), zblk(B_GB), zblk(B_GC),
            zprev(B_U), zprev(B_GC), znext(B_U), znext(B_GC),
            zgate(B_GA), zgate(B_GBR), zgate(B_GCR), tok(D_MODEL),
            full(w_conv), full(wa), full(wb), full(wc), full(wout), full(wrt),
            _ada_spec(2, row0, tiles_per_row), full(n2g),
            _ada_spec(4, row0, tiles_per_row), _ada_spec(3, row0, tiles_per_row),
        ],
        out_specs=[tok(D_MODEL), pl.BlockSpec((tm * CHUNKS, LANES), lambda i: (i, 0)),
                   pl.BlockSpec((N_EXPERTS, tm), lambda i: (0, i))],
        out_shape=[jax.ShapeDtypeStruct((m, D_MODEL), F32),
                   jax.ShapeDtypeStruct((m * CHUNKS, LANES), F32),
                   jax.ShapeDtypeStruct((N_EXPERTS, m), F32)],
        compiler_params=_cparams(("arbitrary",)),
        name="mix",
    )(o_a, o_c, z, z, z, z, z, z, z, z, z, z, h, w_conv, wa, wb, wc, wout, wrt, ada, n2g, ada, ada)


def _first_max(x, axes_iota, size):
    m = x
    for ax in range(x.ndim - 1):
        m = jnp.max(m, axis=ax, keepdims=True)
    first = jnp.where(x == m, axes_iota, size)
    f = first
    for ax in range(x.ndim - 1):
        f = jnp.min(f, axis=ax, keepdims=True)
    return axes_iota == f, f


def _first_max_mask(x, axes_iota, size):
    return _first_max(x, axes_iota, size)[0]


def _routing_kernel(lgt_ref, bias_ref, wpick_ref, eidx_ref, rank_ref, cnt_ref, carry_s):
    tn = lgt_ref.shape[-1]

    @pl.when(pl.program_id(0) == 0)
    def _():
        carry_s[...] = jnp.zeros(carry_s.shape, F32)

    s = _sigmoid(lgt_ref[...])
    sel = (s + bias_ref[...]).reshape(N_GROUPS, GROUP_SIZE, tn)
    s = s.reshape(N_GROUPS, GROUP_SIZE, tn)
    neg_inf = -jnp.inf
    mem_iota = lax.broadcasted_iota(jnp.int32, sel.shape, 1)
    m1 = jnp.max(sel, axis=1, keepdims=True)
    first = jnp.min(jnp.where(sel == m1, mem_iota, GROUP_SIZE), axis=1, keepdims=True)
    m2 = jnp.max(jnp.where(mem_iota == first, neg_inf, sel), axis=1, keepdims=True)
    grp = (m1 + m2).reshape(N_GROUPS, tn)
    g_iota = lax.broadcasted_iota(jnp.int32, grp.shape, 0)
    gmask = jnp.zeros(grp.shape, jnp.bool_)
    for _ in range(TOPK_GROUPS):
        pick = _first_max_mask(grp, g_iota, N_GROUPS)
        gmask = gmask | pick
        grp = jnp.where(pick, neg_inf, grp)
    cand = jnp.where(gmask.reshape(N_GROUPS, 1, tn), sel, NEG_BIG)
    e_iota = lax.broadcasted_iota(jnp.int32, sel.shape, 0) * GROUP_SIZE + mem_iota
    chosen = jnp.zeros(sel.shape, jnp.bool_)
    picked = []
    for _ in range(TOP_K):
        pick, idx = _first_max(cand, e_iota, N_EXPERTS)
        chosen = chosen | pick
        cand = jnp.where(pick, neg_inf, cand)
        picked.append(idx)
    w = jnp.where(chosen, s, 0.0)
    tot = jnp.sum(jnp.sum(w, axis=1, keepdims=True), axis=0, keepdims=True)
    w = w / tot * ROUTED_SCALE
    chosen_f = jnp.where(chosen, 1.0, 0.0).reshape(N_EXPERTS, tn)
    earlier = (lax.broadcasted_iota(jnp.int32, (tn, tn), 0) < lax.broadcasted_iota(jnp.int32, (tn, tn), 1))
    rank = _dot(chosen_f.astype(BF16), jnp.where(earlier, 1.0, 0.0).astype(BF16)) + carry_s[:, 0:1]
    rank = rank.reshape(N_GROUPS, GROUP_SIZE, tn)
    of_pick = lambda v, idx: jnp.sum(jnp.sum(jnp.where(e_iota == idx, v, 0.0), axis=1, keepdims=True),
                                     axis=0, keepdims=True).reshape(1, tn)
    eidx_ref[...] = jnp.concatenate([idx.reshape(1, tn) for idx in picked], axis=0)
    rank_ref[...] = jnp.concatenate([of_pick(rank, idx) for idx in picked], axis=0).astype(jnp.int32)
    wpick_ref[...] = jnp.concatenate([of_pick(w, idx) for idx in picked], axis=0)
    carry_s[...] = carry_s[...] + jnp.sum(chosen_f, axis=1, keepdims=True)
    cnt_ref[...] = carry_s[...]


def _routing_call(lgt, b_router_l):
    m = lgt.shape[1]
    tn = 256
    pick_spec = pl.BlockSpec((TOP_K, tn), lambda i: (0, i))
    cnt_spec = pl.BlockSpec((N_EXPERTS, 128), lambda i: (0, 0))
    return pl.pallas_call(
        _routing_kernel,
        grid=(m // tn,),
        in_specs=[pl.BlockSpec((N_EXPERTS, tn), lambda i: (0, i)),
                  pl.BlockSpec((N_EXPERTS, tn), lambda i: (0, 0))],
        out_specs=[pick_spec, pick_spec, pick_spec, cnt_spec],
        out_shape=[jax.ShapeDtypeStruct((TOP_K, m), F32),
                   jax.ShapeDtypeStruct((TOP_K, m), jnp.int32),
                   jax.ShapeDtypeStruct((TOP_K, m), jnp.int32),
                   jax.ShapeDtypeStruct((N_EXPERTS, 128), F32)],
        scratch_shapes=[pltpu.VMEM((N_EXPERTS, 128), F32)],
        compiler_params=_cparams(("arbitrary",)),
        name="routing",
    )(lgt, jnp.broadcast_to(b_router_l[:, None], (N_EXPERTS, tn)))


MOE_TM = 256
ROW_GROUP = 16
assert TOP_K == CHUNKS and TOP_K & (TOP_K - 1) == 0


def _token_row(idx):
    return pl.multiple_of(idx & ~(TOP_K - 1), CHUNKS)


def _dispatch_plan(eidx, rank, cnt, n_tiles_max):
    counts = cnt[:, 0].astype(jnp.int32)
    tiles = (counts + MOE_TM - 1) // MOE_TM
    tile_end = jnp.cumsum(tiles)
    tile_start = tile_end - tiles
    n_tiles = tile_end[-1]
    experts = jnp.arange(N_EXPERTS, dtype=jnp.int32)
    base = jnp.sum(jnp.where(eidx[:, :, None] == experts, tile_start * MOE_TM, 0), axis=-1)
    pos = (rank + base).T.reshape(-1)
    ti = jnp.arange(n_tiles_max, dtype=jnp.int32)
    tile_e = jnp.sum((ti[:, None] >= tile_end[None, :]).astype(jnp.int32), axis=1)
    onehot = jnp.minimum(tile_e, N_EXPERTS - 1)[:, None] == experts
    pick = lambda v: jnp.sum(jnp.where(onehot, v, 0), axis=1)
    valid = jnp.clip(pick(counts) - (ti - pick(tile_start)) * MOE_TM, 0, MOE_TM)
    valid = jnp.where(ti < n_tiles, valid, 0).astype(jnp.int32)
    last_e = jnp.sum(jnp.where(ti == n_tiles - 1, tile_e, 0))
    tile_e = jnp.where(ti < n_tiles, tile_e, last_e).astype(jnp.int32)
    return pos, tile_e, valid


def _moe_kernel(te_ref, tv_ref, pos_ref, wpick_ref, x_hbm, fill_hbm, wg_ref, wu_ref, wd_ref, out_hbm,
                rowidx_s, x_v, acc_v, g_s, y_s, wg_bf, wu_bf, wd_bf, sem):
    i = pl.program_id(0)
    n_tok = x_hbm.shape[0] // CHUNKS
    zero_blk = 512

    @pl.when(i == 0)
    def _prologue():
        x_copy = pltpu.make_async_copy(x_hbm, x_v.at[pl.ds(0, n_tok * CHUNKS)], sem.at[0])
        fill_copy = pltpu.make_async_copy(fill_hbm, rowidx_s, sem.at[1])
        x_copy.start()
        fill_copy.start()

        def zero(j, carry):
            acc_v[pl.ds(pl.multiple_of(j * zero_blk, zero_blk), zero_blk), :] = jnp.zeros((zero_blk, LANES), F32)
            return carry

        lax.fori_loop(0, n_tok * CHUNKS // zero_blk, zero, 0)
        acc_v[pl.ds(n_tok * CHUNKS, CHUNKS), :] = jnp.zeros((CHUNKS, LANES), F32)
        x_v[pl.ds(n_tok * CHUNKS, CHUNKS), :] = jnp.zeros((CHUNKS, LANES), F32)
        fill_copy.wait()

        def invert(j, carry):
            for u in range(ROW_GROUP):
                idx = j * ROW_GROUP + u
                rowidx_s[pos_ref[idx]] = idx
            return carry

        lax.fori_loop(0, n_tok * TOP_K // ROW_GROUP, invert, 0)
        x_copy.wait()

    @pl.when(tv_ref[i] > 0)
    def _tile():
        e = te_ref[i]
        prev_e = te_ref[jnp.maximum(i - 1, 0)]

        @pl.when((i == 0) | (prev_e != e))
        def _cast():
            wg_bf[...] = wg_ref[...].astype(BF16)
            wu_bf[...] = wu_ref[...].astype(BF16)
            wd_bf[...] = wd_ref[...].astype(BF16)

        row0 = i * MOE_TM

        def gather(j, carry):
            for u in range(ROW_GROUP):
                r = j * ROW_GROUP + u
                g_s[pl.ds(pl.multiple_of(r * CHUNKS, CHUNKS), CHUNKS), :] = \
                    x_v[pl.ds(_token_row(rowidx_s[row0 + r]), CHUNKS), :]
            return carry

        lax.fori_loop(0, MOE_TM // ROW_GROUP, gather, 0)
        x = _load_token_tiles(g_s, MOE_TM).astype(BF16)
        hid = _silu(_dot(x, wg_bf[...])) * _dot(x, wu_bf[...])
        y = _dot(hid.astype(BF16), wd_bf[...])
        blocks_per_group = ROW_GROUP // 8
        for rb in range(MOE_TM // 8):
            for c in range(CHUNKS):
                y_s[rb // blocks_per_group, pl.ds(((rb % blocks_per_group) * CHUNKS + c) * 8, 8), :] = \
                    y[rb * 8:(rb + 1) * 8, c * LANES:(c + 1) * LANES]

        def scatter(j, carry):
            slots, rows_v = [], []
            for u in range(ROW_GROUP):
                idx = rowidx_s[row0 + j * ROW_GROUP + u]
                slots.append(_token_row(idx))
                start = (u // 8) * (8 * CHUNKS) + (u % 8)
                rows_v.append(y_s[j, pl.ds(start, CHUNKS, stride=8), :] * wpick_ref[idx])
            olds = [acc_v[pl.ds(s, CHUNKS), :] for s in slots]
            for s, o, v in zip(slots, olds, rows_v):
                acc_v[pl.ds(s, CHUNKS), :] = o + v
            return carry

        lax.fori_loop(0, MOE_TM // ROW_GROUP, scatter, 0)

    @pl.when(i == pl.num_programs(0) - 1)
    def _epilogue():
        out_copy = pltpu.make_async_copy(acc_v.at[pl.ds(0, n_tok * CHUNKS)], out_hbm, sem.at[2])
        out_copy.start()
        out_copy.wait()


def _moe_call(x, wpick, eidx, rank, cnt, w_gate, w_up, w_down, layer):
    m = x.shape[0] // CHUNKS
    n_rows = TOP_K * m
    n_tiles_max = n_rows // MOE_TM + N_EXPERTS
    n_slots = n_tiles_max * MOE_TM
    pos, tile_e, valid = _dispatch_plan(eidx, rank, cnt, n_tiles_max)
    wflat = jnp.concatenate([wpick.T.reshape(-1), jnp.zeros((TOP_K,), F32)])
    wspec = lambda k, n: pl.BlockSpec((None, None, k, n), lambda i, te, tv: (layer, te[i], 0, 0))
    anyspec = pl.BlockSpec(memory_space=pl.ANY)
    smem = pl.BlockSpec(memory_space=pltpu.SMEM)
    grid_spec = pltpu.PrefetchScalarGridSpec(
        num_scalar_prefetch=2,
        grid=(n_tiles_max,),
        in_specs=[
            smem, smem, anyspec, anyspec,
            wspec(D_MODEL, D_EXPERT), wspec(D_MODEL, D_EXPERT), wspec(D_EXPERT, D_MODEL),
        ],
        out_specs=anyspec,
        scratch_shapes=[
            pltpu.SMEM((n_slots,), jnp.int32),
            pltpu.VMEM(((m + 1) * CHUNKS, LANES), F32),
            pltpu.VMEM(((m + 1) * CHUNKS, LANES), F32),
            pltpu.VMEM((MOE_TM * CHUNKS, LANES), F32),
            pltpu.VMEM((MOE_TM // ROW_GROUP, ROW_GROUP * CHUNKS, LANES), F32),
            pltpu.VMEM((D_MODEL, D_EXPERT), BF16),
            pltpu.VMEM((D_MODEL, D_EXPERT), BF16),
            pltpu.VMEM((D_EXPERT, D_MODEL), BF16),
            pltpu.SemaphoreType.DMA((3,)),
        ],
    )
    return pl.pallas_call(
        _moe_kernel,
        grid_spec=grid_spec,
        out_shape=jax.ShapeDtypeStruct((m * CHUNKS, LANES), F32),
        compiler_params=_cparams(("arbitrary",)),
        name="moe",
    )(tile_e, valid, pos, wflat, x, jnp.full((n_slots,), n_rows, jnp.int32), w_gate, w_up, w_down)


def _moe_finish_kernel(x_ref, routed_ref, h_ref, wsg_ref, wsu_ref, wsd_ref, g2_ref, o_ref):
    rows = h_ref.shape[0]
    x = _load_token_tiles(x_ref, rows).astype(BF16)
    hid = _silu(_dot(x, wsg_ref[...])) * _dot(x, wsu_ref[...])
    shared = _dot(hid.astype(BF16), wsd_ref[...])
    o_ref[...] = h_ref[...] + g2_ref[0] * (_load_token_tiles(routed_ref, rows) + shared)


def _moe_finish_call(x, routed, h, wsg, wsu, wsd, ada, row0, tiles_per_row):
    m = h.shape[0]
    tok = pl.BlockSpec((TOK_TILE, D_MODEL), lambda i: (i, 0))
    tiles = pl.BlockSpec((TOK_TILE * CHUNKS, LANES), lambda i: (i, 0))
    full = lambda a: pl.BlockSpec(a.shape, lambda i: (0,) * a.ndim)
    return pl.pallas_call(
        _moe_finish_kernel,
        grid=(m // TOK_TILE,),
        in_specs=[tiles, tiles, tok, full(wsg), full(wsu), full(wsd), _ada_spec(5, row0, tiles_per_row)],
        out_specs=tok,
        out_shape=jax.ShapeDtypeStruct((m, D_MODEL), F32),
        compiler_params=_cparams(("arbitrary",)),
        name="moe_finish",
    )(x, routed, h, wsg, wsu, wsd, ada)


def _stream(h, p, ada, row0, bsz, seq, attend, states):
    toks_per_row = seq if row0 > 0 else bsz * seq
    tiles_per_row = toks_per_row // TOK_TILE
    tiles_per_seq = seq // TOK_TILE
    keys, vals, sts = [], [], []
    for l in range(DEPTH):
        ada_l = ada[l].reshape(8, 1, N_MOD * D_MODEL)
        n = _norm_mod_call(h, p['norm1_g'][l:l + 1], ada_l, 1, 0, row0, tiles_per_row)
        z = _in_proj_call(n, p['w_in'], l)
        o_a = attend(z, l)
        o_c, st = _hgrn_call(z, p['lb_fwd'], p['lb_bwd'], p['hgrn_norm_g'][l:l + 1], states[l], bsz, seq, l)
        h, n2, lgt = _mix_call(o_a, o_c, z, h, p['w_conv'][l], p['w_br_a'][l], p['w_br_b'][l], p['w_br_c'][l],
                               p['w_out'][l], p['w_router_t'][l], p['norm2_g'][l:l + 1], ada_l,
                               row0, tiles_per_row, tiles_per_seq)
        wpick, eidx, rank, cnt = _routing_call(lgt, p['b_router'][l])
        routed = _moe_call(n2, wpick, eidx, rank, cnt, p['w_gate'], p['w_up'], p['w_down'], l)
        h = _moe_finish_call(n2, routed, h, p['w_sh_gate'][l], p['w_sh_up'][l], p['w_sh_down'][l],
                             ada_l, row0, tiles_per_row)
        keys.append(z[:, B_KA * COL_BLK:(B_KA + 1) * COL_BLK])
        vals.append(z[:, B_VA * COL_BLK:(B_VA + 1) * COL_BLK])
        sts.append(st)
    return h, keys, vals, sts


def kernel(x_prompt, x_sample, cache_k, cache_v, state_hgrn, c, c_ctx, norm1_g, norm2_g, w_ada, b_ada,
           w_in, rpb, w_conv, lb_fwd, lb_bwd, hgrn_norm_g, w_br_a, w_br_b, w_br_c, w_out, w_router,
           b_router, w_gate, w_up, w_down, w_sh_gate, w_sh_up, w_sh_down, final_norm_g):
    batch, seq, _ = x_prompt.shape
    dec_batch, dec_seq, _ = x_sample.shape
    past = cache_k.shape[2]
    p = dict(norm1_g=norm1_g, norm2_g=norm2_g, w_in=w_in, w_conv=w_conv, lb_fwd=lb_fwd, lb_bwd=lb_bwd,
             hgrn_norm_g=hgrn_norm_g,
             w_br_a=w_br_a.astype(BF16), w_br_b=w_br_b.astype(BF16), w_br_c=w_br_c.astype(BF16),
             w_out=w_out.astype(BF16), w_router_t=jnp.swapaxes(w_router, 1, 2), b_router=b_router,
             w_gate=w_gate, w_up=w_up, w_down=w_down,
             w_sh_gate=w_sh_gate.astype(BF16), w_sh_up=w_sh_up.astype(BF16), w_sh_down=w_sh_down.astype(BF16))

    c8 = jnp.zeros((8, D_MODEL), F32).at[0].set(c_ctx).at[1:1 + dec_batch].set(c)
    ada = _ada_call(c8, w_ada, b_ada)

    zero_states = [jnp.zeros((batch, 2, H_C, DK_C, DK_C), F32)] * DEPTH
    ctx_attend = lambda z, l: _ctx_attn_call(z, seq)
    h_ctx, keys, vals, sts = _stream(x_prompt.reshape(batch * seq, D_MODEL), p, ada, 0, batch, seq,
                                     ctx_attend, zero_states)
    y_prompt = _final_norm_call(h_ctx, final_norm_g.reshape(1, D_MODEL)).reshape(batch, seq, D_MODEL)
    new_cache_k = jnp.stack([k.reshape(batch, seq, H_A, DH_A) for k in keys], axis=1)
    new_cache_v = jnp.stack([v.reshape(batch, seq, H_A, DH_A) for v in vals], axis=1)
    new_state = jnp.stack(sts, axis=1)

    rows = dec_seq // GRID_W
    ck = cache_k.reshape(dec_batch, DEPTH, past, D_A)
    cv = cache_v.reshape(dec_batch, DEPTH, past, D_A)
    lat_states = [state_hgrn[:, l].astype(F32) for l in range(DEPTH)]

    def lat_attend(z, l):
        bias = _na_bias_tables(rpb[l], rows)
        return _na_attn_call(z, ck[:, l], cv[:, l], bias, dec_batch, dec_seq)

    h_lat, _, _, _ = _stream(x_sample.reshape(dec_batch * dec_seq, D_MODEL), p, ada, 1, dec_batch, dec_seq,
                             lat_attend, lat_states)
    y_sample = _final_norm_call(h_lat, final_norm_g.reshape(1, D_MODEL)).reshape(dec_batch, dec_seq, D_MODEL)
    return (y_prompt, y_sample, new_cache_k, new_cache_v, new_state)
```

```python
import functools

import numpy as np
import jax
import jax.numpy as jnp
from jax import lax
from jax.experimental import pallas as pl
from jax.experimental.pallas import tpu as pltpu

F32 = jnp.float32
BF16 = jnp.bfloat16

D_MODEL = 1024
DEPTH = 2
GRID_W = 64
H_A = 8
DH_A = 64
D_A = H_A * DH_A
WIN_H = 8
WIN_W = 16
D_CONV = 512
H_C = 4
DK_C = 128
CHUNK = 16
N_EXPERTS = 64
N_GROUPS = 8
GROUP_SIZE = N_EXPERTS // N_GROUPS
TOPK_GROUPS = 4
TOP_K = 8
D_EXPERT = 256
ROUTED_SCALE = 2.5
N_MOD = 6
EPS = 1e-6
NEG_BIG = -1e30
LOG_FLOOR = 1e-30

D_IN = 8704
COL_BLK = 512
N_COL_BLKS = D_IN // COL_BLK
SRC_GATE_BLK = 11
N_GATE_BLKS = 6
B_GA, B_GBR, B_GCR = 0, 2, 4
B_QA, B_KA, B_VA, B_U, B_GB, B_GC, B_QC, B_ZF, B_ZB, B_IC, B_OC = range(6, 17)

NA_QROWS = 4
NA_KROWS = 12
NA_TQ = NA_QROWS * GRID_W
NA_TK = NA_KROWS * GRID_W

TOK_TILE = 256
VMEM_LIMIT = 56 * 1024 * 1024


def _cparams(sem):
    return pltpu.CompilerParams(dimension_semantics=sem, vmem_limit_bytes=VMEM_LIMIT)


def _sigmoid(x):
    return 1.0 / (1.0 + jnp.exp(-x))


def _silu(x):
    return x * _sigmoid(x)


def _dot(a, b):
    return jnp.dot(a, b, preferred_element_type=F32)


def _dot_nt(a, b, precision=None):
    return lax.dot_general(a, b, (((1,), (1,)), ((), ())), preferred_element_type=F32,
                           precision=precision)


LANES = 128
CHUNKS = D_MODEL // LANES


def _store_token_tiles(ref, val):
    for rb in range(val.shape[0] // 8):
        for c in range(CHUNKS):
            ref[pl.ds(rb * 8 * CHUNKS + c, 8, stride=CHUNKS), :] = \
                val[rb * 8:(rb + 1) * 8, c * LANES:(c + 1) * LANES]


def _load_token_tiles(ref, rows):
    return jnp.concatenate([ref[pl.ds(c, rows, stride=CHUNKS), :] for c in range(CHUNKS)], axis=1)


def _ada_kernel(c_ref, w_ref, b_ref, o_ref):
    a = _silu(c_ref[...])
    o_ref[0] = _dot(a, w_ref[0]) + b_ref[0]


def _ada_call(c8, w_ada, b_ada):
    tn = 1536
    n_out = N_MOD * D_MODEL
    return pl.pallas_call(
        _ada_kernel,
        grid=(DEPTH, n_out // tn),
        in_specs=[
            pl.BlockSpec((8, D_MODEL), lambda l, j: (0, 0)),
            pl.BlockSpec((1, D_MODEL, tn), lambda l, j: (l, 0, j)),
            pl.BlockSpec((1, 1, tn), lambda l, j: (l, 0, j)),
        ],
        out_specs=pl.BlockSpec((1, 8, tn), lambda l, j: (l, 0, j)),
        out_shape=jax.ShapeDtypeStruct((DEPTH, 8, n_out), F32),
        compiler_params=_cparams(("arbitrary", "arbitrary")),
        name="ada",
    )(c8, w_ada, b_ada.reshape(DEPTH, 1, n_out))


def _rms(x):
    return x * lax.rsqrt(jnp.mean(x * x, axis=-1, keepdims=True) + EPS)


def _norm_mod_kernel(h_ref, g_ref, sc_ref, sh_ref, o_ref):
    y = _rms(h_ref[...]) * g_ref[...]
    o_ref[...] = (y * (1.0 + sc_ref[0]) + sh_ref[0]).astype(o_ref.dtype)


def _ada_spec(mod_idx, row0, tiles_per_row):
    return pl.BlockSpec((1, 1, D_MODEL), lambda i: (row0 + i // tiles_per_row, 0, mod_idx))


def _norm_mod_call(h, g, ada, sc_idx, sh_idx, row0, tiles_per_row):
    m = h.shape[0]
    return pl.pallas_call(
        _norm_mod_kernel,
        grid=(m // TOK_TILE,),
        in_specs=[
            pl.BlockSpec((TOK_TILE, D_MODEL), lambda i: (i, 0)),
            pl.BlockSpec((1, D_MODEL), lambda i: (0, 0)),
            _ada_spec(sc_idx, row0, tiles_per_row),
            _ada_spec(sh_idx, row0, tiles_per_row),
        ],
        out_specs=pl.BlockSpec((TOK_TILE, D_MODEL), lambda i: (i, 0)),
        out_shape=jax.ShapeDtypeStruct((m, D_MODEL), BF16),
        compiler_params=_cparams(("arbitrary",)),
        name="norm_mod",
    )(h, g, ada, ada)


def _final_norm_kernel(h_ref, g_ref, o_ref):
    o_ref[...] = _rms(h_ref[...]) * g_ref[...]


def _final_norm_call(h, g):
    m = h.shape[0]
    return pl.pallas_call(
        _final_norm_kernel,
        grid=(m // TOK_TILE,),
        in_specs=[
            pl.BlockSpec((TOK_TILE, D_MODEL), lambda i: (i, 0)),
            pl.BlockSpec((1, D_MODEL), lambda i: (0, 0)),
        ],
        out_specs=pl.BlockSpec((TOK_TILE, D_MODEL), lambda i: (i, 0)),
        out_shape=jax.ShapeDtypeStruct((m, D_MODEL), F32),
        compiler_params=_cparams(("arbitrary",)),
        name="final_norm",
    )(h, g)


IN_PROJ_ROWS = 1024


def _in_proj_kernel(a_ref, w_ref, o_ref):
    w = w_ref[...].astype(BF16)
    for r in range(a_ref.shape[0] // IN_PROJ_ROWS):
        rows = slice(r * IN_PROJ_ROWS, (r + 1) * IN_PROJ_ROWS)
        o_ref[rows, :] = _dot(a_ref[rows, :], w)


def _src_col_blk(j):
    return jnp.where(j < N_GATE_BLKS, j + SRC_GATE_BLK, j - N_GATE_BLKS)


def _in_proj_call(n, w_in, layer):
    m = n.shape[0]
    return pl.pallas_call(
        _in_proj_kernel,
        grid=(N_COL_BLKS,),
        in_specs=[
            pl.BlockSpec((m, D_MODEL), lambda j: (0, 0)),
            pl.BlockSpec((None, D_MODEL, COL_BLK), lambda j: (layer, 0, _src_col_blk(j))),
        ],
        out_specs=pl.BlockSpec((m, COL_BLK), lambda j: (0, j)),
        out_shape=jax.ShapeDtypeStruct((m, D_IN), F32),
        compiler_params=_cparams(("arbitrary",)),
        name="in_proj",
    )(n, w_in)


def _softmax_pv(s_list, v_list):
    m = s_list[0].max(axis=-1, keepdims=True)
    for s in s_list[1:]:
        m = jnp.maximum(m, s.max(axis=-1, keepdims=True))
    num = None
    den = None
    for s, v in zip(s_list, v_list):
        p = jnp.exp(s - m)
        d = p.sum(axis=-1, keepdims=True)
        o = _dot(p.astype(BF16), v.astype(BF16))
        num = o if num is None else num + o
        den = d if den is None else den + d
    return num / den


def _ctx_attn_kernel(q_ref, k_ref, v_ref, o_ref):
    scale = DH_A ** -0.5
    for h in range(H_A):
        sl = slice(h * DH_A, (h + 1) * DH_A)
        s = _dot_nt(q_ref[:, sl].astype(BF16), k_ref[:, sl].astype(BF16)) * scale
        o_ref[:, sl] = _softmax_pv([s], [v_ref[:, sl]])


def _ctx_attn_call(z, seq):
    m = z.shape[0]
    spec = lambda blk: pl.BlockSpec((seq, D_A), lambda b: (b, blk))
    return pl.pallas_call(
        _ctx_attn_kernel,
        grid=(m // seq,),
        in_specs=[spec(B_QA), spec(B_KA), spec(B_VA)],
        out_specs=pl.BlockSpec((seq, D_A), lambda b: (b, 0)),
        out_shape=jax.ShapeDtypeStruct((m, D_A), F32),
        compiler_params=_cparams(("arbitrary",)),
        name="ctx_attn",
    )(z, z, z)


def _na_key_row0(rb, rows):
    return jnp.clip(NA_QROWS * rb - (NA_KROWS - WIN_H) , 0, rows - NA_KROWS)


def _na_attn_kernel(q_ref, k_ref, v_ref, ck_ref, cv_ref, bias_ref, o_ref, *, rows):
    scale = DH_A ** -0.5
    rb = pl.program_id(1)
    k0 = pl.multiple_of(_na_key_row0(rb, rows) * GRID_W, GRID_W)
    for h in range(H_A):
        sl = slice(h * DH_A, (h + 1) * DH_A)
        q = q_ref[:, sl].astype(BF16)
        kw = k_ref[pl.ds(k0, NA_TK), sl]
        vw = v_ref[pl.ds(k0, NA_TK), sl]
        s_lat = _dot_nt(q, kw.astype(BF16)) * scale + bias_ref[h]
        s_ctx = _dot_nt(q, ck_ref[:, sl].astype(BF16)) * scale
        o_ref[:, sl] = _softmax_pv([s_lat, s_ctx], [vw, cv_ref[:, sl]])


def _na_bias_pattern(rb, n_rb):
    return jnp.where(rb == 0, 0, jnp.where(rb == n_rb - 1, 2, 1))


def _na_bias_tables(rpb_l, rows):
    n_rb = rows // NA_QROWS
    qc = np.arange(GRID_W)
    q_start = np.clip(qc - WIN_W // 2, 0, GRID_W - WIN_W)
    kc = np.arange(GRID_W)
    valid_c = (kc[None, :] >= q_start[:, None]) & (kc[None, :] < q_start[:, None] + WIN_W)
    pad = GRID_W - WIN_W
    rpb_pad = jnp.pad(rpb_l.astype(F32), ((0, 0), (0, 0), (pad, pad)))
    col_tab = jnp.stack([rpb_pad[:, :, GRID_W - 1 - c:2 * GRID_W - 1 - c] for c in range(GRID_W)], axis=2)
    col_tab = jnp.where(valid_c, col_tab, NEG_BIG)
    masked = jnp.full((H_A, GRID_W, GRID_W), NEG_BIG, F32)
    tabs = []
    for rb in (0, 1, n_rb - 1):
        k_row0 = int(np.clip(NA_QROWS * rb - (NA_KROWS - WIN_H), 0, rows - NA_KROWS))
        q_rows = []
        for i in range(NA_QROWS):
            r = NA_QROWS * rb + i
            w0 = int(np.clip(r - WIN_H // 2, 0, rows - WIN_H))
            blocks = []
            for j in range(NA_KROWS):
                kr = k_row0 + j
                in_window = w0 <= kr < w0 + WIN_H
                blocks.append(col_tab[:, kr - r + WIN_H - 1] if in_window else masked)
            q_rows.append(jnp.concatenate(blocks, axis=2))
        tabs.append(jnp.concatenate(q_rows, axis=1))
    return jnp.stack(tabs)


def _na_attn_call(z, ck, cv, bias, bsz, seq):
    rows = seq // GRID_W
    n_rb = rows // NA_QROWS
    m = z.shape[0]
    kv_spec = lambda blk: pl.BlockSpec((seq, D_A), lambda b, r: (b, blk))
    c_spec = pl.BlockSpec((None, ck.shape[1], D_A), lambda b, r: (b, 0, 0))
    return pl.pallas_call(
        functools.partial(_na_attn_kernel, rows=rows),
        grid=(bsz, n_rb),
        in_specs=[
            pl.BlockSpec((NA_TQ, D_A), lambda b, r: (b * n_rb + r, B_QA)),
            kv_spec(B_KA), kv_spec(B_VA), c_spec, c_spec,
            pl.BlockSpec((None, H_A, NA_TQ, NA_TK), lambda b, r: (_na_bias_pattern(r, n_rb), 0, 0, 0)),
        ],
        out_specs=pl.BlockSpec((NA_TQ, D_A), lambda b, r: (b * n_rb + r, 0)),
        out_shape=jax.ShapeDtypeStruct((m, D_A), F32),
        compiler_params=_cparams(("arbitrary", "arbitrary")),
        name="na_attn",
    )(z, z, z, ck, cv, bias)


def _log1p(x):
    return jnp.log1p(x)


def _hgrn_gates(z, lb):
    log_sig = jnp.minimum(z, 0.0) - _log1p(jnp.exp(-jnp.abs(z)))
    a = _log1p(-lb) + log_sig
    b = jnp.log(jnp.maximum(lb, LOG_FLOOR))
    logf = jnp.maximum(a, b) + _log1p(jnp.exp(-jnp.abs(a - b)))
    key = (1.0 - lb) * _sigmoid(-z)
    return logf, key


def _lower_bound(p_ref, layer):
    p = p_ref[...]
    e = jnp.exp(p - jnp.max(p, axis=0, keepdims=True))
    sm = e / jnp.sum(e, axis=0, keepdims=True)
    lb = jnp.zeros((1, DK_C), F32)
    for j in range(1, layer + 1):
        lb = lb + sm[j:j + 1]
    return lb


def _chunk_cumsum(x, seq, reverse):
    pos = lax.broadcasted_iota(jnp.int32, x.shape, 0) % CHUNK
    s = 1
    while s < CHUNK:
        if reverse:
            x = x + jnp.where(pos < CHUNK - s, pltpu.roll(x, seq - s, axis=0), 0.0)
        else:
            x = x + jnp.where(pos >= s, pltpu.roll(x, s, axis=0), 0.0)
        s *= 2
    return x


HALF = CHUNK // 2
HGRN_UNROLL = 4


def _hgrn_chunk(q_ref, v_ref, b_s, k_s, o_s, r0, st, reverse):
    b = b_s[pl.ds(r0, CHUNK), :]
    q = q_ref[pl.ds(r0, CHUNK), :]
    k = k_s[pl.ds(r0, CHUNK), :]
    v = v_ref[pl.ds(r0, CHUNK), :]
    o = _dot_nt(q * jnp.exp(b), st)
    t_iota = lax.broadcasted_iota(jnp.int32, (HALF, 1), 0)
    o_half = [o[:HALF], o[HALF:]]
    for s in range(CHUNK):
        b_row = b_s[pl.ds(r0 + s, 1), :]
        k_row = k_s[pl.ds(r0 + s, 1), :]
        v_row = v_ref[pl.ds(r0 + s, 1), :]
        for half in range(2):
            t0 = half * HALF
            if (t0 + HALF - 1 < s) if not reverse else (t0 > s):
                continue
            sl = slice(t0, t0 + HALF)
            decay = jnp.exp(jnp.minimum(b[sl] - b_row, 0.0))
            col = jnp.sum(q[sl] * (k_row * decay), axis=-1, keepdims=True)
            if not ((t0 >= s) if not reverse else (t0 + HALF - 1 <= s)):
                keep = (t_iota + t0 >= s) if not reverse else (t_iota + t0 <= s)
                col = jnp.where(keep, col, 0.0)
            o_half[half] = o_half[half] + col * v_row
    o_s[pl.ds(r0, CHUNK), :] = jnp.concatenate(o_half, axis=0)
    edge = 0 if reverse else CHUNK - 1
    b_edge = b[edge:edge + 1, :]
    kh = k * jnp.exp(b_edge - b)
    kv_t = lax.dot_general(v, kh, (((0,), (0,)), ((), ())), preferred_element_type=F32)
    return st * jnp.exp(b_edge) + kv_t


def _hgrn_kernel(q_ref, zf_ref, zb_ref, i_ref, og_ref, lbf_ref, lbb_ref, g_ref, s0_ref,
                 o_ref, sfin_ref, of_s, ob_s, bf_s, bb_s, kf_s, kb_s, *, seq, layer):
    n_chunks = seq // CHUNK
    logf, key = _hgrn_gates(zf_ref[...], _lower_bound(lbf_ref, layer))
    bf_s[...] = _chunk_cumsum(logf, seq, False)
    kf_s[...] = key
    logf, key = _hgrn_gates(zb_ref[...], _lower_bound(lbb_ref, layer))
    bb_s[...] = _chunk_cumsum(logf, seq, True)
    kb_s[...] = key

    def body(it, carry):
        st_f, st_b = carry
        for u in range(HGRN_UNROLL):
            n = it * HGRN_UNROLL + u
            st_f = _hgrn_chunk(q_ref, i_ref, bf_s, kf_s, of_s, pl.multiple_of(n * CHUNK, CHUNK), st_f, False)
            st_b = _hgrn_chunk(q_ref, i_ref, bb_s, kb_s, ob_s,
                               pl.multiple_of((n_chunks - 1 - n) * CHUNK, CHUNK), st_b, True)
        return st_f, st_b

    st_f, st_b = lax.fori_loop(0, n_chunks // HGRN_UNROLL, body, (s0_ref[0].T, s0_ref[1].T))
    sfin_ref[0] = st_f.T
    sfin_ref[1] = st_b.T
    o_ref[...] = _rms(of_s[...] + ob_s[...]) * g_ref[...] * _silu(og_ref[...])


def _hgrn_call(z, lb_fwd, lb_bwd, norm_g, s0, bsz, seq, layer):
    m = z.shape[0]
    w = DK_C
    per = COL_BLK // w
    zspec = lambda blk: pl.BlockSpec((seq, w), lambda b, h: (b, blk * per + h))
    st_spec = pl.BlockSpec((None, 2, None, DK_C, DK_C), lambda b, h: (b, 0, h, 0, 0))
    return pl.pallas_call(
        functools.partial(_hgrn_kernel, seq=seq, layer=layer),
        grid=(bsz, H_C),
        in_specs=[
            zspec(B_QC), zspec(B_ZF), zspec(B_ZB), zspec(B_IC), zspec(B_OC),
            pl.BlockSpec((DEPTH, w), lambda b, h: (0, h)),
            pl.BlockSpec((DEPTH, w), lambda b, h: (0, h)),
            pl.BlockSpec((1, w), lambda b, h: (0, h)),
            st_spec,
        ],
        out_specs=[pl.BlockSpec((seq, w), lambda b, h: (b, h)), st_spec],
        out_shape=[jax.ShapeDtypeStruct((m, H_C * w), F32),
                   jax.ShapeDtypeStruct((bsz, 2, H_C, DK_C, DK_C), F32)],
        scratch_shapes=[pltpu.VMEM((seq, w), F32)] * 6,
        compiler_params=_cparams(("arbitrary", "arbitrary")),
        name="hgrn",
    )(z, z, z, z, z, lb_fwd, lb_bwd, norm_g, s0)


def _mix_kernel(oa_ref, oc_ref, u_ref, gb_ref, gc_ref, up_ref, gcp_ref, un_ref, gcn_ref,
                ga_ref, gbr_ref, gcr_ref, h_ref, wconv_ref, wa_ref, wb_ref, wc_ref, wout_ref,
                wrt_ref, g1_ref, n2g_ref, sc2_ref, sh2_ref,
                hout_ref, n2_ref, lgt_ref, *, tiles_per_seq):
    i = pl.program_id(0)
    tm = u_ref.shape[0]
    pos = i % tiles_per_seq
    zc = gc_ref[...] * u_ref[...]
    z_prev = jnp.where(pos == 0, 0.0, gcp_ref[7:8, :] * up_ref[7:8, :])
    z_next = jnp.where(pos == tiles_per_seq - 1, 0.0, gcn_ref[0:1, :] * un_ref[0:1, :])
    row = lax.broadcasted_iota(jnp.int32, zc.shape, 0)
    zp = jnp.where(row == 0, z_prev, pltpu.roll(zc, 1, axis=0))
    zn = jnp.where(row == tm - 1, z_next, pltpu.roll(zc, tm - 1, axis=0))
    w = wconv_ref[...]
    ob = gb_ref[...] * (w[0:1] * zp + w[1:2] * zc + w[2:3] * zn)
    ya = _dot(oa_ref[...].astype(BF16), wa_ref[...])
    yb = _dot(ob.astype(BF16), wb_ref[...])
    yc = _dot(oc_ref[...].astype(BF16), wc_ref[...])
    pre = _sigmoid(ga_ref[...]) * ya + _sigmoid(gbr_ref[...]) * yb + _sigmoid(gcr_ref[...]) * yc
    hn = h_ref[...] + g1_ref[0] * _dot(pre.astype(BF16), wout_ref[...])
    hout_ref[...] = hn
    n2 = _rms(hn) * n2g_ref[...] * (1.0 + sc2_ref[0]) + sh2_ref[0]
    _store_token_tiles(n2_ref, n2)
    lgt_ref[...] = _dot_nt(wrt_ref[...], n2, precision=lax.Precision.HIGHEST)


def _mix_call(o_a, o_c, z, h, w_conv, wa, wb, wc, wout, wrt, n2g, ada, row0, tiles_per_row, tiles_per_seq):
    m = h.shape[0]
    tm = TOK_TILE
    nt = m // tm
    r8 = tm // 8
    zblk = lambda blk: pl.BlockSpec((tm, COL_BLK), lambda i: (i, blk))
    zprev = lambda blk: pl.BlockSpec((8, COL_BLK), lambda i: (jnp.maximum(i * r8 - 1, 0), blk))
    znext = lambda blk: pl.BlockSpec((8, COL_BLK), lambda i: (jnp.minimum((i + 1) * r8, m // 8 - 1), blk))
    zgate = lambda blk: pl.BlockSpec((tm, D_MODEL), lambda i: (i, blk // 2))
    full = lambda a: pl.BlockSpec(a.shape, lambda i: (0,) * a.ndim)
    tok = lambda wdt: pl.BlockSpec((tm, wdt), lambda i: (i, 0))
    return pl.pallas_call(
        functools.partial(_mix_kernel, tiles_per_seq=tiles_per_seq),
        grid=(nt,),
        in_specs=[
            tok(D_A), tok(D_A), zblk(B_U), zblk(B_GB), zblk(B_GC),
            zprev(B_U), zprev(B_GC), znext(B_U), znext(B_GC),
            zgate(B_GA), zgate(B_GBR), zgate(B_GCR), tok(D_MODEL),
            full(w_conv), full(wa), full(wb), full(wc), full(wout), full(wrt),
            _ada_spec(2, row0, tiles_per_row), full(n2g),
            _ada_spec(4, row0, tiles_per_row), _ada_spec(3, row0, tiles_per_row),
        ],
        out_specs=[tok(D_MODEL), pl.BlockSpec((tm * CHUNKS, LANES), lambda i: (i, 0)),
                   pl.BlockSpec((N_EXPERTS, tm), lambda i: (0, i))],
        out_shape=[jax.ShapeDtypeStruct((m, D_MODEL), F32),
                   jax.ShapeDtypeStruct((m * CHUNKS, LANES), F32),
                   jax.ShapeDtypeStruct((N_EXPERTS, m), F32)],
        compiler_params=_cparams(("arbitrary",)),
        name="mix",
    )(o_a, o_c, z, z, z, z, z, z, z, z, z, z, h, w_conv, wa, wb, wc, wout, wrt, ada, n2g, ada, ada)


def _first_max(x, axes_iota, size):
    m = x
    for ax in range(x.ndim - 1):
        m = jnp.max(m, axis=ax, keepdims=True)
    first = jnp.where(x == m, axes_iota, size)
    f = first
    for ax in range(x.ndim - 1):
        f = jnp.min(f, axis=ax, keepdims=True)
    return axes_iota == f, f


def _first_max_mask(x, axes_iota, size):
    return _first_max(x, axes_iota, size)[0]


def _routing_kernel(lgt_ref, bias_ref, wpick_ref, eidx_ref, rank_ref, cnt_ref, carry_s):
    tn = lgt_ref.shape[-1]

    @pl.when(pl.program_id(0) == 0)
    def _():
        carry_s[...] = jnp.zeros(carry_s.shape, F32)

    s = _sigmoid(lgt_ref[...])
    sel = (s + bias_ref[...]).reshape(N_GROUPS, GROUP_SIZE, tn)
    s = s.reshape(N_GROUPS, GROUP_SIZE, tn)
    neg_inf = -jnp.inf
    mem_iota = lax.broadcasted_iota(jnp.int32, sel.shape, 1)
    m1 = jnp.max(sel, axis=1, keepdims=True)
    first = jnp.min(jnp.where(sel == m1, mem_iota, GROUP_SIZE), axis=1, keepdims=True)
    m2 = jnp.max(jnp.where(mem_iota == first, neg_inf, sel), axis=1, keepdims=True)
    grp = (m1 + m2).reshape(N_GROUPS, tn)
    g_iota = lax.broadcasted_iota(jnp.int32, grp.shape, 0)
    gmask = jnp.zeros(grp.shape, jnp.bool_)
    for _ in range(TOPK_GROUPS):
        pick = _first_max_mask(grp, g_iota, N_GROUPS)
        gmask = gmask | pick
        grp = jnp.where(pick, neg_inf, grp)
    cand = jnp.where(gmask.reshape(N_GROUPS, 1, tn), sel, NEG_BIG)
    e_iota = lax.broadcasted_iota(jnp.int32, sel.shape, 0) * GROUP_SIZE + mem_iota
    chosen = jnp.zeros(sel.shape, jnp.bool_)
    picked = []
    for _ in range(TOP_K):
        pick, idx = _first_max(cand, e_iota, N_EXPERTS)
        chosen = chosen | pick
        cand = jnp.where(pick, neg_inf, cand)
        picked.append(idx)
    w = jnp.where(chosen, s, 0.0)
    tot = jnp.sum(jnp.sum(w, axis=1, keepdims=True), axis=0, keepdims=True)
    w = w / tot * ROUTED_SCALE
    chosen_f = jnp.where(chosen, 1.0, 0.0).reshape(N_EXPERTS, tn)
    earlier = (lax.broadcasted_iota(jnp.int32, (tn, tn), 0) < lax.broadcasted_iota(jnp.int32, (tn, tn), 1))
    rank = _dot(chosen_f.astype(BF16), jnp.where(earlier, 1.0, 0.0).astype(BF16)) + carry_s[:, 0:1]
    rank = rank.reshape(N_GROUPS, GROUP_SIZE, tn)
    of_pick = lambda v, idx: jnp.sum(jnp.sum(jnp.where(e_iota == idx, v, 0.0), axis=1, keepdims=True),
                                     axis=0, keepdims=True).reshape(1, tn)
    eidx_ref[...] = jnp.concatenate([idx.reshape(1, tn) for idx in picked], axis=0)
    rank_ref[...] = jnp.concatenate([of_pick(rank, idx) for idx in picked], axis=0).astype(jnp.int32)
    wpick_ref[...] = jnp.concatenate([of_pick(w, idx) for idx in picked], axis=0)
    carry_s[...] = carry_s[...] + jnp.sum(chosen_f, axis=1, keepdims=True)
    cnt_ref[...] = carry_s[...]


def _routing_call(lgt, b_router_l):
    m = lgt.shape[1]
    tn = 256
    pick_spec = pl.BlockSpec((TOP_K, tn), lambda i: (0, i))
    cnt_spec = pl.BlockSpec((N_EXPERTS, 128), lambda i: (0, 0))
    return pl.pallas_call(
        _routing_kernel,
        grid=(m // tn,),
        in_specs=[pl.BlockSpec((N_EXPERTS, tn), lambda i: (0, i)),
                  pl.BlockSpec((N_EXPERTS, tn), lambda i: (0, 0))],
        out_specs=[pick_spec, pick_spec, pick_spec, cnt_spec],
        out_shape=[jax.ShapeDtypeStruct((TOP_K, m), F32),
                   jax.ShapeDtypeStruct((TOP_K, m), jnp.int32),
                   jax.ShapeDtypeStruct((TOP_K, m), jnp.int32),
                   jax.ShapeDtypeStruct((N_EXPERTS, 128), F32)],
        scratch_shapes=[pltpu.VMEM((N_EXPERTS, 128), F32)],
        compiler_params=_cparams(("arbitrary",)),
        name="routing",
    )(lgt, jnp.broadcast_to(b_router_l[:, None], (N_EXPERTS, tn)))


MOE_TM = 256
ROW_GROUP = 16
assert TOP_K == CHUNKS and TOP_K & (TOP_K - 1) == 0


def _token_row(idx):
    return pl.multiple_of(idx & ~(TOP_K - 1), CHUNKS)


def _dispatch_plan(eidx, rank, cnt, n_tiles_max):
    counts = cnt[:, 0].astype(jnp.int32)
    tiles = (counts + MOE_TM - 1) // MOE_TM
    tile_end = jnp.cumsum(tiles)
    tile_start = tile_end - tiles
    n_tiles = tile_end[-1]
    experts = jnp.arange(N_EXPERTS, dtype=jnp.int32)
    base = jnp.sum(jnp.where(eidx[:, :, None] == experts, tile_start * MOE_TM, 0), axis=-1)
    pos = (rank + base).T.reshape(-1)
    ti = jnp.arange(n_tiles_max, dtype=jnp.int32)
    tile_e = jnp.sum((ti[:, None] >= tile_end[None, :]).astype(jnp.int32), axis=1)
    last_e = jnp.sum(jnp.where(ti == n_tiles - 1, tile_e, 0))
    tile_e = jnp.where(ti < n_tiles, tile_e, last_e).astype(jnp.int32)
    return pos, tile_e, n_tiles.astype(jnp.int32)


def _moe_step(tile_g, tile_s, rowidx_s, wpick_ref, x_v, acc_v, g_w, g_r, y_w, y_r, wg_bf, wu_bf, wd_bf):
    row_g = tile_g * MOE_TM
    for r in range(MOE_TM):
        g_w[pl.ds(r * CHUNKS, CHUNKS), :] = x_v[pl.ds(_token_row(rowidx_s[row_g + r]), CHUNKS), :]

    x = _load_token_tiles(g_r, MOE_TM).astype(BF16)
    hid = _silu(_dot(x, wg_bf[...])) * _dot(x, wu_bf[...])
    y = _dot(hid.astype(BF16), wd_bf[...])
    for rb in range(MOE_TM // 8):
        for c in range(CHUNKS):
            y_w[pl.ds((rb * CHUNKS + c) * 8, 8), :] = y[rb * 8:(rb + 1) * 8, c * LANES:(c + 1) * LANES]

    row_s = tile_s * MOE_TM
    for j in range(MOE_TM // ROW_GROUP):
        slots, rows_v = [], []
        for u in range(ROW_GROUP):
            r = j * ROW_GROUP + u
            idx = rowidx_s[row_s + r]
            slots.append(_token_row(idx))
            rows_v.append(y_r[pl.ds((r // 8) * (8 * CHUNKS) + r % 8, CHUNKS, stride=8), :] * wpick_ref[idx])
        olds = [acc_v[pl.ds(s, CHUNKS), :] for s in slots]
        for s, o, v in zip(slots, olds, rows_v):
            acc_v[pl.ds(s, CHUNKS), :] = o + v


def _moe_kernel(te_ref, nt_ref, pos_ref, wpick_ref, x_hbm, fill_hbm, wg_ref, wu_ref, wd_ref, out_hbm,
                rowidx_s, x_v, acc_v, g0_s, g1_s, y0_s, y1_s, wg_bf, wu_bf, wd_bf, sem):
    s = pl.program_id(0)
    n_tok = x_hbm.shape[0] // CHUNKS
    n_tiles_max = te_ref.shape[0]
    zero_blk = 512

    @pl.when(s == 0)
    def _prologue():
        x_copy = pltpu.make_async_copy(x_hbm, x_v.at[pl.ds(0, n_tok * CHUNKS)], sem.at[0])
        fill_copy = pltpu.make_async_copy(fill_hbm, rowidx_s, sem.at[1])
        x_copy.start()
        fill_copy.start()

        def zero(j, carry):
            acc_v[pl.ds(pl.multiple_of(j * zero_blk, zero_blk), zero_blk), :] = jnp.zeros((zero_blk, LANES), F32)
            return carry

        lax.fori_loop(0, n_tok * CHUNKS // zero_blk, zero, 0)
        acc_v[pl.ds(n_tok * CHUNKS, CHUNKS), :] = jnp.zeros((CHUNKS, LANES), F32)
        x_v[pl.ds(n_tok * CHUNKS, CHUNKS), :] = jnp.zeros((CHUNKS, LANES), F32)
        g1_s[...] = jnp.zeros(g1_s.shape, F32)
        y0_s[...] = jnp.zeros(y0_s.shape, F32)
        fill_copy.wait()

        def invert(j, carry):
            for u in range(ROW_GROUP):
                idx = j * ROW_GROUP + u
                rowidx_s[pos_ref[idx]] = idx
            return carry

        lax.fori_loop(0, n_tok * TOP_K // ROW_GROUP, invert, 0)
        x_copy.wait()

    active = s < nt_ref[0] + 2
    ffn_tile = jnp.clip(s - 1, 0, n_tiles_max - 1)

    @pl.when(active & ((s <= 1) | (te_ref[ffn_tile] != te_ref[jnp.maximum(ffn_tile - 1, 0)])))
    def _cast():
        wg_bf[...] = wg_ref[...].astype(BF16)
        wu_bf[...] = wu_ref[...].astype(BF16)
        wd_bf[...] = wd_ref[...].astype(BF16)

    tile_g = jnp.minimum(s, n_tiles_max - 1)
    tile_s = jnp.maximum(s - 2, 0)
    common = (rowidx_s, wpick_ref, x_v, acc_v)
    weights = (wg_bf, wu_bf, wd_bf)

    @pl.when(active & (s % 2 == 0))
    def _even():
        _moe_step(tile_g, tile_s, *common, g0_s, g1_s, y1_s, y0_s, *weights)

    @pl.when(active & (s % 2 == 1))
    def _odd():
        _moe_step(tile_g, tile_s, *common, g1_s, g0_s, y0_s, y1_s, *weights)

    @pl.when(s == pl.num_programs(0) - 1)
    def _epilogue():
        out_copy = pltpu.make_async_copy(acc_v.at[pl.ds(0, n_tok * CHUNKS)], out_hbm, sem.at[2])
        out_copy.start()
        out_copy.wait()


def _moe_call(x, wpick, eidx, rank, cnt, w_gate, w_up, w_down, layer):
    m = x.shape[0] // CHUNKS
    n_rows = TOP_K * m
    n_tiles_max = n_rows // MOE_TM + N_EXPERTS
    n_slots = n_tiles_max * MOE_TM
    pos, tile_e, n_tiles = _dispatch_plan(eidx, rank, cnt, n_tiles_max)
    wflat = jnp.concatenate([wpick.T.reshape(-1), jnp.zeros((TOP_K,), F32)])
    ffn_tile = lambda s: jnp.clip(s - 1, 0, n_tiles_max - 1)
    wspec = lambda k, n: pl.BlockSpec((None, None, k, n), lambda s, te, nt: (layer, te[ffn_tile(s)], 0, 0))
    anyspec = pl.BlockSpec(memory_space=pl.ANY)
    smem = pl.BlockSpec(memory_space=pltpu.SMEM)
    tile_buf = pltpu.VMEM((MOE_TM * CHUNKS, LANES), F32)
    grid_spec = pltpu.PrefetchScalarGridSpec(
        num_scalar_prefetch=2,
        grid=(n_tiles_max + 2,),
        in_specs=[
            smem, smem, anyspec, anyspec,
            wspec(D_MODEL, D_EXPERT), wspec(D_MODEL, D_EXPERT), wspec(D_EXPERT, D_MODEL),
        ],
        out_specs=anyspec,
        scratch_shapes=[
            pltpu.SMEM((n_slots,), jnp.int32),
            pltpu.VMEM(((m + 1) * CHUNKS, LANES), F32),
            pltpu.VMEM(((m + 1) * CHUNKS, LANES), F32),
            tile_buf, tile_buf, tile_buf, tile_buf,
            pltpu.VMEM((D_MODEL, D_EXPERT), BF16),
            pltpu.VMEM((D_MODEL, D_EXPERT), BF16),
            pltpu.VMEM((D_EXPERT, D_MODEL), BF16),
            pltpu.SemaphoreType.DMA((3,)),
        ],
    )
    return pl.pallas_call(
        _moe_kernel,
        grid_spec=grid_spec,
        out_shape=jax.ShapeDtypeStruct((m * CHUNKS, LANES), F32),
        compiler_params=_cparams(("arbitrary",)),
        name="moe",
    )(tile_e, n_tiles.reshape(1), pos, wflat, x, jnp.full((n_slots,), n_rows, jnp.int32), w_gate, w_up, w_down)


def _moe_finish_kernel(x_ref, routed_ref, h_ref, wsg_ref, wsu_ref, wsd_ref, g2_ref, o_ref):
    rows = h_ref.shape[0]
    x = _load_token_tiles(x_ref, rows).astype(BF16)
    hid = _silu(_dot(x, wsg_ref[...])) * _dot(x, wsu_ref[...])
    shared = _dot(hid.astype(BF16), wsd_ref[...])
    o_ref[...] = h_ref[...] + g2_ref[0] * (_load_token_tiles(routed_ref, rows) + shared)


def _moe_finish_call(x, routed, h, wsg, wsu, wsd, ada, row0, tiles_per_row):
    m = h.shape[0]
    tok = pl.BlockSpec((TOK_TILE, D_MODEL), lambda i: (i, 0))
    tiles = pl.BlockSpec((TOK_TILE * CHUNKS, LANES), lambda i: (i, 0))
    full = lambda a: pl.BlockSpec(a.shape, lambda i: (0,) * a.ndim)
    return pl.pallas_call(
        _moe_finish_kernel,
        grid=(m // TOK_TILE,),
        in_specs=[tiles, tiles, tok, full(wsg), full(wsu), full(wsd), _ada_spec(5, row0, tiles_per_row)],
        out_specs=tok,
        out_shape=jax.ShapeDtypeStruct((m, D_MODEL), F32),
        compiler_params=_cparams(("arbitrary",)),
        name="moe_finish",
    )(x, routed, h, wsg, wsu, wsd, ada)


def _stream(h, p, ada, row0, bsz, seq, attend, states):
    toks_per_row = seq if row0 > 0 else bsz * seq
    tiles_per_row = toks_per_row // TOK_TILE
    tiles_per_seq = seq // TOK_TILE
    keys, vals, sts = [], [], []
    for l in range(DEPTH):
        ada_l = ada[l].reshape(8, 1, N_MOD * D_MODEL)
        n = _norm_mod_call(h, p['norm1_g'][l:l + 1], ada_l, 1, 0, row0, tiles_per_row)
        z = _in_proj_call(n, p['w_in'], l)
        o_a = attend(z, l)
        o_c, st = _hgrn_call(z, p['lb_fwd'], p['lb_bwd'], p['hgrn_norm_g'][l:l + 1], states[l], bsz, seq, l)
        h, n2, lgt = _mix_call(o_a, o_c, z, h, p['w_conv'][l], p['w_br_a'][l], p['w_br_b'][l], p['w_br_c'][l],
                               p['w_out'][l], p['w_router_t'][l], p['norm2_g'][l:l + 1], ada_l,
                               row0, tiles_per_row, tiles_per_seq)
        wpick, eidx, rank, cnt = _routing_call(lgt, p['b_router'][l])
        routed = _moe_call(n2, wpick, eidx, rank, cnt, p['w_gate'], p['w_up'], p['w_down'], l)
        h = _moe_finish_call(n2, routed, h, p['w_sh_gate'][l], p['w_sh_up'][l], p['w_sh_down'][l],
                             ada_l, row0, tiles_per_row)
        keys.append(z[:, B_KA * COL_BLK:(B_KA + 1) * COL_BLK])
        vals.append(z[:, B_VA * COL_BLK:(B_VA + 1) * COL_BLK])
        sts.append(st)
    return h, keys, vals, sts


def kernel(x_prompt, x_sample, cache_k, cache_v, state_hgrn, c, c_ctx, norm1_g, norm2_g, w_ada, b_ada,
           w_in, rpb, w_conv, lb_fwd, lb_bwd, hgrn_norm_g, w_br_a, w_br_b, w_br_c, w_out, w_router,
           b_router, w_gate, w_up, w_down, w_sh_gate, w_sh_up, w_sh_down, final_norm_g):
    batch, seq, _ = x_prompt.shape
    dec_batch, dec_seq, _ = x_sample.shape
    past = cache_k.shape[2]
    p = dict(norm1_g=norm1_g, norm2_g=norm2_g, w_in=w_in, w_conv=w_conv, lb_fwd=lb_fwd, lb_bwd=lb_bwd,
             hgrn_norm_g=hgrn_norm_g,
             w_br_a=w_br_a.astype(BF16), w_br_b=w_br_b.astype(BF16), w_br_c=w_br_c.astype(BF16),
             w_out=w_out.astype(BF16), w_router_t=jnp.swapaxes(w_router, 1, 2), b_router=b_router,
             w_gate=w_gate, w_up=w_up, w_down=w_down,
             w_sh_gate=w_sh_gate.astype(BF16), w_sh_up=w_sh_up.astype(BF16), w_sh_down=w_sh_down.astype(BF16))

    c8 = jnp.zeros((8, D_MODEL), F32).at[0].set(c_ctx).at[1:1 + dec_batch].set(c)
    ada = _ada_call(c8, w_ada, b_ada)

    zero_states = [jnp.zeros((batch, 2, H_C, DK_C, DK_C), F32)] * DEPTH
    ctx_attend = lambda z, l: _ctx_attn_call(z, seq)
    h_ctx, keys, vals, sts = _stream(x_prompt.reshape(batch * seq, D_MODEL), p, ada, 0, batch, seq,
                                     ctx_attend, zero_states)
    y_prompt = _final_norm_call(h_ctx, final_norm_g.reshape(1, D_MODEL)).reshape(batch, seq, D_MODEL)
    new_cache_k = jnp.stack([k.reshape(batch, seq, H_A, DH_A) for k in keys], axis=1)
    new_cache_v = jnp.stack([v.reshape(batch, seq, H_A, DH_A) for v in vals], axis=1)
    new_state = jnp.stack(sts, axis=1)

    rows = dec_seq // GRID_W
    ck = cache_k.reshape(dec_batch, DEPTH, past, D_A)
    cv = cache_v.reshape(dec_batch, DEPTH, past, D_A)
    lat_states = [state_hgrn[:, l].astype(F32) for l in range(DEPTH)]

    def lat_attend(z, l):
        bias = _na_bias_tables(rpb[l], rows)
        return _na_attn_call(z, ck[:, l], cv[:, l], bias, dec_batch, dec_seq)

    h_lat, _, _, _ = _stream(x_sample.reshape(dec_batch * dec_seq, D_MODEL), p, ada, 1, dec_batch, dec_seq,
                             lat_attend, lat_states)
    y_sample = _final_norm_call(h_lat, final_norm_g.reshape(1, D_MODEL)).reshape(dec_batch, dec_seq, D_MODEL)
    return (y_prompt, y_sample, new_cache_k, new_cache_v, new_state)
```

```python
import functools

import numpy as np
import jax
import jax.numpy as jnp
from jax import lax
from jax.experimental import pallas as pl
from jax.experimental.pallas import tpu as pltpu

F32 = jnp.float32
BF16 = jnp.bfloat16

D_MODEL = 1024
DEPTH = 2
GRID_W = 64
H_A = 8
DH_A = 64
D_A = H_A * DH_A
WIN_H = 8
WIN_W = 16
D_CONV = 512
H_C = 4
DK_C = 128
CHUNK = 16
N_EXPERTS = 64
N_GROUPS = 8
GROUP_SIZE = N_EXPERTS // N_GROUPS
TOPK_GROUPS = 4
TOP_K = 8
D_EXPERT = 256
ROUTED_SCALE = 2.5
N_MOD = 6
EPS = 1e-6
NEG_BIG = -1e30
LOG_FLOOR = 1e-30

D_IN = 8704
COL_BLK = 512
N_COL_BLKS = D_IN // COL_BLK
SRC_GATE_BLK = 11
N_GATE_BLKS = 6
B_GA, B_GBR, B_GCR = 0, 2, 4
B_QA, B_KA, B_VA, B_U, B_GB, B_GC, B_QC, B_ZF, B_ZB, B_IC, B_OC = range(6, 17)

NA_QROWS = 4
NA_KROWS = 12
NA_TQ = NA_QROWS * GRID_W
NA_TK = NA_KROWS * GRID_W

TOK_TILE = 256
VMEM_LIMIT = 56 * 1024 * 1024


def _cparams(sem):
    return pltpu.CompilerParams(dimension_semantics=sem, vmem_limit_bytes=VMEM_LIMIT)


def _sigmoid(x):
    return 1.0 / (1.0 + jnp.exp(-x))


def _silu(x):
    return x * _sigmoid(x)


def _dot(a, b):
    return jnp.dot(a, b, preferred_element_type=F32)


def _dot_nt(a, b, precision=None):
    return lax.dot_general(a, b, (((1,), (1,)), ((), ())), preferred_element_type=F32,
                           precision=precision)


LANES = 128
CHUNKS = D_MODEL // LANES


def _store_token_tiles(ref, val):
    for rb in range(val.shape[0] // 8):
        for c in range(CHUNKS):
            ref[pl.ds(rb * 8 * CHUNKS + c, 8, stride=CHUNKS), :] = \
                val[rb * 8:(rb + 1) * 8, c * LANES:(c + 1) * LANES]


def _load_token_tiles(ref, rows):
    return jnp.concatenate([ref[pl.ds(c, rows, stride=CHUNKS), :] for c in range(CHUNKS)], axis=1)


def _ada_kernel(c_ref, w_ref, b_ref, o_ref):
    a = _silu(c_ref[...])
    o_ref[0] = _dot(a, w_ref[0]) + b_ref[0]


def _ada_call(c8, w_ada, b_ada):
    tn = 1536
    n_out = N_MOD * D_MODEL
    return pl.pallas_call(
        _ada_kernel,
        grid=(DEPTH, n_out // tn),
        in_specs=[
            pl.BlockSpec((8, D_MODEL), lambda l, j: (0, 0)),
            pl.BlockSpec((1, D_MODEL, tn), lambda l, j: (l, 0, j)),
            pl.BlockSpec((1, 1, tn), lambda l, j: (l, 0, j)),
        ],
        out_specs=pl.BlockSpec((1, 8, tn), lambda l, j: (l, 0, j)),
        out_shape=jax.ShapeDtypeStruct((DEPTH, 8, n_out), F32),
        compiler_params=_cparams(("arbitrary", "arbitrary")),
        name="ada",
    )(c8, w_ada, b_ada.reshape(DEPTH, 1, n_out))


def _rms(x):
    return x * lax.rsqrt(jnp.mean(x * x, axis=-1, keepdims=True) + EPS)


def _norm_mod_kernel(h_ref, g_ref, sc_ref, sh_ref, o_ref):
    y = _rms(h_ref[...]) * g_ref[...]
    o_ref[...] = (y * (1.0 + sc_ref[0]) + sh_ref[0]).astype(o_ref.dtype)


def _ada_spec(mod_idx, row0, tiles_per_row):
    return pl.BlockSpec((1, 1, D_MODEL), lambda i: (row0 + i // tiles_per_row, 0, mod_idx))


def _norm_mod_call(h, g, ada, sc_idx, sh_idx, row0, tiles_per_row):
    m = h.shape[0]
    return pl.pallas_call(
        _norm_mod_kernel,
        grid=(m // TOK_TILE,),
        in_specs=[
            pl.BlockSpec((TOK_TILE, D_MODEL), lambda i: (i, 0)),
            pl.BlockSpec((1, D_MODEL), lambda i: (0, 0)),
            _ada_spec(sc_idx, row0, tiles_per_row),
            _ada_spec(sh_idx, row0, tiles_per_row),
        ],
        out_specs=pl.BlockSpec((TOK_TILE, D_MODEL), lambda i: (i, 0)),
        out_shape=jax.ShapeDtypeStruct((m, D_MODEL), BF16),
        compiler_params=_cparams(("arbitrary",)),
        name="norm_mod",
    )(h, g, ada, ada)


def _final_norm_kernel(h_ref, g_ref, o_ref):
    o_ref[...] = _rms(h_ref[...]) * g_ref[...]


def _final_norm_call(h, g):
    m = h.shape[0]
    return pl.pallas_call(
        _final_norm_kernel,
        grid=(m // TOK_TILE,),
        in_specs=[
            pl.BlockSpec((TOK_TILE, D_MODEL), lambda i: (i, 0)),
            pl.BlockSpec((1, D_MODEL), lambda i: (0, 0)),
        ],
        out_specs=pl.BlockSpec((TOK_TILE, D_MODEL), lambda i: (i, 0)),
        out_shape=jax.ShapeDtypeStruct((m, D_MODEL), F32),
        compiler_params=_cparams(("arbitrary",)),
        name="final_norm",
    )(h, g)


IN_PROJ_ROWS = 1024


def _in_proj_kernel(a_ref, w_ref, o_ref):
    w = w_ref[...].astype(BF16)
    for r in range(a_ref.shape[0] // IN_PROJ_ROWS):
        rows = slice(r * IN_PROJ_ROWS, (r + 1) * IN_PROJ_ROWS)
        o_ref[rows, :] = _dot(a_ref[rows, :], w)


def _src_col_blk(j):
    return jnp.where(j < N_GATE_BLKS, j + SRC_GATE_BLK, j - N_GATE_BLKS)


def _in_proj_call(n, w_in, layer):
    m = n.shape[0]
    return pl.pallas_call(
        _in_proj_kernel,
        grid=(N_COL_BLKS,),
        in_specs=[
            pl.BlockSpec((m, D_MODEL), lambda j: (0, 0)),
            pl.BlockSpec((None, D_MODEL, COL_BLK), lambda j: (layer, 0, _src_col_blk(j))),
        ],
        out_specs=pl.BlockSpec((m, COL_BLK), lambda j: (0, j)),
        out_shape=jax.ShapeDtypeStruct((m, D_IN), F32),
        compiler_params=_cparams(("arbitrary",)),
        name="in_proj",
    )(n, w_in)


def _softmax_pv(s_list, v_list):
    m = s_list[0].max(axis=-1, keepdims=True)
    for s in s_list[1:]:
        m = jnp.maximum(m, s.max(axis=-1, keepdims=True))
    num = None
    den = None
    for s, v in zip(s_list, v_list):
        p = jnp.exp(s - m)
        d = p.sum(axis=-1, keepdims=True)
        o = _dot(p.astype(BF16), v.astype(BF16))
        num = o if num is None else num + o
        den = d if den is None else den + d
    return num / den


def _ctx_attn_kernel(q_ref, k_ref, v_ref, o_ref):
    scale = DH_A ** -0.5
    for h in range(H_A):
        sl = slice(h * DH_A, (h + 1) * DH_A)
        s = _dot_nt(q_ref[:, sl].astype(BF16), k_ref[:, sl].astype(BF16)) * scale
        o_ref[:, sl] = _softmax_pv([s], [v_ref[:, sl]])


def _ctx_attn_call(z, seq):
    m = z.shape[0]
    spec = lambda blk: pl.BlockSpec((seq, D_A), lambda b: (b, blk))
    return pl.pallas_call(
        _ctx_attn_kernel,
        grid=(m // seq,),
        in_specs=[spec(B_QA), spec(B_KA), spec(B_VA)],
        out_specs=pl.BlockSpec((seq, D_A), lambda b: (b, 0)),
        out_shape=jax.ShapeDtypeStruct((m, D_A), F32),
        compiler_params=_cparams(("arbitrary",)),
        name="ctx_attn",
    )(z, z, z)


def _na_key_row0(rb, rows):
    return jnp.clip(NA_QROWS * rb - (NA_KROWS - WIN_H) , 0, rows - NA_KROWS)


def _na_attn_kernel(q_ref, k_ref, v_ref, ck_ref, cv_ref, bias_ref, o_ref, *, rows):
    scale = DH_A ** -0.5
    rb = pl.program_id(1)
    k0 = pl.multiple_of(_na_key_row0(rb, rows) * GRID_W, GRID_W)
    for h in range(H_A):
        sl = slice(h * DH_A, (h + 1) * DH_A)
        q = q_ref[:, sl].astype(BF16)
        kw = k_ref[pl.ds(k0, NA_TK), sl]
        vw = v_ref[pl.ds(k0, NA_TK), sl]
        s_lat = _dot_nt(q, kw.astype(BF16)) * scale + bias_ref[h]
        s_ctx = _dot_nt(q, ck_ref[:, sl].astype(BF16)) * scale
        o_ref[:, sl] = _softmax_pv([s_lat, s_ctx], [vw, cv_ref[:, sl]])


def _na_bias_pattern(rb, n_rb):
    return jnp.where(rb == 0, 0, jnp.where(rb == n_rb - 1, 2, 1))


def _na_bias_tables(rpb_l, rows):
    n_rb = rows // NA_QROWS
    qc = np.arange(GRID_W)
    q_start = np.clip(qc - WIN_W // 2, 0, GRID_W - WIN_W)
    kc = np.arange(GRID_W)
    valid_c = (kc[None, :] >= q_start[:, None]) & (kc[None, :] < q_start[:, None] + WIN_W)
    pad = GRID_W - WIN_W
    rpb_pad = jnp.pad(rpb_l.astype(F32), ((0, 0), (0, 0), (pad, pad)))
    col_tab = jnp.stack([rpb_pad[:, :, GRID_W - 1 - c:2 * GRID_W - 1 - c] for c in range(GRID_W)], axis=2)
    col_tab = jnp.where(valid_c, col_tab, NEG_BIG)
    masked = jnp.full((H_A, GRID_W, GRID_W), NEG_BIG, F32)
    tabs = []
    for rb in (0, 1, n_rb - 1):
        k_row0 = int(np.clip(NA_QROWS * rb - (NA_KROWS - WIN_H), 0, rows - NA_KROWS))
        q_rows = []
        for i in range(NA_QROWS):
            r = NA_QROWS * rb + i
            w0 = int(np.clip(r - WIN_H // 2, 0, rows - WIN_H))
            blocks = []
            for j in range(NA_KROWS):
                kr = k_row0 + j
                in_window = w0 <= kr < w0 + WIN_H
                blocks.append(col_tab[:, kr - r + WIN_H - 1] if in_window else masked)
            q_rows.append(jnp.concatenate(blocks, axis=2))
        tabs.append(jnp.concatenate(q_rows, axis=1))
    return jnp.stack(tabs)


def _na_attn_call(z, ck, cv, bias, bsz, seq):
    rows = seq // GRID_W
    n_rb = rows // NA_QROWS
    m = z.shape[0]
    kv_spec = lambda blk: pl.BlockSpec((seq, D_A), lambda b, r: (b, blk))
    c_spec = pl.BlockSpec((None, ck.shape[1], D_A), lambda b, r: (b, 0, 0))
    return pl.pallas_call(
        functools.partial(_na_attn_kernel, rows=rows),
        grid=(bsz, n_rb),
        in_specs=[
            pl.BlockSpec((NA_TQ, D_A), lambda b, r: (b * n_rb + r, B_QA)),
            kv_spec(B_KA), kv_spec(B_VA), c_spec, c_spec,
            pl.BlockSpec((None, H_A, NA_TQ, NA_TK), lambda b, r: (_na_bias_pattern(r, n_rb), 0, 0, 0)),
        ],
        out_specs=pl.BlockSpec((NA_TQ, D_A), lambda b, r: (b * n_rb + r, 0)),
        out_shape=jax.ShapeDtypeStruct((m, D_A), F32),
        compiler_params=_cparams(("arbitrary", "arbitrary")),
        name="na_attn",
    )(z, z, z, ck, cv, bias)


def _log1p(x):
    return jnp.log1p(x)


def _hgrn_gates(z, lb):
    log_sig = jnp.minimum(z, 0.0) - _log1p(jnp.exp(-jnp.abs(z)))
    a = _log1p(-lb) + log_sig
    b = jnp.log(jnp.maximum(lb, LOG_FLOOR))
    logf = jnp.maximum(a, b) + _log1p(jnp.exp(-jnp.abs(a - b)))
    key = (1.0 - lb) * _sigmoid(-z)
    return logf, key


def _lower_bound(p_ref, layer):
    p = p_ref[...]
    e = jnp.exp(p - jnp.max(p, axis=0, keepdims=True))
    sm = e / jnp.sum(e, axis=0, keepdims=True)
    lb = jnp.zeros((1, DK_C), F32)
    for j in range(1, layer + 1):
        lb = lb + sm[j:j + 1]
    return lb


def _chunk_cumsum(x, seq, reverse):
    pos = lax.broadcasted_iota(jnp.int32, x.shape, 0) % CHUNK
    s = 1
    while s < CHUNK:
        if reverse:
            x = x + jnp.where(pos < CHUNK - s, pltpu.roll(x, seq - s, axis=0), 0.0)
        else:
            x = x + jnp.where(pos >= s, pltpu.roll(x, s, axis=0), 0.0)
        s *= 2
    return x


HALF = CHUNK // 2
HGRN_UNROLL = 4


def _hgrn_chunk(q_ref, v_ref, b_s, k_s, o_s, r0, st, reverse):
    b = b_s[pl.ds(r0, CHUNK), :]
    q = q_ref[pl.ds(r0, CHUNK), :]
    k = k_s[pl.ds(r0, CHUNK), :]
    v = v_ref[pl.ds(r0, CHUNK), :]
    o = _dot_nt(q * jnp.exp(b), st)
    t_iota = lax.broadcasted_iota(jnp.int32, (HALF, 1), 0)
    o_half = [o[:HALF], o[HALF:]]
    for s in range(CHUNK):
        b_row = b_s[pl.ds(r0 + s, 1), :]
        k_row = k_s[pl.ds(r0 + s, 1), :]
        v_row = v_ref[pl.ds(r0 + s, 1), :]
        for half in range(2):
            t0 = half * HALF
            if (t0 + HALF - 1 < s) if not reverse else (t0 > s):
                continue
            sl = slice(t0, t0 + HALF)
            decay = jnp.exp(jnp.minimum(b[sl] - b_row, 0.0))
            col = jnp.sum(q[sl] * (k_row * decay), axis=-1, keepdims=True)
            if not ((t0 >= s) if not reverse else (t0 + HALF - 1 <= s)):
                keep = (t_iota + t0 >= s) if not reverse else (t_iota + t0 <= s)
                col = jnp.where(keep, col, 0.0)
            o_half[half] = o_half[half] + col * v_row
    o_s[pl.ds(r0, CHUNK), :] = jnp.concatenate(o_half, axis=0)
    edge = 0 if reverse else CHUNK - 1
    b_edge = b[edge:edge + 1, :]
    kh = k * jnp.exp(b_edge - b)
    kv_t = lax.dot_general(v, kh, (((0,), (0,)), ((), ())), preferred_element_type=F32)
    return st * jnp.exp(b_edge) + kv_t


def _hgrn_kernel(q_ref, zf_ref, zb_ref, i_ref, og_ref, lbf_ref, lbb_ref, g_ref, s0_ref,
                 o_ref, sfin_ref, of_s, ob_s, bf_s, bb_s, kf_s, kb_s, *, seq, layer):
    n_chunks = seq // CHUNK
    logf, key = _hgrn_gates(zf_ref[...], _lower_bound(lbf_ref, layer))
    bf_s[...] = _chunk_cumsum(logf, seq, False)
    kf_s[...] = key
    logf, key = _hgrn_gates(zb_ref[...], _lower_bound(lbb_ref, layer))
    bb_s[...] = _chunk_cumsum(logf, seq, True)
    kb_s[...] = key

    def body(it, carry):
        st_f, st_b = carry
        for u in range(HGRN_UNROLL):
            n = it * HGRN_UNROLL + u
            st_f = _hgrn_chunk(q_ref, i_ref, bf_s, kf_s, of_s, pl.multiple_of(n * CHUNK, CHUNK), st_f, False)
            st_b = _hgrn_chunk(q_ref, i_ref, bb_s, kb_s, ob_s,
                               pl.multiple_of((n_chunks - 1 - n) * CHUNK, CHUNK), st_b, True)
        return st_f, st_b

    st_f, st_b = lax.fori_loop(0, n_chunks // HGRN_UNROLL, body, (s0_ref[0].T, s0_ref[1].T))
    sfin_ref[0] = st_f.T
    sfin_ref[1] = st_b.T
    o_ref[...] = _rms(of_s[...] + ob_s[...]) * g_ref[...] * _silu(og_ref[...])


def _hgrn_call(z, lb_fwd, lb_bwd, norm_g, s0, bsz, seq, layer):
    m = z.shape[0]
    w = DK_C
    per = COL_BLK // w
    zspec = lambda blk: pl.BlockSpec((seq, w), lambda b, h: (b, blk * per + h))
    st_spec = pl.BlockSpec((None, 2, None, DK_C, DK_C), lambda b, h: (b, 0, h, 0, 0))
    return pl.pallas_call(
        functools.partial(_hgrn_kernel, seq=seq, layer=layer),
        grid=(bsz, H_C),
        in_specs=[
            zspec(B_QC), zspec(B_ZF), zspec(B_ZB), zspec(B_IC), zspec(B_OC),
            pl.BlockSpec((DEPTH, w), lambda b, h: (0, h)),
            pl.BlockSpec((DEPTH, w), lambda b, h: (0, h)),
            pl.BlockSpec((1, w), lambda b, h: (0, h)),
            st_spec,
        ],
        out_specs=[pl.BlockSpec((seq, w), lambda b, h: (b, h)), st_spec],
        out_shape=[jax.ShapeDtypeStruct((m, H_C * w), F32),
                   jax.ShapeDtypeStruct((bsz, 2, H_C, DK_C, DK_C), F32)],
        scratch_shapes=[pltpu.VMEM((seq, w), F32)] * 6,
        compiler_params=_cparams(("arbitrary", "arbitrary")),
        name="hgrn",
    )(z, z, z, z, z, lb_fwd, lb_bwd, norm_g, s0)


def _mix_kernel(oa_ref, oc_ref, u_ref, gb_ref, gc_ref, up_ref, gcp_ref, un_ref, gcn_ref,
                ga_ref, gbr_ref, gcr_ref, h_ref, wconv_ref, wa_ref, wb_ref, wc_ref, wout_ref,
                wrt_ref, g1_ref, n2g_ref, sc2_ref, sh2_ref,
                hout_ref, n2_ref, lgt_ref, *, seq):
    tm = u_ref.shape[0]
    zc = gc_ref[...] * u_ref[...]
    row = lax.broadcasted_iota(jnp.int32, zc.shape, 0)
    pos = (pl.program_id(0) * tm + row) % seq
    zp = jnp.where(row == 0, gcp_ref[7:8, :] * up_ref[7:8, :], pltpu.roll(zc, 1, axis=0))
    zn = jnp.where(row == tm - 1, gcn_ref[0:1, :] * un_ref[0:1, :], pltpu.roll(zc, tm - 1, axis=0))
    zp = jnp.where(pos == 0, 0.0, zp)
    zn = jnp.where(pos == seq - 1, 0.0, zn)
    w = wconv_ref[...]
    ob = gb_ref[...] * (w[0:1] * zp + w[1:2] * zc + w[2:3] * zn)
    ya = _dot(oa_ref[...].astype(BF16), wa_ref[...])
    yb = _dot(ob.astype(BF16), wb_ref[...])
    yc = _dot(oc_ref[...].astype(BF16), wc_ref[...])
    pre = _sigmoid(ga_ref[...]) * ya + _sigmoid(gbr_ref[...]) * yb + _sigmoid(gcr_ref[...]) * yc
    hn = h_ref[...] + g1_ref[0] * _dot(pre.astype(BF16), wout_ref[...])
    hout_ref[...] = hn
    n2 = _rms(hn) * n2g_ref[...] * (1.0 + sc2_ref[0]) + sh2_ref[0]
    _store_token_tiles(n2_ref, n2)
    lgt_ref[...] = _dot_nt(wrt_ref[...], n2, precision=lax.Precision.HIGHEST)


MIX_TILE = 512


def _mix_call(o_a, o_c, z, h, w_conv, wa, wb, wc, wout, wrt, n2g, ada, row0, toks_per_row, seq):
    m = h.shape[0]
    tm = MIX_TILE
    nt = m // tm
    tiles_per_row = toks_per_row // tm
    r8 = tm // 8
    zblk = lambda blk: pl.BlockSpec((tm, COL_BLK), lambda i: (i, blk))
    zprev = lambda blk: pl.BlockSpec((8, COL_BLK), lambda i: (jnp.maximum(i * r8 - 1, 0), blk))
    znext = lambda blk: pl.BlockSpec((8, COL_BLK), lambda i: (jnp.minimum((i + 1) * r8, m // 8 - 1), blk))
    zgate = lambda blk: pl.BlockSpec((tm, D_MODEL), lambda i: (i, blk // 2))
    full = lambda a: pl.BlockSpec(a.shape, lambda i: (0,) * a.ndim)
    tok = lambda wdt: pl.BlockSpec((tm, wdt), lambda i: (i, 0))
    return pl.pallas_call(
        functools.partial(_mix_kernel, seq=seq),
        grid=(nt,),
        in_specs=[
            tok(D_A), tok(D_A), zblk(B_U), zblk(B_GB), zblk(B_GC),
            zprev(B_U), zprev(B_GC), znext(B_U), znext(B_GC),
            zgate(B_GA), zgate(B_GBR), zgate(B_GCR), tok(D_MODEL),
            full(w_conv), full(wa), full(wb), full(wc), full(wout), full(wrt),
            _ada_spec(2, row0, tiles_per_row), full(n2g),
            _ada_spec(4, row0, tiles_per_row), _ada_spec(3, row0, tiles_per_row),
        ],
        out_specs=[tok(D_MODEL), pl.BlockSpec((tm * CHUNKS, LANES), lambda i: (i, 0)),
                   pl.BlockSpec((N_EXPERTS, tm), lambda i: (0, i))],
        out_shape=[jax.ShapeDtypeStruct((m, D_MODEL), F32),
                   jax.ShapeDtypeStruct((m * CHUNKS, LANES), F32),
                   jax.ShapeDtypeStruct((N_EXPERTS, m), F32)],
        compiler_params=_cparams(("arbitrary",)),
        name="mix",
    )(o_a, o_c, z, z, z, z, z, z, z, z, z, z, h, w_conv, wa, wb, wc, wout, wrt, ada, n2g, ada, ada)


def _first_max(x, axes_iota, size):
    m = x
    for ax in range(x.ndim - 1):
        m = jnp.max(m, axis=ax, keepdims=True)
    first = jnp.where(x == m, axes_iota, size)
    f = first
    for ax in range(x.ndim - 1):
        f = jnp.min(f, axis=ax, keepdims=True)
    return axes_iota == f, f


def _first_max_mask(x, axes_iota, size):
    return _first_max(x, axes_iota, size)[0]


def _routing_kernel(lgt_ref, bias_ref, wpick_ref, eidx_ref, rank_ref, cnt_ref, carry_s):
    tn = lgt_ref.shape[-1]

    @pl.when(pl.program_id(0) == 0)
    def _():
        carry_s[...] = jnp.zeros(carry_s.shape, F32)

    s = _sigmoid(lgt_ref[...])
    sel = (s + bias_ref[...]).reshape(N_GROUPS, GROUP_SIZE, tn)
    s = s.reshape(N_GROUPS, GROUP_SIZE, tn)
    neg_inf = -jnp.inf
    mem_iota = lax.broadcasted_iota(jnp.int32, sel.shape, 1)
    m1 = jnp.max(sel, axis=1, keepdims=True)
    first = jnp.min(jnp.where(sel == m1, mem_iota, GROUP_SIZE), axis=1, keepdims=True)
    m2 = jnp.max(jnp.where(mem_iota == first, neg_inf, sel), axis=1, keepdims=True)
    grp = (m1 + m2).reshape(N_GROUPS, tn)
    g_iota = lax.broadcasted_iota(jnp.int32, grp.shape, 0)
    gmask = jnp.zeros(grp.shape, jnp.bool_)
    for _ in range(TOPK_GROUPS):
        pick = _first_max_mask(grp, g_iota, N_GROUPS)
        gmask = gmask | pick
        grp = jnp.where(pick, neg_inf, grp)
    cand = jnp.where(gmask.reshape(N_GROUPS, 1, tn), sel, NEG_BIG)
    e_iota = lax.broadcasted_iota(jnp.int32, sel.shape, 0) * GROUP_SIZE + mem_iota
    chosen = jnp.zeros(sel.shape, jnp.bool_)
    picked = []
    for _ in range(TOP_K):
        pick, idx = _first_max(cand, e_iota, N_EXPERTS)
        chosen = chosen | pick
        cand = jnp.where(pick, neg_inf, cand)
        picked.append(idx)
    w = jnp.where(chosen, s, 0.0)
    tot = jnp.sum(jnp.sum(w, axis=1, keepdims=True), axis=0, keepdims=True)
    w = w / tot * ROUTED_SCALE
    chosen_f = jnp.where(chosen, 1.0, 0.0).reshape(N_EXPERTS, tn)
    earlier = (lax.broadcasted_iota(jnp.int32, (tn, tn), 0) < lax.broadcasted_iota(jnp.int32, (tn, tn), 1))
    rank = _dot(chosen_f.astype(BF16), jnp.where(earlier, 1.0, 0.0).astype(BF16)) + carry_s[:, 0:1]
    rank = rank.reshape(N_GROUPS, GROUP_SIZE, tn)
    of_pick = lambda v, idx: jnp.sum(jnp.sum(jnp.where(e_iota == idx, v, 0.0), axis=1, keepdims=True),
                                     axis=0, keepdims=True).reshape(1, tn)
    eidx_ref[...] = jnp.concatenate([idx.reshape(1, tn) for idx in picked], axis=0)
    rank_ref[...] = jnp.concatenate([of_pick(rank, idx) for idx in picked], axis=0).astype(jnp.int32)
    wpick_ref[...] = jnp.concatenate([of_pick(w, idx) for idx in picked], axis=0)
    carry_s[...] = carry_s[...] + jnp.sum(chosen_f, axis=1, keepdims=True)
    cnt_ref[...] = carry_s[...]


def _routing_call(lgt, b_router_l):
    m = lgt.shape[1]
    tn = 256
    pick_spec = pl.BlockSpec((TOP_K, tn), lambda i: (0, i))
    cnt_spec = pl.BlockSpec((N_EXPERTS, 128), lambda i: (0, 0))
    return pl.pallas_call(
        _routing_kernel,
        grid=(m // tn,),
        in_specs=[pl.BlockSpec((N_EXPERTS, tn), lambda i: (0, i)),
                  pl.BlockSpec((N_EXPERTS, tn), lambda i: (0, 0))],
        out_specs=[pick_spec, pick_spec, pick_spec, cnt_spec],
        out_shape=[jax.ShapeDtypeStruct((TOP_K, m), F32),
                   jax.ShapeDtypeStruct((TOP_K, m), jnp.int32),
                   jax.ShapeDtypeStruct((TOP_K, m), jnp.int32),
                   jax.ShapeDtypeStruct((N_EXPERTS, 128), F32)],
        scratch_shapes=[pltpu.VMEM((N_EXPERTS, 128), F32)],
        compiler_params=_cparams(("arbitrary",)),
        name="routing",
    )(lgt, jnp.broadcast_to(b_router_l[:, None], (N_EXPERTS, tn)))


MOE_TM = 256
ROW_GROUP = 16
assert TOP_K == CHUNKS and TOP_K & (TOP_K - 1) == 0


def _token_row(idx):
    return pl.multiple_of(idx & ~(TOP_K - 1), CHUNKS)


def _dispatch_plan(eidx, rank, cnt, n_tiles_max):
    counts = cnt[:, 0].astype(jnp.int32)
    tiles = (counts + MOE_TM - 1) // MOE_TM
    tile_end = jnp.cumsum(tiles)
    tile_start = tile_end - tiles
    n_tiles = tile_end[-1]
    experts = jnp.arange(N_EXPERTS, dtype=jnp.int32)
    base = jnp.sum(jnp.where(eidx[:, :, None] == experts, tile_start * MOE_TM, 0), axis=-1)
    pos = (rank + base).T.reshape(-1)
    ti = jnp.arange(n_tiles_max, dtype=jnp.int32)
    tile_e = jnp.sum((ti[:, None] >= tile_end[None, :]).astype(jnp.int32), axis=1)
    last_e = jnp.sum(jnp.where(ti == n_tiles - 1, tile_e, 0))
    tile_e = jnp.where(ti < n_tiles, tile_e, last_e).astype(jnp.int32)
    first = ((ti == 0) | (tile_e != jnp.roll(tile_e, 1))) & (ti < n_tiles)
    slot = (jnp.cumsum(first.astype(jnp.int32)) - 1) % 2
    next_tile = jnp.sum(jnp.where(tile_e[:, None] == experts, tile_end, 0), axis=1)
    next_e = jnp.sum(jnp.where(next_tile[:, None] == ti[None, :], tile_e[None, :], 0), axis=1)
    next_e = jnp.where(next_tile < n_tiles, next_e, -1)
    i32 = lambda v: v.astype(jnp.int32)
    return pos, (tile_e, i32(first), i32(next_e), i32(slot), i32(n_tiles).reshape(1))


def _moe_step(tile_g, tile_s, rowidx_s, wpick_ref, x_v, acc_v, g_w, g_r, y_w, y_r, wg_bf, wu_bf, wd_bf):
    row_g = tile_g * MOE_TM
    for r in range(MOE_TM):
        g_w[pl.ds(r * CHUNKS, CHUNKS), :] = x_v[pl.ds(_token_row(rowidx_s[row_g + r]), CHUNKS), :]

    x = _load_token_tiles(g_r, MOE_TM).astype(BF16)
    hid = _silu(_dot(x, wg_bf[...])) * _dot(x, wu_bf[...])
    y = _dot(hid.astype(BF16), wd_bf[...])
    for rb in range(MOE_TM // 8):
        for c in range(CHUNKS):
            y_w[pl.ds((rb * CHUNKS + c) * 8, 8), :] = y[rb * 8:(rb + 1) * 8, c * LANES:(c + 1) * LANES]

    row_s = tile_s * MOE_TM
    for j in range(MOE_TM // ROW_GROUP):
        slots, rows_v = [], []
        for u in range(ROW_GROUP):
            r = j * ROW_GROUP + u
            idx = rowidx_s[row_s + r]
            slots.append(_token_row(idx))
            rows_v.append(y_r[pl.ds((r // 8) * (8 * CHUNKS) + r % 8, CHUNKS, stride=8), :] * wpick_ref[idx])
        olds = [acc_v[pl.ds(s, CHUNKS), :] for s in slots]
        for s, o, v in zip(slots, olds, rows_v):
            acc_v[pl.ds(s, CHUNKS), :] = o + v


def _moe_kernel(te_ref, first_ref, next_ref, slot_ref, nt_ref, pos_ref, wpick_ref, x_hbm, fill_hbm,
                wg_hbm, wu_hbm, wd_hbm, out_hbm,
                rowidx_s, x_v, acc_v, g0_s, g1_s, y0_s, y1_s, wg_f, wu_f, wd_f, wg_bf, wu_bf, wd_bf, sem, wsem,
                *, layer):
    def weight_copies(e, slot):
        return [pltpu.make_async_copy(hbm.at[layer, e], buf.at[slot], wsem.at[slot, k])
                for k, (hbm, buf) in enumerate(((wg_hbm, wg_f), (wu_hbm, wu_f), (wd_hbm, wd_f)))]

    s = pl.program_id(0)
    n_tok = x_hbm.shape[0] // CHUNKS
    n_tiles_max = te_ref.shape[0]
    zero_blk = 512

    @pl.when(s == 0)
    def _prologue():
        x_copy = pltpu.make_async_copy(x_hbm, x_v.at[pl.ds(0, n_tok * CHUNKS)], sem.at[0])
        fill_copy = pltpu.make_async_copy(fill_hbm, rowidx_s, sem.at[1])
        x_copy.start()
        fill_copy.start()
        for cp in weight_copies(te_ref[0], 0):
            cp.start()

        def zero(j, carry):
            acc_v[pl.ds(pl.multiple_of(j * zero_blk, zero_blk), zero_blk), :] = jnp.zeros((zero_blk, LANES), F32)
            return carry

        lax.fori_loop(0, n_tok * CHUNKS // zero_blk, zero, 0)
        acc_v[pl.ds(n_tok * CHUNKS, CHUNKS), :] = jnp.zeros((CHUNKS, LANES), F32)
        x_v[pl.ds(n_tok * CHUNKS, CHUNKS), :] = jnp.zeros((CHUNKS, LANES), F32)
        g1_s[...] = jnp.zeros(g1_s.shape, F32)
        y0_s[...] = jnp.zeros(y0_s.shape, F32)
        fill_copy.wait()

        def invert(j, carry):
            for u in range(ROW_GROUP):
                idx = j * ROW_GROUP + u
                rowidx_s[pos_ref[idx]] = idx
            return carry

        lax.fori_loop(0, n_tok * TOP_K // ROW_GROUP, invert, 0)
        x_copy.wait()

    active = s < nt_ref[0] + 2
    ffn_tile = jnp.clip(s - 1, 0, n_tiles_max - 1)

    @pl.when(active & (first_ref[ffn_tile] == 1) & (s != 1))
    def _next_expert():
        slot = slot_ref[ffn_tile]
        for cp in weight_copies(te_ref[ffn_tile], slot):
            cp.wait()
        wg_bf[...] = wg_f[slot].astype(BF16)
        wu_bf[...] = wu_f[slot].astype(BF16)
        wd_bf[...] = wd_f[slot].astype(BF16)

        @pl.when(next_ref[ffn_tile] >= 0)
        def _prefetch():
            for cp in weight_copies(next_ref[ffn_tile], 1 - slot):
                cp.start()

    tile_g = jnp.minimum(s, n_tiles_max - 1)
    tile_s = jnp.maximum(s - 2, 0)
    common = (rowidx_s, wpick_ref, x_v, acc_v)
    weights = (wg_bf, wu_bf, wd_bf)

    @pl.when(active & (s % 2 == 0))
    def _even():
        _moe_step(tile_g, tile_s, *common, g0_s, g1_s, y1_s, y0_s, *weights)

    @pl.when(active & (s % 2 == 1))
    def _odd():
        _moe_step(tile_g, tile_s, *common, g1_s, g0_s, y0_s, y1_s, *weights)

    @pl.when(s == pl.num_programs(0) - 1)
    def _epilogue():
        out_copy = pltpu.make_async_copy(acc_v.at[pl.ds(0, n_tok * CHUNKS)], out_hbm, sem.at[2])
        out_copy.start()
        out_copy.wait()


def _moe_call(x, wpick, eidx, rank, cnt, w_gate, w_up, w_down, layer):
    m = x.shape[0] // CHUNKS
    n_rows = TOP_K * m
    n_tiles_max = n_rows // MOE_TM + N_EXPERTS
    n_slots = n_tiles_max * MOE_TM
    pos, tables = _dispatch_plan(eidx, rank, cnt, n_tiles_max)
    wflat = jnp.concatenate([wpick.T.reshape(-1), jnp.zeros((TOP_K,), F32)])
    anyspec = pl.BlockSpec(memory_space=pl.ANY)
    smem = pl.BlockSpec(memory_space=pltpu.SMEM)
    tile_buf = pltpu.VMEM((MOE_TM * CHUNKS, LANES), F32)
    grid_spec = pltpu.PrefetchScalarGridSpec(
        num_scalar_prefetch=len(tables),
        grid=(n_tiles_max + 2,),
        in_specs=[smem, smem, anyspec, anyspec, anyspec, anyspec, anyspec],
        out_specs=anyspec,
        scratch_shapes=[
            pltpu.SMEM((n_slots,), jnp.int32),
            pltpu.VMEM(((m + 1) * CHUNKS, LANES), F32),
            pltpu.VMEM(((m + 1) * CHUNKS, LANES), F32),
            tile_buf, tile_buf, tile_buf, tile_buf,
            pltpu.VMEM((2, D_MODEL, D_EXPERT), F32),
            pltpu.VMEM((2, D_MODEL, D_EXPERT), F32),
            pltpu.VMEM((2, D_EXPERT, D_MODEL), F32),
            pltpu.VMEM((D_MODEL, D_EXPERT), BF16),
            pltpu.VMEM((D_MODEL, D_EXPERT), BF16),
            pltpu.VMEM((D_EXPERT, D_MODEL), BF16),
            pltpu.SemaphoreType.DMA((3,)),
            pltpu.SemaphoreType.DMA((2, 3)),
        ],
    )
    return pl.pallas_call(
        functools.partial(_moe_kernel, layer=layer),
        grid_spec=grid_spec,
        out_shape=jax.ShapeDtypeStruct((m * CHUNKS, LANES), F32),
        compiler_params=_cparams(("arbitrary",)),
        name="moe",
    )(*tables, pos, wflat, x, jnp.full((n_slots,), n_rows, jnp.int32), w_gate, w_up, w_down)


def _moe_finish_kernel(x_ref, routed_ref, h_ref, wsg_ref, wsu_ref, wsd_ref, g2_ref, o_ref):
    rows = h_ref.shape[0]
    x = _load_token_tiles(x_ref, rows).astype(BF16)
    hid = _silu(_dot(x, wsg_ref[...])) * _dot(x, wsu_ref[...])
    shared = _dot(hid.astype(BF16), wsd_ref[...])
    o_ref[...] = h_ref[...] + g2_ref[0] * (_load_token_tiles(routed_ref, rows) + shared)


def _moe_finish_call(x, routed, h, wsg, wsu, wsd, ada, row0, tiles_per_row):
    m = h.shape[0]
    tok = pl.BlockSpec((TOK_TILE, D_MODEL), lambda i: (i, 0))
    tiles = pl.BlockSpec((TOK_TILE * CHUNKS, LANES), lambda i: (i, 0))
    full = lambda a: pl.BlockSpec(a.shape, lambda i: (0,) * a.ndim)
    return pl.pallas_call(
        _moe_finish_kernel,
        grid=(m // TOK_TILE,),
        in_specs=[tiles, tiles, tok, full(wsg), full(wsu), full(wsd), _ada_spec(5, row0, tiles_per_row)],
        out_specs=tok,
        out_shape=jax.ShapeDtypeStruct((m, D_MODEL), F32),
        compiler_params=_cparams(("arbitrary",)),
        name="moe_finish",
    )(x, routed, h, wsg, wsu, wsd, ada)


def _stream(h, p, ada, row0, bsz, seq, attend, states):
    toks_per_row = seq if row0 > 0 else bsz * seq
    tiles_per_row = toks_per_row // TOK_TILE
    keys, vals, sts = [], [], []
    for l in range(DEPTH):
        ada_l = ada[l].reshape(8, 1, N_MOD * D_MODEL)
        n = _norm_mod_call(h, p['norm1_g'][l:l + 1], ada_l, 1, 0, row0, tiles_per_row)
        z = _in_proj_call(n, p['w_in'], l)
        o_a = attend(z, l)
        o_c, st = _hgrn_call(z, p['lb_fwd'], p['lb_bwd'], p['hgrn_norm_g'][l:l + 1], states[l], bsz, seq, l)
        h, n2, lgt = _mix_call(o_a, o_c, z, h, p['w_conv'][l], p['w_br_a'][l], p['w_br_b'][l], p['w_br_c'][l],
                               p['w_out'][l], p['w_router_t'][l], p['norm2_g'][l:l + 1], ada_l,
                               row0, toks_per_row, seq)
        wpick, eidx, rank, cnt = _routing_call(lgt, p['b_router'][l])
        routed = _moe_call(n2, wpick, eidx, rank, cnt, p['w_gate'], p['w_up'], p['w_down'], l)
        h = _moe_finish_call(n2, routed, h, p['w_sh_gate'][l], p['w_sh_up'][l], p['w_sh_down'][l],
                             ada_l, row0, tiles_per_row)
        keys.append(z[:, B_KA * COL_BLK:(B_KA + 1) * COL_BLK])
        vals.append(z[:, B_VA * COL_BLK:(B_VA + 1) * COL_BLK])
        sts.append(st)
    return h, keys, vals, sts


def kernel(x_prompt, x_sample, cache_k, cache_v, state_hgrn, c, c_ctx, norm1_g, norm2_g, w_ada, b_ada,
           w_in, rpb, w_conv, lb_fwd, lb_bwd, hgrn_norm_g, w_br_a, w_br_b, w_br_c, w_out, w_router,
           b_router, w_gate, w_up, w_down, w_sh_gate, w_sh_up, w_sh_down, final_norm_g):
    batch, seq, _ = x_prompt.shape
    dec_batch, dec_seq, _ = x_sample.shape
    past = cache_k.shape[2]
    p = dict(norm1_g=norm1_g, norm2_g=norm2_g, w_in=w_in, w_conv=w_conv, lb_fwd=lb_fwd, lb_bwd=lb_bwd,
             hgrn_norm_g=hgrn_norm_g,
             w_br_a=w_br_a.astype(BF16), w_br_b=w_br_b.astype(BF16), w_br_c=w_br_c.astype(BF16),
             w_out=w_out.astype(BF16), w_router_t=jnp.swapaxes(w_router, 1, 2), b_router=b_router,
             w_gate=w_gate, w_up=w_up, w_down=w_down,
             w_sh_gate=w_sh_gate.astype(BF16), w_sh_up=w_sh_up.astype(BF16), w_sh_down=w_sh_down.astype(BF16))

    c8 = jnp.zeros((8, D_MODEL), F32).at[0].set(c_ctx).at[1:1 + dec_batch].set(c)
    ada = _ada_call(c8, w_ada, b_ada)

    zero_states = [jnp.zeros((batch, 2, H_C, DK_C, DK_C), F32)] * DEPTH
    ctx_attend = lambda z, l: _ctx_attn_call(z, seq)
    h_ctx, keys, vals, sts = _stream(x_prompt.reshape(batch * seq, D_MODEL), p, ada, 0, batch, seq,
                                     ctx_attend, zero_states)
    y_prompt = _final_norm_call(h_ctx, final_norm_g.reshape(1, D_MODEL)).reshape(batch, seq, D_MODEL)
    new_cache_k = jnp.stack([k.reshape(batch, seq, H_A, DH_A) for k in keys], axis=1)
    new_cache_v = jnp.stack([v.reshape(batch, seq, H_A, DH_A) for v in vals], axis=1)
    new_state = jnp.stack(sts, axis=1)

    rows = dec_seq // GRID_W
    ck = cache_k.reshape(dec_batch, DEPTH, past, D_A)
    cv = cache_v.reshape(dec_batch, DEPTH, past, D_A)
    lat_states = [state_hgrn[:, l].astype(F32) for l in range(DEPTH)]

    def lat_attend(z, l):
        bias = _na_bias_tables(rpb[l], rows)
        return _na_attn_call(z, ck[:, l], cv[:, l], bias, dec_batch, dec_seq)

    h_lat, _, _, _ = _stream(x_sample.reshape(dec_batch * dec_seq, D_MODEL), p, ada, 1, dec_batch, dec_seq,
                             lat_attend, lat_states)
    y_sample = _final_norm_call(h_lat, final_norm_g.reshape(1, D_MODEL)).reshape(dec_batch, dec_seq, D_MODEL)
    return (y_prompt, y_sample, new_cache_k, new_cache_v, new_state)
```

```python
import functools

import numpy as np
import jax
import jax.numpy as jnp
from jax import lax
from jax.experimental import pallas as pl
from jax.experimental.pallas import tpu as pltpu

F32 = jnp.float32
BF16 = jnp.bfloat16

D_MODEL = 1024
DEPTH = 2
GRID_W = 64
H_A = 8
DH_A = 64
D_A = H_A * DH_A
WIN_H = 8
WIN_W = 16
D_CONV = 512
H_C = 4
DK_C = 128
CHUNK = 16
N_EXPERTS = 64
N_GROUPS = 8
GROUP_SIZE = N_EXPERTS // N_GROUPS
TOPK_GROUPS = 4
TOP_K = 8
D_EXPERT = 256
ROUTED_SCALE = 2.5
N_MOD = 6
EPS = 1e-6
NEG_BIG = -1e30
LOG_FLOOR = 1e-30

D_IN = 8704
COL_BLK = 512
N_COL_BLKS = D_IN // COL_BLK
SRC_GATE_BLK = 11
N_GATE_BLKS = 6
B_GA, B_GBR, B_GCR = 0, 2, 4
B_QA, B_KA, B_VA, B_U, B_GB, B_GC, B_QC, B_ZF, B_ZB, B_IC, B_OC = range(6, 17)

NA_QROWS = 4
NA_KROWS = 12
NA_TQ = NA_QROWS * GRID_W
NA_TK = NA_KROWS * GRID_W

TOK_TILE = 256
VMEM_LIMIT = 56 * 1024 * 1024


def _cparams(sem):
    return pltpu.CompilerParams(dimension_semantics=sem, vmem_limit_bytes=VMEM_LIMIT)


def _sigmoid(x):
    return 1.0 / (1.0 + jnp.exp(-x))


def _silu(x):
    return x * _sigmoid(x)


def _dot(a, b):
    return jnp.dot(a, b, preferred_element_type=F32)


def _dot_nt(a, b, precision=None):
    return lax.dot_general(a, b, (((1,), (1,)), ((), ())), preferred_element_type=F32,
                           precision=precision)


LANES = 128
CHUNKS = D_MODEL // LANES


def _store_token_tiles(ref, val):
    for rb in range(val.shape[0] // 8):
        for c in range(CHUNKS):
            ref[pl.ds(rb * 8 * CHUNKS + c, 8, stride=CHUNKS), :] = \
                val[rb * 8:(rb + 1) * 8, c * LANES:(c + 1) * LANES]


def _load_token_tiles(ref, rows):
    return jnp.concatenate([ref[pl.ds(c, rows, stride=CHUNKS), :] for c in range(CHUNKS)], axis=1)


def _ada_kernel(c_ref, w_ref, b_ref, o_ref):
    a = _silu(c_ref[...])
    o_ref[0] = _dot(a, w_ref[0]) + b_ref[0]


def _ada_call(c8, w_ada, b_ada):
    tn = 1536
    n_out = N_MOD * D_MODEL
    return pl.pallas_call(
        _ada_kernel,
        grid=(DEPTH, n_out // tn),
        in_specs=[
            pl.BlockSpec((8, D_MODEL), lambda l, j: (0, 0)),
            pl.BlockSpec((1, D_MODEL, tn), lambda l, j: (l, 0, j)),
            pl.BlockSpec((1, 1, tn), lambda l, j: (l, 0, j)),
        ],
        out_specs=pl.BlockSpec((1, 8, tn), lambda l, j: (l, 0, j)),
        out_shape=jax.ShapeDtypeStruct((DEPTH, 8, n_out), F32),
        compiler_params=_cparams(("arbitrary", "arbitrary")),
        name="ada",
    )(c8, w_ada, b_ada.reshape(DEPTH, 1, n_out))


def _rms(x):
    return x * lax.rsqrt(jnp.mean(x * x, axis=-1, keepdims=True) + EPS)


def _norm_mod_kernel(h_ref, g_ref, sc_ref, sh_ref, o_ref):
    y = _rms(h_ref[...]) * g_ref[...]
    o_ref[...] = (y * (1.0 + sc_ref[0]) + sh_ref[0]).astype(o_ref.dtype)


def _ada_spec(mod_idx, row0, tiles_per_row):
    return pl.BlockSpec((1, 1, D_MODEL), lambda i: (row0 + i // tiles_per_row, 0, mod_idx))


def _norm_mod_call(h, g, ada, sc_idx, sh_idx, row0, tiles_per_row):
    m = h.shape[0]
    return pl.pallas_call(
        _norm_mod_kernel,
        grid=(m // TOK_TILE,),
        in_specs=[
            pl.BlockSpec((TOK_TILE, D_MODEL), lambda i: (i, 0)),
            pl.BlockSpec((1, D_MODEL), lambda i: (0, 0)),
            _ada_spec(sc_idx, row0, tiles_per_row),
            _ada_spec(sh_idx, row0, tiles_per_row),
        ],
        out_specs=pl.BlockSpec((TOK_TILE, D_MODEL), lambda i: (i, 0)),
        out_shape=jax.ShapeDtypeStruct((m, D_MODEL), BF16),
        compiler_params=_cparams(("arbitrary",)),
        name="norm_mod",
    )(h, g, ada, ada)


IN_PROJ_ROWS = 1024


def _in_proj_kernel(a_ref, w_ref, o_ref):
    w = w_ref[...].astype(BF16)
    for r in range(a_ref.shape[0] // IN_PROJ_ROWS):
        rows = slice(r * IN_PROJ_ROWS, (r + 1) * IN_PROJ_ROWS)
        o_ref[rows, :] = _dot(a_ref[rows, :], w)


def _src_col_blk(j):
    return jnp.where(j < N_GATE_BLKS, j + SRC_GATE_BLK, j - N_GATE_BLKS)


def _in_proj_call(n, w_in, layer):
    m = n.shape[0]
    return pl.pallas_call(
        _in_proj_kernel,
        grid=(N_COL_BLKS,),
        in_specs=[
            pl.BlockSpec((m, D_MODEL), lambda j: (0, 0)),
            pl.BlockSpec((None, D_MODEL, COL_BLK), lambda j: (layer, 0, _src_col_blk(j))),
        ],
        out_specs=pl.BlockSpec((m, COL_BLK), lambda j: (0, j)),
        out_shape=jax.ShapeDtypeStruct((m, D_IN), F32),
        compiler_params=_cparams(("arbitrary",)),
        name="in_proj",
    )(n, w_in)


def _softmax_pv(s_list, v_list):
    m = s_list[0].max(axis=-1, keepdims=True)
    for s in s_list[1:]:
        m = jnp.maximum(m, s.max(axis=-1, keepdims=True))
    num = None
    den = None
    for s, v in zip(s_list, v_list):
        p = jnp.exp(s - m)
        d = p.sum(axis=-1, keepdims=True)
        o = _dot(p.astype(BF16), v.astype(BF16))
        num = o if num is None else num + o
        den = d if den is None else den + d
    return num / den


assert DH_A == 4 ** (DH_A.bit_length() // 2)


def _ctx_attn_kernel(q_ref, k_ref, v_ref, o_ref):
    scale = DH_A ** -0.5
    for h in range(H_A):
        sl = slice(h * DH_A, (h + 1) * DH_A)
        s = _dot_nt((q_ref[:, sl] * scale).astype(BF16), k_ref[:, sl].astype(BF16))
        o_ref[:, sl] = _softmax_pv([s], [v_ref[:, sl]])


def _ctx_attn_call(z, seq):
    m = z.shape[0]
    spec = lambda blk: pl.BlockSpec((seq, D_A), lambda b: (b, blk))
    return pl.pallas_call(
        _ctx_attn_kernel,
        grid=(m // seq,),
        in_specs=[spec(B_QA), spec(B_KA), spec(B_VA)],
        out_specs=pl.BlockSpec((seq, D_A), lambda b: (b, 0)),
        out_shape=jax.ShapeDtypeStruct((m, D_A), F32),
        compiler_params=_cparams(("arbitrary",)),
        name="ctx_attn",
    )(z, z, z)


def _na_key_row0(rb, rows):
    return jnp.clip(NA_QROWS * rb - (NA_KROWS - WIN_H) , 0, rows - NA_KROWS)


def _na_attn_kernel(q_ref, k_ref, v_ref, ck_ref, cv_ref, bias_ref, o_ref, *, rows):
    scale = DH_A ** -0.5
    rb = pl.program_id(1)
    k0 = pl.multiple_of(_na_key_row0(rb, rows) * GRID_W, GRID_W)
    for h in range(H_A):
        sl = slice(h * DH_A, (h + 1) * DH_A)
        q = (q_ref[:, sl] * scale).astype(BF16)
        kw = k_ref[pl.ds(k0, NA_TK), sl]
        vw = v_ref[pl.ds(k0, NA_TK), sl]
        s_lat = _dot_nt(q, kw.astype(BF16)) + bias_ref[h]
        s_ctx = _dot_nt(q, ck_ref[:, sl].astype(BF16))
        o_ref[:, sl] = _softmax_pv([s_lat, s_ctx], [vw, cv_ref[:, sl]])


def _na_bias_pattern(rb, n_rb):
    return jnp.where(rb == 0, 0, jnp.where(rb == n_rb - 1, 2, 1))


def _na_bias_tables(rpb_l, rows):
    n_rb = rows // NA_QROWS
    qc = np.arange(GRID_W)
    q_start = np.clip(qc - WIN_W // 2, 0, GRID_W - WIN_W)
    kc = np.arange(GRID_W)
    valid_c = (kc[None, :] >= q_start[:, None]) & (kc[None, :] < q_start[:, None] + WIN_W)
    pad = GRID_W - WIN_W
    rpb_pad = jnp.pad(rpb_l.astype(F32), ((0, 0), (0, 0), (pad, pad)))
    col_tab = jnp.stack([rpb_pad[:, :, GRID_W - 1 - c:2 * GRID_W - 1 - c] for c in range(GRID_W)], axis=2)
    col_tab = jnp.where(valid_c, col_tab, NEG_BIG)
    ext = jnp.pad(col_tab, ((0, 0), (NA_QROWS, NA_KROWS), (0, 0), (0, 0)), constant_values=NEG_BIG)
    pats, masks = [], []
    for rb in (0, 1, n_rb - 1):
        k_row0 = int(np.clip(NA_QROWS * rb - (NA_KROWS - WIN_H), 0, rows - NA_KROWS))
        c = k_row0 - NA_QROWS * rb + WIN_H - 1 + NA_QROWS
        pats.append(jnp.stack([ext[:, c - i:c - i + NA_KROWS] for i in range(NA_QROWS)], axis=1))
        r = NA_QROWS * rb + np.arange(NA_QROWS)
        w0 = np.clip(r - WIN_H // 2, 0, rows - WIN_H)
        kr = k_row0 + np.arange(NA_KROWS)
        masks.append((kr[None, :] >= w0[:, None]) & (kr[None, :] < w0[:, None] + WIN_H))
    in_window = np.stack(masks)[:, None, :, :, None, None]
    tab = jnp.where(in_window, jnp.stack(pats), NEG_BIG)
    return tab.transpose(0, 1, 2, 4, 3, 5).reshape(3, H_A, NA_TQ, NA_TK)


def _na_attn_call(z, ck, cv, bias, bsz, seq):
    rows = seq // GRID_W
    n_rb = rows // NA_QROWS
    m = z.shape[0]
    kv_spec = lambda blk: pl.BlockSpec((seq, D_A), lambda b, r: (b, blk))
    c_spec = pl.BlockSpec((None, ck.shape[1], D_A), lambda b, r: (b, 0, 0))
    return pl.pallas_call(
        functools.partial(_na_attn_kernel, rows=rows),
        grid=(bsz, n_rb),
        in_specs=[
            pl.BlockSpec((NA_TQ, D_A), lambda b, r: (b * n_rb + r, B_QA)),
            kv_spec(B_KA), kv_spec(B_VA), c_spec, c_spec,
            pl.BlockSpec((None, H_A, NA_TQ, NA_TK), lambda b, r: (_na_bias_pattern(r, n_rb), 0, 0, 0)),
        ],
        out_specs=pl.BlockSpec((NA_TQ, D_A), lambda b, r: (b * n_rb + r, 0)),
        out_shape=jax.ShapeDtypeStruct((m, D_A), F32),
        compiler_params=_cparams(("arbitrary", "arbitrary")),
        name="na_attn",
    )(z, z, z, ck, cv, bias)


def _log1p(x):
    return jnp.log1p(x)


def _hgrn_gates(z, lb):
    log_sig = jnp.minimum(z, 0.0) - _log1p(jnp.exp(-jnp.abs(z)))
    a = _log1p(-lb) + log_sig
    b = jnp.log(jnp.maximum(lb, LOG_FLOOR))
    logf = jnp.maximum(a, b) + _log1p(jnp.exp(-jnp.abs(a - b)))
    key = (1.0 - lb) * _sigmoid(-z)
    return logf, key


def _lower_bound(p_ref, layer):
    p = p_ref[...]
    e = jnp.exp(p - jnp.max(p, axis=0, keepdims=True))
    sm = e / jnp.sum(e, axis=0, keepdims=True)
    lb = jnp.zeros((1, DK_C), F32)
    for j in range(1, layer + 1):
        lb = lb + sm[j:j + 1]
    return lb


def _chunk_cumsum(x, seq, reverse):
    pos = lax.broadcasted_iota(jnp.int32, x.shape, 0) % CHUNK
    s = 1
    while s < CHUNK:
        if reverse:
            x = x + jnp.where(pos < CHUNK - s, pltpu.roll(x, seq - s, axis=0), 0.0)
        else:
            x = x + jnp.where(pos >= s, pltpu.roll(x, s, axis=0), 0.0)
        s *= 2
    return x


HALF = CHUNK // 2
LOG2_E = 1.4426950408889634
HGRN_UNROLL = 4


def _hgrn_chunk(q_ref, v_ref, b_s, k_s, o_s, r0, st, reverse):
    b = b_s[pl.ds(r0, CHUNK), :]
    q = q_ref[pl.ds(r0, CHUNK), :]
    k = k_s[pl.ds(r0, CHUNK), :]
    v = v_ref[pl.ds(r0, CHUNK), :]
    o = _dot_nt(q * jnp.exp2(b), st)
    t_iota = lax.broadcasted_iota(jnp.int32, (HALF, 1), 0)
    o_half = [o[:HALF], o[HALF:]]
    for s in range(CHUNK):
        b_row = b_s[pl.ds(r0 + s, 1), :]
        k_row = k_s[pl.ds(r0 + s, 1), :]
        v_row = v_ref[pl.ds(r0 + s, 1), :]
        for half in range(2):
            t0 = half * HALF
            if (t0 + HALF - 1 < s) if not reverse else (t0 > s):
                continue
            sl = slice(t0, t0 + HALF)
            decay = jnp.exp2(jnp.minimum(b[sl] - b_row, 0.0))
            col = jnp.sum(q[sl] * (k_row * decay), axis=-1, keepdims=True)
            if not ((t0 >= s) if not reverse else (t0 + HALF - 1 <= s)):
                keep = (t_iota + t0 >= s) if not reverse else (t_iota + t0 <= s)
                col = jnp.where(keep, col, 0.0)
            o_half[half] = o_half[half] + col * v_row
    o_s[pl.ds(r0, CHUNK), :] = jnp.concatenate(o_half, axis=0)
    edge = 0 if reverse else CHUNK - 1
    b_edge = b[edge:edge + 1, :]
    kh = k * jnp.exp2(b_edge - b)
    kv_t = lax.dot_general(v, kh, (((0,), (0,)), ((), ())), preferred_element_type=F32)
    return st * jnp.exp2(b_edge) + kv_t


def _hgrn_kernel(q_ref, zf_ref, zb_ref, i_ref, og_ref, lbf_ref, lbb_ref, g_ref, s0_ref,
                 o_ref, sfin_ref, of_s, ob_s, bf_s, bb_s, kf_s, kb_s, *, seq, layer):
    n_chunks = seq // CHUNK
    logf, key = _hgrn_gates(zf_ref[...], _lower_bound(lbf_ref, layer))
    bf_s[...] = _chunk_cumsum(logf * LOG2_E, seq, False)
    kf_s[...] = key
    logf, key = _hgrn_gates(zb_ref[...], _lower_bound(lbb_ref, layer))
    bb_s[...] = _chunk_cumsum(logf * LOG2_E, seq, True)
    kb_s[...] = key

    def body(it, carry):
        st_f, st_b = carry
        for u in range(HGRN_UNROLL):
            n = it * HGRN_UNROLL + u
            st_f = _hgrn_chunk(q_ref, i_ref, bf_s, kf_s, of_s, pl.multiple_of(n * CHUNK, CHUNK), st_f, False)
            st_b = _hgrn_chunk(q_ref, i_ref, bb_s, kb_s, ob_s,
                               pl.multiple_of((n_chunks - 1 - n) * CHUNK, CHUNK), st_b, True)
        return st_f, st_b

    st_f, st_b = lax.fori_loop(0, n_chunks // HGRN_UNROLL, body, (s0_ref[0].T, s0_ref[1].T))
    sfin_ref[0] = st_f.T
    sfin_ref[1] = st_b.T
    o_ref[...] = _rms(of_s[...] + ob_s[...]) * g_ref[...] * _silu(og_ref[...])


def _hgrn_call(z, lb_fwd, lb_bwd, norm_g, s0, bsz, seq, layer):
    m = z.shape[0]
    w = DK_C
    per = COL_BLK // w
    zspec = lambda blk: pl.BlockSpec((seq, w), lambda b, h: (b, blk * per + h))
    st_spec = pl.BlockSpec((None, 2, None, DK_C, DK_C), lambda b, h: (b, 0, h, 0, 0))
    return pl.pallas_call(
        functools.partial(_hgrn_kernel, seq=seq, layer=layer),
        grid=(bsz, H_C),
        in_specs=[
            zspec(B_QC), zspec(B_ZF), zspec(B_ZB), zspec(B_IC), zspec(B_OC),
            pl.BlockSpec((DEPTH, w), lambda b, h: (0, h)),
            pl.BlockSpec((DEPTH, w), lambda b, h: (0, h)),
            pl.BlockSpec((1, w), lambda b, h: (0, h)),
            st_spec,
        ],
        out_specs=[pl.BlockSpec((seq, w), lambda b, h: (b, h)), st_spec],
        out_shape=[jax.ShapeDtypeStruct((m, H_C * w), F32),
                   jax.ShapeDtypeStruct((bsz, 2, H_C, DK_C, DK_C), F32)],
        scratch_shapes=[pltpu.VMEM((seq, w), F32)] * 6,
        compiler_params=_cparams(("arbitrary", "arbitrary")),
        name="hgrn",
    )(z, z, z, z, z, lb_fwd, lb_bwd, norm_g, s0)


def _mix_kernel(oa_ref, oc_ref, u_ref, gb_ref, gc_ref, up_ref, gcp_ref, un_ref, gcn_ref,
                ga_ref, gbr_ref, gcr_ref, h_ref, wconv_ref, wa_ref, wb_ref, wc_ref, wout_ref,
                wrt_ref, g1_ref, n2g_ref, sc2_ref, sh2_ref,
                hout_ref, n2_ref, lgt_ref, *, seq):
    tm = u_ref.shape[0]
    zc = gc_ref[...] * u_ref[...]
    row = lax.broadcasted_iota(jnp.int32, zc.shape, 0)
    pos = (pl.program_id(0) * tm + row) % seq
    zp = jnp.where(row == 0, gcp_ref[7:8, :] * up_ref[7:8, :], pltpu.roll(zc, 1, axis=0))
    zn = jnp.where(row == tm - 1, gcn_ref[0:1, :] * un_ref[0:1, :], pltpu.roll(zc, tm - 1, axis=0))
    zp = jnp.where(pos == 0, 0.0, zp)
    zn = jnp.where(pos == seq - 1, 0.0, zn)
    w = wconv_ref[...]
    ob = gb_ref[...] * (w[0:1] * zp + w[1:2] * zc + w[2:3] * zn)
    ya = _dot(oa_ref[...].astype(BF16), wa_ref[...])
    yb = _dot(ob.astype(BF16), wb_ref[...])
    yc = _dot(oc_ref[...].astype(BF16), wc_ref[...])
    pre = _sigmoid(ga_ref[...]) * ya + _sigmoid(gbr_ref[...]) * yb + _sigmoid(gcr_ref[...]) * yc
    hn = h_ref[...] + g1_ref[0] * _dot(pre.astype(BF16), wout_ref[...])
    hout_ref[...] = hn
    n2 = _rms(hn) * n2g_ref[...] * (1.0 + sc2_ref[0]) + sh2_ref[0]
    _store_token_tiles(n2_ref, n2)
    lgt_ref[...] = _dot_nt(wrt_ref[...], n2, precision=lax.Precision.HIGHEST)


MIX_TILE = 512


def _mix_call(o_a, o_c, z, h, w_conv, wa, wb, wc, wout, wrt, n2g, ada, row0, toks_per_row, seq):
    m = h.shape[0]
    tm = MIX_TILE
    nt = m // tm
    tiles_per_row = toks_per_row // tm
    r8 = tm // 8
    zblk = lambda blk: pl.BlockSpec((tm, COL_BLK), lambda i: (i, blk))
    zprev = lambda blk: pl.BlockSpec((8, COL_BLK), lambda i: (jnp.maximum(i * r8 - 1, 0), blk))
    znext = lambda blk: pl.BlockSpec((8, COL_BLK), lambda i: (jnp.minimum((i + 1) * r8, m // 8 - 1), blk))
    zgate = lambda blk: pl.BlockSpec((tm, D_MODEL), lambda i: (i, blk // 2))
    full = lambda a: pl.BlockSpec(a.shape, lambda i: (0,) * a.ndim)
    tok = lambda wdt: pl.BlockSpec((tm, wdt), lambda i: (i, 0))
    return pl.pallas_call(
        functools.partial(_mix_kernel, seq=seq),
        grid=(nt,),
        in_specs=[
            tok(D_A), tok(D_A), zblk(B_U), zblk(B_GB), zblk(B_GC),
            zprev(B_U), zprev(B_GC), znext(B_U), znext(B_GC),
            zgate(B_GA), zgate(B_GBR), zgate(B_GCR), tok(D_MODEL),
            full(w_conv), full(wa), full(wb), full(wc), full(wout), full(wrt),
            _ada_spec(2, row0, tiles_per_row), full(n2g),
            _ada_spec(4, row0, tiles_per_row), _ada_spec(3, row0, tiles_per_row),
        ],
        out_specs=[tok(D_MODEL), pl.BlockSpec((tm * CHUNKS, LANES), lambda i: (i, 0)),
                   pl.BlockSpec((N_EXPERTS, tm), lambda i: (0, i))],
        out_shape=[jax.ShapeDtypeStruct((m, D_MODEL), F32),
                   jax.ShapeDtypeStruct((m * CHUNKS, LANES), F32),
                   jax.ShapeDtypeStruct((N_EXPERTS, m), F32)],
        compiler_params=_cparams(("arbitrary",)),
        name="mix",
    )(o_a, o_c, z, z, z, z, z, z, z, z, z, z, h, w_conv, wa, wb, wc, wout, wrt, ada, n2g, ada, ada)


def _first_max(x, axes_iota, size):
    m = x
    for ax in range(x.ndim - 1):
        m = jnp.max(m, axis=ax, keepdims=True)
    first = jnp.where(x == m, axes_iota, size)
    f = first
    for ax in range(x.ndim - 1):
        f = jnp.min(f, axis=ax, keepdims=True)
    return axes_iota == f, f


def _first_max_mask(x, axes_iota, size):
    return _first_max(x, axes_iota, size)[0]


def _routing_kernel(lgt_ref, bias_ref, wpick_ref, eidx_ref, rank_ref, cnt_ref, carry_s):
    tn = lgt_ref.shape[-1]

    @pl.when(pl.program_id(0) == 0)
    def _():
        carry_s[...] = jnp.zeros(carry_s.shape, F32)

    s = _sigmoid(lgt_ref[...])
    sel = (s + bias_ref[...]).reshape(N_GROUPS, GROUP_SIZE, tn)
    s = s.reshape(N_GROUPS, GROUP_SIZE, tn)
    neg_inf = -jnp.inf
    mem_iota = lax.broadcasted_iota(jnp.int32, sel.shape, 1)
    m1 = jnp.max(sel, axis=1, keepdims=True)
    first = jnp.min(jnp.where(sel == m1, mem_iota, GROUP_SIZE), axis=1, keepdims=True)
    m2 = jnp.max(jnp.where(mem_iota == first, neg_inf, sel), axis=1, keepdims=True)
    grp = (m1 + m2).reshape(N_GROUPS, tn)
    g_iota = lax.broadcasted_iota(jnp.int32, grp.shape, 0)
    gmask = jnp.zeros(grp.shape, jnp.bool_)
    for _ in range(TOPK_GROUPS):
        pick = _first_max_mask(grp, g_iota, N_GROUPS)
        gmask = gmask | pick
        grp = jnp.where(pick, neg_inf, grp)
    cand = jnp.where(gmask.reshape(N_GROUPS, 1, tn), sel, NEG_BIG)
    e_iota = lax.broadcasted_iota(jnp.int32, sel.shape, 0) * GROUP_SIZE + mem_iota
    chosen = jnp.zeros(sel.shape, jnp.bool_)
    picked = []
    for _ in range(TOP_K):
        pick, idx = _first_max(cand, e_iota, N_EXPERTS)
        chosen = chosen | pick
        cand = jnp.where(pick, neg_inf, cand)
        picked.append(idx)
    w = jnp.where(chosen, s, 0.0)
    tot = jnp.sum(jnp.sum(w, axis=1, keepdims=True), axis=0, keepdims=True)
    w = w / tot * ROUTED_SCALE
    chosen_f = jnp.where(chosen, 1.0, 0.0).reshape(N_EXPERTS, tn)
    earlier = (lax.broadcasted_iota(jnp.int32, (tn, tn), 0) < lax.broadcasted_iota(jnp.int32, (tn, tn), 1))
    rank = _dot(chosen_f.astype(BF16), jnp.where(earlier, 1.0, 0.0).astype(BF16)) + carry_s[:, 0:1]
    rank = rank.reshape(N_GROUPS, GROUP_SIZE, tn)
    of_pick = lambda v, idx: jnp.sum(jnp.sum(jnp.where(e_iota == idx, v, 0.0), axis=1, keepdims=True),
                                     axis=0, keepdims=True).reshape(1, tn)
    eidx_ref[...] = jnp.concatenate([idx.reshape(1, tn) for idx in picked], axis=0)
    rank_ref[...] = jnp.concatenate([of_pick(rank, idx) for idx in picked], axis=0).astype(jnp.int32)
    wpick_ref[...] = jnp.concatenate([of_pick(w, idx) for idx in picked], axis=0)
    carry_s[...] = carry_s[...] + jnp.sum(chosen_f, axis=1, keepdims=True)
    cnt_ref[...] = carry_s[...]


def _routing_call(lgt, b_router_l):
    m = lgt.shape[1]
    tn = 256
    pick_spec = pl.BlockSpec((TOP_K, tn), lambda i: (0, i))
    cnt_spec = pl.BlockSpec((N_EXPERTS, 128), lambda i: (0, 0))
    return pl.pallas_call(
        _routing_kernel,
        grid=(m // tn,),
        in_specs=[pl.BlockSpec((N_EXPERTS, tn), lambda i: (0, i)),
                  pl.BlockSpec((N_EXPERTS, tn), lambda i: (0, 0))],
        out_specs=[pick_spec, pick_spec, pick_spec, cnt_spec],
        out_shape=[jax.ShapeDtypeStruct((TOP_K, m), F32),
                   jax.ShapeDtypeStruct((TOP_K, m), jnp.int32),
                   jax.ShapeDtypeStruct((TOP_K, m), jnp.int32),
                   jax.ShapeDtypeStruct((N_EXPERTS, 128), F32)],
        scratch_shapes=[pltpu.VMEM((N_EXPERTS, 128), F32)],
        compiler_params=_cparams(("arbitrary",)),
        name="routing",
    )(lgt, jnp.broadcast_to(b_router_l[:, None], (N_EXPERTS, tn)))


MOE_TM = 256
ROW_GROUP = 16
assert TOP_K == CHUNKS and TOP_K & (TOP_K - 1) == 0


def _token_row(idx):
    return pl.multiple_of(idx & ~(TOP_K - 1), CHUNKS)


def _dispatch_plan(eidx, rank, cnt, n_tiles_max):
    counts = cnt[:, 0].astype(jnp.int32)
    tiles = (counts + MOE_TM - 1) // MOE_TM
    tile_end = jnp.cumsum(tiles)
    tile_start = tile_end - tiles
    n_tiles = tile_end[-1]
    experts = jnp.arange(N_EXPERTS, dtype=jnp.int32)
    base = jnp.sum(jnp.where(eidx[:, :, None] == experts, tile_start * MOE_TM, 0), axis=-1)
    pos = (rank + base).T.reshape(-1)
    ti = jnp.arange(n_tiles_max, dtype=jnp.int32)
    tile_e = jnp.sum((ti[:, None] >= tile_end[None, :]).astype(jnp.int32), axis=1)
    last_e = jnp.sum(jnp.where(ti == n_tiles - 1, tile_e, 0))
    tile_e = jnp.where(ti < n_tiles, tile_e, last_e).astype(jnp.int32)
    first = ((ti == 0) | (tile_e != jnp.roll(tile_e, 1))) & (ti < n_tiles)
    slot = (jnp.cumsum(first.astype(jnp.int32)) - 1) % 2
    next_tile = jnp.sum(jnp.where(tile_e[:, None] == experts, tile_end, 0), axis=1)
    next_e = jnp.sum(jnp.where(next_tile[:, None] == ti[None, :], tile_e[None, :], 0), axis=1)
    next_e = jnp.where(next_tile < n_tiles, next_e, -1)
    i32 = lambda v: v.astype(jnp.int32)
    return pos, (tile_e, i32(first), i32(next_e), i32(slot), i32(n_tiles).reshape(1))


def _moe_step(tile_g, tile_s, rowidx_s, wpick_ref, x_v, acc_v, g_w, g_r, y_w, y_r, wg_bf, wu_bf, wd_bf):
    row_g = tile_g * MOE_TM
    for r in range(MOE_TM):
        g_w[pl.ds(r * CHUNKS, CHUNKS), :] = x_v[pl.ds(_token_row(rowidx_s[row_g + r]), CHUNKS), :]

    x = _load_token_tiles(g_r, MOE_TM).astype(BF16)
    hid = _silu(_dot(x, wg_bf[...])) * _dot(x, wu_bf[...])
    y = _dot(hid.astype(BF16), wd_bf[...])
    for rb in range(MOE_TM // 8):
        for c in range(CHUNKS):
            y_w[pl.ds((rb * CHUNKS + c) * 8, 8), :] = y[rb * 8:(rb + 1) * 8, c * LANES:(c + 1) * LANES]

    row_s = tile_s * MOE_TM
    for j in range(MOE_TM // ROW_GROUP):
        slots, rows_v = [], []
        for u in range(ROW_GROUP):
            r = j * ROW_GROUP + u
            idx = rowidx_s[row_s + r]
            slots.append(_token_row(idx))
            rows_v.append(y_r[pl.ds((r // 8) * (8 * CHUNKS) + r % 8, CHUNKS, stride=8), :] * wpick_ref[idx])
        olds = [acc_v[pl.ds(s, CHUNKS), :] for s in slots]
        for s, o, v in zip(slots, olds, rows_v):
            acc_v[pl.ds(s, CHUNKS), :] = o + v


def _moe_kernel(te_ref, first_ref, next_ref, slot_ref, nt_ref, pos_ref, wpick_ref, x_hbm, fill_hbm,
                wg_hbm, wu_hbm, wd_hbm, out_hbm,
                rowidx_s, x_v, acc_v, g0_s, g1_s, y0_s, y1_s, wg_f, wu_f, wd_f, wg_bf, wu_bf, wd_bf, sem, wsem,
                *, layer):
    def weight_copies(e, slot):
        return [pltpu.make_async_copy(hbm.at[layer, e], buf.at[slot], wsem.at[slot, k])
                for k, (hbm, buf) in enumerate(((wg_hbm, wg_f), (wu_hbm, wu_f), (wd_hbm, wd_f)))]

    s = pl.program_id(0)
    n_tok = x_hbm.shape[0] // CHUNKS
    n_tiles_max = te_ref.shape[0]
    zero_blk = 512

    @pl.when(s == 0)
    def _prologue():
        x_copy = pltpu.make_async_copy(x_hbm, x_v.at[pl.ds(0, n_tok * CHUNKS)], sem.at[0])
        fill_copy = pltpu.make_async_copy(fill_hbm, rowidx_s, sem.at[1])
        x_copy.start()
        fill_copy.start()
        for cp in weight_copies(te_ref[0], 0):
            cp.start()

        def zero(j, carry):
            acc_v[pl.ds(pl.multiple_of(j * zero_blk, zero_blk), zero_blk), :] = jnp.zeros((zero_blk, LANES), F32)
            return carry

        lax.fori_loop(0, n_tok * CHUNKS // zero_blk, zero, 0)
        acc_v[pl.ds(n_tok * CHUNKS, CHUNKS), :] = jnp.zeros((CHUNKS, LANES), F32)
        x_v[pl.ds(n_tok * CHUNKS, CHUNKS), :] = jnp.zeros((CHUNKS, LANES), F32)
        g1_s[...] = jnp.zeros(g1_s.shape, F32)
        y0_s[...] = jnp.zeros(y0_s.shape, F32)
        fill_copy.wait()

        def invert(j, carry):
            for u in range(ROW_GROUP):
                idx = j * ROW_GROUP + u
                rowidx_s[pos_ref[idx]] = idx
            return carry

        lax.fori_loop(0, n_tok * TOP_K // ROW_GROUP, invert, 0)
        x_copy.wait()

    active = s < nt_ref[0] + 2
    ffn_tile = jnp.clip(s - 1, 0, n_tiles_max - 1)

    @pl.when(active & (first_ref[ffn_tile] == 1) & (s != 1))
    def _next_expert():
        slot = slot_ref[ffn_tile]
        for cp in weight_copies(te_ref[ffn_tile], slot):
            cp.wait()
        wg_bf[...] = wg_f[slot].astype(BF16)
        wu_bf[...] = wu_f[slot].astype(BF16)
        wd_bf[...] = wd_f[slot].astype(BF16)

        @pl.when(next_ref[ffn_tile] >= 0)
        def _prefetch():
            for cp in weight_copies(next_ref[ffn_tile], 1 - slot):
                cp.start()

    tile_g = jnp.minimum(s, n_tiles_max - 1)
    tile_s = jnp.maximum(s - 2, 0)
    common = (rowidx_s, wpick_ref, x_v, acc_v)
    weights = (wg_bf, wu_bf, wd_bf)

    @pl.when(active & (s % 2 == 0))
    def _even():
        _moe_step(tile_g, tile_s, *common, g0_s, g1_s, y1_s, y0_s, *weights)

    @pl.when(active & (s % 2 == 1))
    def _odd():
        _moe_step(tile_g, tile_s, *common, g1_s, g0_s, y0_s, y1_s, *weights)

    @pl.when(s == pl.num_programs(0) - 1)
    def _epilogue():
        out_copy = pltpu.make_async_copy(acc_v.at[pl.ds(0, n_tok * CHUNKS)], out_hbm, sem.at[2])
        out_copy.start()
        out_copy.wait()


def _moe_call(x, wpick, eidx, rank, cnt, w_gate, w_up, w_down, layer):
    m = x.shape[0] // CHUNKS
    n_rows = TOP_K * m
    n_tiles_max = n_rows // MOE_TM + N_EXPERTS
    n_slots = n_tiles_max * MOE_TM
    pos, tables = _dispatch_plan(eidx, rank, cnt, n_tiles_max)
    wflat = jnp.concatenate([wpick.T.reshape(-1), jnp.zeros((TOP_K,), F32)])
    anyspec = pl.BlockSpec(memory_space=pl.ANY)
    smem = pl.BlockSpec(memory_space=pltpu.SMEM)
    tile_buf = pltpu.VMEM((MOE_TM * CHUNKS, LANES), F32)
    grid_spec = pltpu.PrefetchScalarGridSpec(
        num_scalar_prefetch=len(tables),
        grid=(n_tiles_max + 2,),
        in_specs=[smem, smem, anyspec, anyspec, anyspec, anyspec, anyspec],
        out_specs=anyspec,
        scratch_shapes=[
            pltpu.SMEM((n_slots,), jnp.int32),
            pltpu.VMEM(((m + 1) * CHUNKS, LANES), F32),
            pltpu.VMEM(((m + 1) * CHUNKS, LANES), F32),
            tile_buf, tile_buf, tile_buf, tile_buf,
            pltpu.VMEM((2, D_MODEL, D_EXPERT), F32),
            pltpu.VMEM((2, D_MODEL, D_EXPERT), F32),
            pltpu.VMEM((2, D_EXPERT, D_MODEL), F32),
            pltpu.VMEM((D_MODEL, D_EXPERT), BF16),
            pltpu.VMEM((D_MODEL, D_EXPERT), BF16),
            pltpu.VMEM((D_EXPERT, D_MODEL), BF16),
            pltpu.SemaphoreType.DMA((3,)),
            pltpu.SemaphoreType.DMA((2, 3)),
        ],
    )
    return pl.pallas_call(
        functools.partial(_moe_kernel, layer=layer),
        grid_spec=grid_spec,
        out_shape=jax.ShapeDtypeStruct((m * CHUNKS, LANES), F32),
        compiler_params=_cparams(("arbitrary",)),
        name="moe",
    )(*tables, pos, wflat, x, jnp.full((n_slots,), n_rows, jnp.int32), w_gate, w_up, w_down)


def _moe_finish_kernel(x_ref, routed_ref, h_ref, wsg_ref, wsu_ref, wsd_ref, g2_ref, gf_ref, o_ref, *, last):
    rows = h_ref.shape[0]
    x = _load_token_tiles(x_ref, rows).astype(BF16)
    hid = _silu(_dot(x, wsg_ref[...])) * _dot(x, wsu_ref[...])
    shared = _dot(hid.astype(BF16), wsd_ref[...])
    h = h_ref[...] + g2_ref[0] * (_load_token_tiles(routed_ref, rows) + shared)
    o_ref[...] = _rms(h) * gf_ref[...] if last else h


def _moe_finish_call(x, routed, h, wsg, wsu, wsd, ada, row0, tiles_per_row, final_g, last):
    m = h.shape[0]
    tok = pl.BlockSpec((TOK_TILE, D_MODEL), lambda i: (i, 0))
    tiles = pl.BlockSpec((TOK_TILE * CHUNKS, LANES), lambda i: (i, 0))
    full = lambda a: pl.BlockSpec(a.shape, lambda i: (0,) * a.ndim)
    return pl.pallas_call(
        functools.partial(_moe_finish_kernel, last=last),
        grid=(m // TOK_TILE,),
        in_specs=[tiles, tiles, tok, full(wsg), full(wsu), full(wsd), _ada_spec(5, row0, tiles_per_row),
                  full(final_g)],
        out_specs=tok,
        out_shape=jax.ShapeDtypeStruct((m, D_MODEL), F32),
        compiler_params=_cparams(("arbitrary",)),
        name="moe_finish",
    )(x, routed, h, wsg, wsu, wsd, ada, final_g)


def _stream(h, p, ada, row0, bsz, seq, attend, states):
    toks_per_row = seq if row0 > 0 else bsz * seq
    tiles_per_row = toks_per_row // TOK_TILE
    keys, vals, sts = [], [], []
    for l in range(DEPTH):
        ada_l = ada[l].reshape(8, 1, N_MOD * D_MODEL)
        n = _norm_mod_call(h, p['norm1_g'][l:l + 1], ada_l, 1, 0, row0, tiles_per_row)
        z = _in_proj_call(n, p['w_in'], l)
        o_a = attend(z, l)
        o_c, st = _hgrn_call(z, p['lb_fwd'], p['lb_bwd'], p['hgrn_norm_g'][l:l + 1], states[l], bsz, seq, l)
        h, n2, lgt = _mix_call(o_a, o_c, z, h, p['w_conv'][l], p['w_br_a'][l], p['w_br_b'][l], p['w_br_c'][l],
                               p['w_out'][l], p['w_router_t'][l], p['norm2_g'][l:l + 1], ada_l,
                               row0, toks_per_row, seq)
        wpick, eidx, rank, cnt = _routing_call(lgt, p['b_router'][l])
        routed = _moe_call(n2, wpick, eidx, rank, cnt, p['w_gate'], p['w_up'], p['w_down'], l)
        h = _moe_finish_call(n2, routed, h, p['w_sh_gate'][l], p['w_sh_up'][l], p['w_sh_down'][l],
                             ada_l, row0, tiles_per_row, p['final_norm_g'], l == DEPTH - 1)
        keys.append(z[:, B_KA * COL_BLK:(B_KA + 1) * COL_BLK])
        vals.append(z[:, B_VA * COL_BLK:(B_VA + 1) * COL_BLK])
        sts.append(st)
    return h, keys, vals, sts


def kernel(x_prompt, x_sample, cache_k, cache_v, state_hgrn, c, c_ctx, norm1_g, norm2_g, w_ada, b_ada,
           w_in, rpb, w_conv, lb_fwd, lb_bwd, hgrn_norm_g, w_br_a, w_br_b, w_br_c, w_out, w_router,
           b_router, w_gate, w_up, w_down, w_sh_gate, w_sh_up, w_sh_down, final_norm_g):
    batch, seq, _ = x_prompt.shape
    dec_batch, dec_seq, _ = x_sample.shape
    past = cache_k.shape[2]
    p = dict(norm1_g=norm1_g, norm2_g=norm2_g, w_in=w_in, w_conv=w_conv, lb_fwd=lb_fwd, lb_bwd=lb_bwd,
             hgrn_norm_g=hgrn_norm_g,
             w_br_a=w_br_a.astype(BF16), w_br_b=w_br_b.astype(BF16), w_br_c=w_br_c.astype(BF16),
             w_out=w_out.astype(BF16), w_router_t=jnp.swapaxes(w_router, 1, 2), b_router=b_router,
             w_gate=w_gate, w_up=w_up, w_down=w_down,
             w_sh_gate=w_sh_gate.astype(BF16), w_sh_up=w_sh_up.astype(BF16), w_sh_down=w_sh_down.astype(BF16),
             final_norm_g=final_norm_g.reshape(1, D_MODEL))

    c8 = jnp.zeros((8, D_MODEL), F32).at[0].set(c_ctx).at[1:1 + dec_batch].set(c)
    ada = _ada_call(c8, w_ada, b_ada)

    zero_states = [jnp.zeros((batch, 2, H_C, DK_C, DK_C), F32)] * DEPTH
    ctx_attend = lambda z, l: _ctx_attn_call(z, seq)
    h_ctx, keys, vals, sts = _stream(x_prompt.reshape(batch * seq, D_MODEL), p, ada, 0, batch, seq,
                                     ctx_attend, zero_states)
    y_prompt = h_ctx.reshape(batch, seq, D_MODEL)
    new_cache_k = jnp.stack([k.reshape(batch, seq, H_A, DH_A) for k in keys], axis=1)
    new_cache_v = jnp.stack([v.reshape(batch, seq, H_A, DH_A) for v in vals], axis=1)
    new_state = jnp.stack(sts, axis=1)

    rows = dec_seq // GRID_W
    ck = cache_k.reshape(dec_batch, DEPTH, past, D_A)
    cv = cache_v.reshape(dec_batch, DEPTH, past, D_A)
    lat_states = [state_hgrn[:, l].astype(F32) for l in range(DEPTH)]

    def lat_attend(z, l):
        bias = _na_bias_tables(rpb[l], rows)
        return _na_attn_call(z, ck[:, l], cv[:, l], bias, dec_batch, dec_seq)

    h_lat, _, _, _ = _stream(x_sample.reshape(dec_batch * dec_seq, D_MODEL), p, ada, 1, dec_batch, dec_seq,
                             lat_attend, lat_states)
    y_sample = h_lat.reshape(dec_batch, dec_seq, D_MODEL)
    return (y_prompt, y_sample, new_cache_k, new_cache_v, new_state)
```

```python
import functools

import numpy as np
import jax
import jax.numpy as jnp
from jax import lax
from jax.experimental import pallas as pl
from jax.experimental.pallas import tpu as pltpu

F32 = jnp.float32
BF16 = jnp.bfloat16

D_MODEL = 1024
DEPTH = 2
GRID_W = 64
H_A = 8
DH_A = 64
D_A = H_A * DH_A
WIN_H = 8
WIN_W = 16
D_CONV = 512
H_C = 4
DK_C = 128
CHUNK = 16
N_EXPERTS = 64
N_GROUPS = 8
GROUP_SIZE = N_EXPERTS // N_GROUPS
TOPK_GROUPS = 4
TOP_K = 8
D_EXPERT = 256
ROUTED_SCALE = 2.5
N_MOD = 6
EPS = 1e-6
NEG_BIG = -1e30
LOG_FLOOR = 1e-30

D_IN = 8704
COL_BLK = 512
N_COL_BLKS = D_IN // COL_BLK
SRC_GATE_BLK = 11
N_GATE_BLKS = 6
B_GA, B_GBR, B_GCR = 0, 2, 4
B_QA, B_KA, B_VA, B_U, B_GB, B_GC, B_QC, B_ZF, B_ZB, B_IC, B_OC = range(6, 17)

NA_QROWS = 4
NA_KROWS = 12
NA_TQ = NA_QROWS * GRID_W
NA_TK = NA_KROWS * GRID_W

TOK_TILE = 256
VMEM_LIMIT = 56 * 1024 * 1024


def _cparams(sem):
    return pltpu.CompilerParams(dimension_semantics=sem, vmem_limit_bytes=VMEM_LIMIT)


def _sigmoid(x):
    return 1.0 / (1.0 + jnp.exp(-x))


def _silu(x):
    return x * _sigmoid(x)


def _dot(a, b):
    return jnp.dot(a, b, preferred_element_type=F32)


def _dot_nt(a, b, precision=None):
    return lax.dot_general(a, b, (((1,), (1,)), ((), ())), preferred_element_type=F32,
                           precision=precision)


LANES = 128
CHUNKS = D_MODEL // LANES


def _store_token_tiles(ref, val):
    for rb in range(val.shape[0] // 8):
        for c in range(CHUNKS):
            ref[pl.ds(rb * 8 * CHUNKS + c, 8, stride=CHUNKS), :] = \
                val[rb * 8:(rb + 1) * 8, c * LANES:(c + 1) * LANES]


def _load_token_tiles(ref, rows):
    return jnp.concatenate([ref[pl.ds(c, rows, stride=CHUNKS), :] for c in range(CHUNKS)], axis=1)


def _ada_kernel(c_ref, w_ref, b_ref, o_ref):
    a = _silu(c_ref[...])
    o_ref[0] = _dot(a, w_ref[0]) + b_ref[0]


def _ada_call(c8, w_ada, b_ada):
    tn = 1536
    n_out = N_MOD * D_MODEL
    return pl.pallas_call(
        _ada_kernel,
        grid=(DEPTH, n_out // tn),
        in_specs=[
            pl.BlockSpec((8, D_MODEL), lambda l, j: (0, 0)),
            pl.BlockSpec((1, D_MODEL, tn), lambda l, j: (l, 0, j)),
            pl.BlockSpec((1, 1, tn), lambda l, j: (l, 0, j)),
        ],
        out_specs=pl.BlockSpec((1, 8, tn), lambda l, j: (l, 0, j)),
        out_shape=jax.ShapeDtypeStruct((DEPTH, 8, n_out), F32),
        compiler_params=_cparams(("arbitrary", "arbitrary")),
        name="ada",
    )(c8, w_ada, b_ada.reshape(DEPTH, 1, n_out))


def _rms(x):
    return x * lax.rsqrt(jnp.mean(x * x, axis=-1, keepdims=True) + EPS)


def _norm_mod_kernel(h_ref, g_ref, sc_ref, sh_ref, o_ref):
    y = _rms(h_ref[...]) * g_ref[...]
    o_ref[...] = (y * (1.0 + sc_ref[0]) + sh_ref[0]).astype(o_ref.dtype)


def _ada_spec(mod_idx, row0, tiles_per_row):
    return pl.BlockSpec((1, 1, D_MODEL), lambda i: (row0 + i // tiles_per_row, 0, mod_idx))


def _norm_mod_call(h, g, ada, sc_idx, sh_idx, row0, tiles_per_row):
    m = h.shape[0]
    return pl.pallas_call(
        _norm_mod_kernel,
        grid=(m // TOK_TILE,),
        in_specs=[
            pl.BlockSpec((TOK_TILE, D_MODEL), lambda i: (i, 0)),
            pl.BlockSpec((1, D_MODEL), lambda i: (0, 0)),
            _ada_spec(sc_idx, row0, tiles_per_row),
            _ada_spec(sh_idx, row0, tiles_per_row),
        ],
        out_specs=pl.BlockSpec((TOK_TILE, D_MODEL), lambda i: (i, 0)),
        out_shape=jax.ShapeDtypeStruct((m, D_MODEL), BF16),
        compiler_params=_cparams(("arbitrary",)),
        name="norm_mod",
    )(h, g, ada, ada)


IN_PROJ_ROWS = 1024


def _in_proj_kernel(a_ref, w_ref, o_ref):
    w = w_ref[...].astype(BF16)
    for r in range(a_ref.shape[0] // IN_PROJ_ROWS):
        rows = slice(r * IN_PROJ_ROWS, (r + 1) * IN_PROJ_ROWS)
        o_ref[rows, :] = _dot(a_ref[rows, :], w)


def _src_col_blk(j):
    return jnp.where(j < N_GATE_BLKS, j + SRC_GATE_BLK, j - N_GATE_BLKS)


def _in_proj_call(n, w_in, layer):
    m = n.shape[0]
    return pl.pallas_call(
        _in_proj_kernel,
        grid=(N_COL_BLKS,),
        in_specs=[
            pl.BlockSpec((m, D_MODEL), lambda j: (0, 0)),
            pl.BlockSpec((None, D_MODEL, COL_BLK), lambda j: (layer, 0, _src_col_blk(j))),
        ],
        out_specs=pl.BlockSpec((m, COL_BLK), lambda j: (0, j)),
        out_shape=jax.ShapeDtypeStruct((m, D_IN), F32),
        compiler_params=_cparams(("arbitrary",)),
        name="in_proj",
    )(n, w_in)


def _softmax_pv(s_list, v_list):
    m = s_list[0].max(axis=-1, keepdims=True)
    for s in s_list[1:]:
        m = jnp.maximum(m, s.max(axis=-1, keepdims=True))
    num = None
    den = None
    for s, v in zip(s_list, v_list):
        p = jnp.exp(s - m)
        d = p.sum(axis=-1, keepdims=True)
        o = _dot(p.astype(BF16), v.astype(BF16))
        num = o if num is None else num + o
        den = d if den is None else den + d
    return num / den


assert DH_A == 4 ** (DH_A.bit_length() // 2)


def _ctx_attn_kernel(q_ref, k_ref, v_ref, o_ref):
    scale = DH_A ** -0.5
    for h in range(H_A):
        sl = slice(h * DH_A, (h + 1) * DH_A)
        s = _dot_nt((q_ref[:, sl] * scale).astype(BF16), k_ref[:, sl].astype(BF16))
        o_ref[:, sl] = _softmax_pv([s], [v_ref[:, sl]])


def _ctx_attn_call(z, seq):
    m = z.shape[0]
    spec = lambda blk: pl.BlockSpec((seq, D_A), lambda b: (b, blk))
    return pl.pallas_call(
        _ctx_attn_kernel,
        grid=(m // seq,),
        in_specs=[spec(B_QA), spec(B_KA), spec(B_VA)],
        out_specs=pl.BlockSpec((seq, D_A), lambda b: (b, 0)),
        out_shape=jax.ShapeDtypeStruct((m, D_A), F32),
        compiler_params=_cparams(("arbitrary",)),
        name="ctx_attn",
    )(z, z, z)


def _na_key_row0(rb, rows):
    return jnp.clip(NA_QROWS * rb - (NA_KROWS - WIN_H) , 0, rows - NA_KROWS)


def _na_attn_kernel(q_ref, k_ref, v_ref, ck_ref, cv_ref, bias_ref, o_ref, *, rows):
    scale = DH_A ** -0.5
    rb = pl.program_id(1)
    k0 = pl.multiple_of(_na_key_row0(rb, rows) * GRID_W, GRID_W)
    for h in range(H_A):
        sl = slice(h * DH_A, (h + 1) * DH_A)
        q = (q_ref[:, sl] * scale).astype(BF16)
        kw = k_ref[pl.ds(k0, NA_TK), sl]
        vw = v_ref[pl.ds(k0, NA_TK), sl]
        s_lat = _dot_nt(q, kw.astype(BF16)) + bias_ref[h]
        s_ctx = _dot_nt(q, ck_ref[:, sl].astype(BF16))
        o_ref[:, sl] = _softmax_pv([s_lat, s_ctx], [vw, cv_ref[:, sl]])


def _na_bias_pattern(rb, n_rb):
    return jnp.where(rb == 0, 0, jnp.where(rb == n_rb - 1, 2, 1))


def _na_bias_tables(rpb_l, rows):
    n_heads = rpb_l.shape[0]
    n_rb = rows // NA_QROWS
    qc = np.arange(GRID_W)
    q_start = np.clip(qc - WIN_W // 2, 0, GRID_W - WIN_W)
    kc = np.arange(GRID_W)
    valid_c = (kc[None, :] >= q_start[:, None]) & (kc[None, :] < q_start[:, None] + WIN_W)
    pad = GRID_W - WIN_W
    rpb_pad = jnp.pad(rpb_l.astype(F32), ((0, 0), (0, 0), (pad, pad)))
    col_tab = jnp.stack([rpb_pad[:, :, GRID_W - 1 - c:2 * GRID_W - 1 - c] for c in range(GRID_W)], axis=2)
    col_tab = jnp.where(valid_c, col_tab, NEG_BIG)
    masked = jnp.full((n_heads, GRID_W, GRID_W), NEG_BIG, F32)
    tabs = []
    for rb in (0, 1, n_rb - 1):
        k_row0 = int(np.clip(NA_QROWS * rb - (NA_KROWS - WIN_H), 0, rows - NA_KROWS))
        q_rows = []
        for i in range(NA_QROWS):
            r = NA_QROWS * rb + i
            w0 = int(np.clip(r - WIN_H // 2, 0, rows - WIN_H))
            blocks = []
            for j in range(NA_KROWS):
                kr = k_row0 + j
                in_window = w0 <= kr < w0 + WIN_H
                blocks.append(col_tab[:, kr - r + WIN_H - 1] if in_window else masked)
            q_rows.append(jnp.concatenate(blocks, axis=2))
        tabs.append(jnp.concatenate(q_rows, axis=1))
    return jnp.stack(tabs)


def _na_attn_call(z, ck, cv, bias, bsz, seq, layer):
    rows = seq // GRID_W
    n_rb = rows // NA_QROWS
    m = z.shape[0]
    kv_spec = lambda blk: pl.BlockSpec((seq, D_A), lambda b, r: (b, blk))
    c_spec = pl.BlockSpec((None, ck.shape[1], D_A), lambda b, r: (b, 0, 0))
    return pl.pallas_call(
        functools.partial(_na_attn_kernel, rows=rows),
        grid=(bsz, n_rb),
        in_specs=[
            pl.BlockSpec((NA_TQ, D_A), lambda b, r: (b * n_rb + r, B_QA)),
            kv_spec(B_KA), kv_spec(B_VA), c_spec, c_spec,
            pl.BlockSpec((None, H_A, NA_TQ, NA_TK), lambda b, r: (_na_bias_pattern(r, n_rb), layer, 0, 0)),
        ],
        out_specs=pl.BlockSpec((NA_TQ, D_A), lambda b, r: (b * n_rb + r, 0)),
        out_shape=jax.ShapeDtypeStruct((m, D_A), F32),
        compiler_params=_cparams(("arbitrary", "arbitrary")),
        name="na_attn",
    )(z, z, z, ck, cv, bias)


def _log1p(x):
    return jnp.log1p(x)


def _hgrn_gates(z, lb):
    log_sig = jnp.minimum(z, 0.0) - _log1p(jnp.exp(-jnp.abs(z)))
    a = _log1p(-lb) + log_sig
    b = jnp.log(jnp.maximum(lb, LOG_FLOOR))
    logf = jnp.maximum(a, b) + _log1p(jnp.exp(-jnp.abs(a - b)))
    key = (1.0 - lb) * _sigmoid(-z)
    return logf, key


def _lower_bound(p_ref, layer):
    p = p_ref[...]
    e = jnp.exp(p - jnp.max(p, axis=0, keepdims=True))
    sm = e / jnp.sum(e, axis=0, keepdims=True)
    lb = jnp.zeros((1, DK_C), F32)
    for j in range(1, layer + 1):
        lb = lb + sm[j:j + 1]
    return lb


def _chunk_cumsum(x, seq, reverse):
    pos = lax.broadcasted_iota(jnp.int32, x.shape, 0) % CHUNK
    s = 1
    while s < CHUNK:
        if reverse:
            x = x + jnp.where(pos < CHUNK - s, pltpu.roll(x, seq - s, axis=0), 0.0)
        else:
            x = x + jnp.where(pos >= s, pltpu.roll(x, s, axis=0), 0.0)
        s *= 2
    return x


HALF = CHUNK // 2
LOG2_E = 1.4426950408889634
HGRN_UNROLL = 4


def _hgrn_chunk(q_ref, v_ref, b_s, k_s, o_s, r0, st, reverse):
    b = b_s[pl.ds(r0, CHUNK), :]
    q = q_ref[pl.ds(r0, CHUNK), :]
    k = k_s[pl.ds(r0, CHUNK), :]
    v = v_ref[pl.ds(r0, CHUNK), :]
    o = _dot_nt(q * jnp.exp2(b), st)
    t_iota = lax.broadcasted_iota(jnp.int32, (HALF, 1), 0)
    o_half = [o[:HALF], o[HALF:]]
    for s in range(CHUNK):
        b_row = b_s[pl.ds(r0 + s, 1), :]
        k_row = k_s[pl.ds(r0 + s, 1), :]
        v_row = v_ref[pl.ds(r0 + s, 1), :]
        for half in range(2):
            t0 = half * HALF
            if (t0 + HALF - 1 < s) if not reverse else (t0 > s):
                continue
            sl = slice(t0, t0 + HALF)
            decay = jnp.exp2(jnp.minimum(b[sl] - b_row, 0.0))
            col = jnp.sum(q[sl] * (k_row * decay), axis=-1, keepdims=True)
            if not ((t0 >= s) if not reverse else (t0 + HALF - 1 <= s)):
                keep = (t_iota + t0 >= s) if not reverse else (t_iota + t0 <= s)
                col = jnp.where(keep, col, 0.0)
            o_half[half] = o_half[half] + col * v_row
    o_s[pl.ds(r0, CHUNK), :] = jnp.concatenate(o_half, axis=0)
    edge = 0 if reverse else CHUNK - 1
    b_edge = b[edge:edge + 1, :]
    kh = k * jnp.exp2(b_edge - b)
    kv_t = lax.dot_general(v, kh, (((0,), (0,)), ((), ())), preferred_element_type=F32)
    return st * jnp.exp2(b_edge) + kv_t


def _hgrn_kernel(q_ref, zf_ref, zb_ref, i_ref, og_ref, lbf_ref, lbb_ref, g_ref, s0_ref,
                 o_ref, sfin_ref, of_s, ob_s, bf_s, bb_s, kf_s, kb_s, *, seq, layer):
    n_chunks = seq // CHUNK
    logf, key = _hgrn_gates(zf_ref[...], _lower_bound(lbf_ref, layer))
    bf_s[...] = _chunk_cumsum(logf * LOG2_E, seq, False)
    kf_s[...] = key
    logf, key = _hgrn_gates(zb_ref[...], _lower_bound(lbb_ref, layer))
    bb_s[...] = _chunk_cumsum(logf * LOG2_E, seq, True)
    kb_s[...] = key

    def body(it, carry):
        st_f, st_b = carry
        for u in range(HGRN_UNROLL):
            n = it * HGRN_UNROLL + u
            st_f = _hgrn_chunk(q_ref, i_ref, bf_s, kf_s, of_s, pl.multiple_of(n * CHUNK, CHUNK), st_f, False)
            st_b = _hgrn_chunk(q_ref, i_ref, bb_s, kb_s, ob_s,
                               pl.multiple_of((n_chunks - 1 - n) * CHUNK, CHUNK), st_b, True)
        return st_f, st_b

    st_f, st_b = lax.fori_loop(0, n_chunks // HGRN_UNROLL, body, (s0_ref[0].T, s0_ref[1].T))
    sfin_ref[0] = st_f.T
    sfin_ref[1] = st_b.T
    o_ref[...] = _rms(of_s[...] + ob_s[...]) * g_ref[...] * _silu(og_ref[...])


def _hgrn_call(z, lb_fwd, lb_bwd, norm_g, s0, bsz, seq, layer):
    m = z.shape[0]
    w = DK_C
    per = COL_BLK // w
    zspec = lambda blk: pl.BlockSpec((seq, w), lambda b, h: (b, blk * per + h))
    st_spec = pl.BlockSpec((None, 2, None, DK_C, DK_C), lambda b, h: (b, 0, h, 0, 0))
    return pl.pallas_call(
        functools.partial(_hgrn_kernel, seq=seq, layer=layer),
        grid=(bsz, H_C),
        in_specs=[
            zspec(B_QC), zspec(B_ZF), zspec(B_ZB), zspec(B_IC), zspec(B_OC),
            pl.BlockSpec((DEPTH, w), lambda b, h: (0, h)),
            pl.BlockSpec((DEPTH, w), lambda b, h: (0, h)),
            pl.BlockSpec((1, w), lambda b, h: (0, h)),
            st_spec,
        ],
        out_specs=[pl.BlockSpec((seq, w), lambda b, h: (b, h)), st_spec],
        out_shape=[jax.ShapeDtypeStruct((m, H_C * w), F32),
                   jax.ShapeDtypeStruct((bsz, 2, H_C, DK_C, DK_C), F32)],
        scratch_shapes=[pltpu.VMEM((seq, w), F32)] * 6,
        compiler_params=_cparams(("arbitrary", "arbitrary")),
        name="hgrn",
    )(z, z, z, z, z, lb_fwd, lb_bwd, norm_g, s0)


def _mix_kernel(oa_ref, oc_ref, u_ref, gb_ref, gc_ref, up_ref, gcp_ref, un_ref, gcn_ref,
                ga_ref, gbr_ref, gcr_ref, h_ref, wconv_ref, wa_ref, wb_ref, wc_ref, wout_ref,
                wrt_ref, g1_ref, n2g_ref, sc2_ref, sh2_ref,
                hout_ref, n2_ref, lgt_ref, *, seq):
    tm = u_ref.shape[0]
    zc = gc_ref[...] * u_ref[...]
    row = lax.broadcasted_iota(jnp.int32, zc.shape, 0)
    pos = (pl.program_id(0) * tm + row) % seq
    zp = jnp.where(row == 0, gcp_ref[7:8, :] * up_ref[7:8, :], pltpu.roll(zc, 1, axis=0))
    zn = jnp.where(row == tm - 1, gcn_ref[0:1, :] * un_ref[0:1, :], pltpu.roll(zc, tm - 1, axis=0))
    zp = jnp.where(pos == 0, 0.0, zp)
    zn = jnp.where(pos == seq - 1, 0.0, zn)
    w = wconv_ref[...]
    ob = gb_ref[...] * (w[0:1] * zp + w[1:2] * zc + w[2:3] * zn)
    ya = _dot(oa_ref[...].astype(BF16), wa_ref[...])
    yb = _dot(ob.astype(BF16), wb_ref[...])
    yc = _dot(oc_ref[...].astype(BF16), wc_ref[...])
    pre = _sigmoid(ga_ref[...]) * ya + _sigmoid(gbr_ref[...]) * yb + _sigmoid(gcr_ref[...]) * yc
    hn = h_ref[...] + g1_ref[0] * _dot(pre.astype(BF16), wout_ref[...])
    hout_ref[...] = hn
    n2 = _rms(hn) * n2g_ref[...] * (1.0 + sc2_ref[0]) + sh2_ref[0]
    _store_token_tiles(n2_ref, n2)
    lgt_ref[...] = _dot_nt(wrt_ref[...], n2, precision=lax.Precision.HIGHEST)


MIX_TILE = 512


def _mix_call(o_a, o_c, z, h, w_conv, wa, wb, wc, wout, wrt, n2g, ada, row0, toks_per_row, seq):
    m = h.shape[0]
    tm = MIX_TILE
    nt = m // tm
    tiles_per_row = toks_per_row // tm
    r8 = tm // 8
    zblk = lambda blk: pl.BlockSpec((tm, COL_BLK), lambda i: (i, blk))
    zprev = lambda blk: pl.BlockSpec((8, COL_BLK), lambda i: (jnp.maximum(i * r8 - 1, 0), blk))
    znext = lambda blk: pl.BlockSpec((8, COL_BLK), lambda i: (jnp.minimum((i + 1) * r8, m // 8 - 1), blk))
    zgate = lambda blk: pl.BlockSpec((tm, D_MODEL), lambda i: (i, blk // 2))
    full = lambda a: pl.BlockSpec(a.shape, lambda i: (0,) * a.ndim)
    tok = lambda wdt: pl.BlockSpec((tm, wdt), lambda i: (i, 0))
    return pl.pallas_call(
        functools.partial(_mix_kernel, seq=seq),
        grid=(nt,),
        in_specs=[
            tok(D_A), tok(D_A), zblk(B_U), zblk(B_GB), zblk(B_GC),
            zprev(B_U), zprev(B_GC), znext(B_U), znext(B_GC),
            zgate(B_GA), zgate(B_GBR), zgate(B_GCR), tok(D_MODEL),
            full(w_conv), full(wa), full(wb), full(wc), full(wout), full(wrt),
            _ada_spec(2, row0, tiles_per_row), full(n2g),
            _ada_spec(4, row0, tiles_per_row), _ada_spec(3, row0, tiles_per_row),
        ],
        out_specs=[tok(D_MODEL), pl.BlockSpec((tm * CHUNKS, LANES), lambda i: (i, 0)),
                   pl.BlockSpec((N_EXPERTS, tm), lambda i: (0, i))],
        out_shape=[jax.ShapeDtypeStruct((m, D_MODEL), F32),
                   jax.ShapeDtypeStruct((m * CHUNKS, LANES), F32),
                   jax.ShapeDtypeStruct((N_EXPERTS, m), F32)],
        compiler_params=_cparams(("arbitrary",)),
        name="mix",
    )(o_a, o_c, z, z, z, z, z, z, z, z, z, z, h, w_conv, wa, wb, wc, wout, wrt, ada, n2g, ada, ada)


def _first_max(x, axes_iota, size):
    m = x
    for ax in range(x.ndim - 1):
        m = jnp.max(m, axis=ax, keepdims=True)
    first = jnp.where(x == m, axes_iota, size)
    f = first
    for ax in range(x.ndim - 1):
        f = jnp.min(f, axis=ax, keepdims=True)
    return axes_iota == f, f


def _first_max_mask(x, axes_iota, size):
    return _first_max(x, axes_iota, size)[0]


def _routing_kernel(lgt_ref, bias_ref, wpick_ref, eidx_ref, rank_ref, cnt_ref, carry_s):
    tn = lgt_ref.shape[-1]

    @pl.when(pl.program_id(0) == 0)
    def _():
        carry_s[...] = jnp.zeros(carry_s.shape, F32)

    s = _sigmoid(lgt_ref[...])
    sel = (s + bias_ref[...]).reshape(N_GROUPS, GROUP_SIZE, tn)
    s = s.reshape(N_GROUPS, GROUP_SIZE, tn)
    neg_inf = -jnp.inf
    mem_iota = lax.broadcasted_iota(jnp.int32, sel.shape, 1)
    m1 = jnp.max(sel, axis=1, keepdims=True)
    first = jnp.min(jnp.where(sel == m1, mem_iota, GROUP_SIZE), axis=1, keepdims=True)
    m2 = jnp.max(jnp.where(mem_iota == first, neg_inf, sel), axis=1, keepdims=True)
    grp = (m1 + m2).reshape(N_GROUPS, tn)
    g_iota = lax.broadcasted_iota(jnp.int32, grp.shape, 0)
    gmask = jnp.zeros(grp.shape, jnp.bool_)
    for _ in range(TOPK_GROUPS):
        pick = _first_max_mask(grp, g_iota, N_GROUPS)
        gmask = gmask | pick
        grp = jnp.where(pick, neg_inf, grp)
    cand = jnp.where(gmask.reshape(N_GROUPS, 1, tn), sel, NEG_BIG)
    e_iota = lax.broadcasted_iota(jnp.int32, sel.shape, 0) * GROUP_SIZE + mem_iota
    chosen = jnp.zeros(sel.shape, jnp.bool_)
    picked = []
    for _ in range(TOP_K):
        pick, idx = _first_max(cand, e_iota, N_EXPERTS)
        chosen = chosen | pick
        cand = jnp.where(pick, neg_inf, cand)
        picked.append(idx)
    w = jnp.where(chosen, s, 0.0)
    tot = jnp.sum(jnp.sum(w, axis=1, keepdims=True), axis=0, keepdims=True)
    w = w / tot * ROUTED_SCALE
    chosen_f = jnp.where(chosen, 1.0, 0.0).reshape(N_EXPERTS, tn)
    earlier = (lax.broadcasted_iota(jnp.int32, (tn, tn), 0) < lax.broadcasted_iota(jnp.int32, (tn, tn), 1))
    rank = _dot(chosen_f.astype(BF16), jnp.where(earlier, 1.0, 0.0).astype(BF16)) + carry_s[:, 0:1]
    rank = rank.reshape(N_GROUPS, GROUP_SIZE, tn)
    of_pick = lambda v, idx: jnp.sum(jnp.sum(jnp.where(e_iota == idx, v, 0.0), axis=1, keepdims=True),
                                     axis=0, keepdims=True).reshape(1, tn)
    eidx_ref[...] = jnp.concatenate([idx.reshape(1, tn) for idx in picked], axis=0)
    rank_ref[...] = jnp.concatenate([of_pick(rank, idx) for idx in picked], axis=0).astype(jnp.int32)
    wpick_ref[...] = jnp.concatenate([of_pick(w, idx) for idx in picked], axis=0)
    carry_s[...] = carry_s[...] + jnp.sum(chosen_f, axis=1, keepdims=True)
    cnt_ref[...] = carry_s[...]


def _routing_call(lgt, b_router_l):
    m = lgt.shape[1]
    tn = 256
    pick_spec = pl.BlockSpec((TOP_K, tn), lambda i: (0, i))
    cnt_spec = pl.BlockSpec((N_EXPERTS, 128), lambda i: (0, 0))
    return pl.pallas_call(
        _routing_kernel,
        grid=(m // tn,),
        in_specs=[pl.BlockSpec((N_EXPERTS, tn), lambda i: (0, i)),
                  pl.BlockSpec((N_EXPERTS, tn), lambda i: (0, 0))],
        out_specs=[pick_spec, pick_spec, pick_spec, cnt_spec],
        out_shape=[jax.ShapeDtypeStruct((TOP_K, m), F32),
                   jax.ShapeDtypeStruct((TOP_K, m), jnp.int32),
                   jax.ShapeDtypeStruct((TOP_K, m), jnp.int32),
                   jax.ShapeDtypeStruct((N_EXPERTS, 128), F32)],
        scratch_shapes=[pltpu.VMEM((N_EXPERTS, 128), F32)],
        compiler_params=_cparams(("arbitrary",)),
        name="routing",
    )(lgt, jnp.broadcast_to(b_router_l[:, None], (N_EXPERTS, tn)))


MOE_TM = 256
ROW_GROUP = 16
assert TOP_K == CHUNKS and TOP_K & (TOP_K - 1) == 0


def _token_row(idx):
    return pl.multiple_of(idx & ~(TOP_K - 1), CHUNKS)


def _dispatch_plan(eidx, rank, cnt, n_tiles_max):
    counts = cnt[:, 0].astype(jnp.int32)
    tiles = (counts + MOE_TM - 1) // MOE_TM
    tile_end = jnp.cumsum(tiles)
    tile_start = tile_end - tiles
    n_tiles = tile_end[-1]
    experts = jnp.arange(N_EXPERTS, dtype=jnp.int32)
    base = jnp.sum(jnp.where(eidx[:, :, None] == experts, tile_start * MOE_TM, 0), axis=-1)
    pos = (rank + base).T.reshape(-1)
    ti = jnp.arange(n_tiles_max, dtype=jnp.int32)
    tile_e = jnp.sum((ti[:, None] >= tile_end[None, :]).astype(jnp.int32), axis=1)
    last_e = jnp.sum(jnp.where(ti == n_tiles - 1, tile_e, 0))
    tile_e = jnp.where(ti < n_tiles, tile_e, last_e).astype(jnp.int32)
    first = ((ti == 0) | (tile_e != jnp.roll(tile_e, 1))) & (ti < n_tiles)
    slot = (jnp.cumsum(first.astype(jnp.int32)) - 1) % 2
    next_tile = jnp.sum(jnp.where(tile_e[:, None] == experts, tile_end, 0), axis=1)
    next_e = jnp.sum(jnp.where(next_tile[:, None] == ti[None, :], tile_e[None, :], 0), axis=1)
    next_e = jnp.where(next_tile < n_tiles, next_e, -1)
    i32 = lambda v: v.astype(jnp.int32)
    return pos, (tile_e, i32(first), i32(next_e), i32(slot), i32(n_tiles).reshape(1))


def _moe_step(tile_g, tile_s, rowidx_s, wpick_ref, x_v, acc_v, g_w, g_r, y_w, y_r, wg_bf, wu_bf, wd_bf):
    row_g = tile_g * MOE_TM
    for r in range(MOE_TM):
        g_w[pl.ds(r * CHUNKS, CHUNKS), :] = x_v[pl.ds(_token_row(rowidx_s[row_g + r]), CHUNKS), :]

    x = _load_token_tiles(g_r, MOE_TM).astype(BF16)
    hid = _silu(_dot(x, wg_bf[...])) * _dot(x, wu_bf[...])
    y = _dot(hid.astype(BF16), wd_bf[...])
    for rb in range(MOE_TM // 8):
        for c in range(CHUNKS):
            y_w[pl.ds((rb * CHUNKS + c) * 8, 8), :] = y[rb * 8:(rb + 1) * 8, c * LANES:(c + 1) * LANES]

    row_s = tile_s * MOE_TM
    for j in range(MOE_TM // ROW_GROUP):
        slots, rows_v = [], []
        for u in range(ROW_GROUP):
            r = j * ROW_GROUP + u
            idx = rowidx_s[row_s + r]
            slots.append(_token_row(idx))
            rows_v.append(y_r[pl.ds((r // 8) * (8 * CHUNKS) + r % 8, CHUNKS, stride=8), :] * wpick_ref[idx])
        olds = [acc_v[pl.ds(s, CHUNKS), :] for s in slots]
        for s, o, v in zip(slots, olds, rows_v):
            acc_v[pl.ds(s, CHUNKS), :] = o + v


def _moe_kernel(te_ref, first_ref, next_ref, slot_ref, nt_ref, pos_ref, wpick_ref, x_hbm, fill_hbm,
                wg_hbm, wu_hbm, wd_hbm, out_hbm,
                rowidx_s, x_v, acc_v, g0_s, g1_s, y0_s, y1_s, wg_f, wu_f, wd_f, wg_bf, wu_bf, wd_bf, sem, wsem,
                *, layer):
    def weight_copies(e, slot):
        return [pltpu.make_async_copy(hbm.at[layer, e], buf.at[slot], wsem.at[slot, k])
                for k, (hbm, buf) in enumerate(((wg_hbm, wg_f), (wu_hbm, wu_f), (wd_hbm, wd_f)))]

    s = pl.program_id(0)
    n_tok = x_hbm.shape[0] // CHUNKS
    n_tiles_max = te_ref.shape[0]
    zero_blk = 512

    @pl.when(s == 0)
    def _prologue():
        x_copy = pltpu.make_async_copy(x_hbm, x_v.at[pl.ds(0, n_tok * CHUNKS)], sem.at[0])
        fill_copy = pltpu.make_async_copy(fill_hbm, rowidx_s, sem.at[1])
        x_copy.start()
        fill_copy.start()
        for cp in weight_copies(te_ref[0], 0):
            cp.start()

        def zero(j, carry):
            acc_v[pl.ds(pl.multiple_of(j * zero_blk, zero_blk), zero_blk), :] = jnp.zeros((zero_blk, LANES), F32)
            return carry

        lax.fori_loop(0, n_tok * CHUNKS // zero_blk, zero, 0)
        acc_v[pl.ds(n_tok * CHUNKS, CHUNKS), :] = jnp.zeros((CHUNKS, LANES), F32)
        x_v[pl.ds(n_tok * CHUNKS, CHUNKS), :] = jnp.zeros((CHUNKS, LANES), F32)
        g1_s[...] = jnp.zeros(g1_s.shape, F32)
        y0_s[...] = jnp.zeros(y0_s.shape, F32)
        fill_copy.wait()

        def invert(j, carry):
            for u in range(ROW_GROUP):
                idx = j * ROW_GROUP + u
                rowidx_s[pos_ref[idx]] = idx
            return carry

        lax.fori_loop(0, n_tok * TOP_K // ROW_GROUP, invert, 0)
        x_copy.wait()

    active = s < nt_ref[0] + 2
    ffn_tile = jnp.clip(s - 1, 0, n_tiles_max - 1)

    @pl.when(active & (first_ref[ffn_tile] == 1) & (s != 1))
    def _next_expert():
        slot = slot_ref[ffn_tile]
        for cp in weight_copies(te_ref[ffn_tile], slot):
            cp.wait()
        wg_bf[...] = wg_f[slot].astype(BF16)
        wu_bf[...] = wu_f[slot].astype(BF16)
        wd_bf[...] = wd_f[slot].astype(BF16)

        @pl.when(next_ref[ffn_tile] >= 0)
        def _prefetch():
            for cp in weight_copies(next_ref[ffn_tile], 1 - slot):
                cp.start()

    tile_g = jnp.minimum(s, n_tiles_max - 1)
    tile_s = jnp.maximum(s - 2, 0)
    common = (rowidx_s, wpick_ref, x_v, acc_v)
    weights = (wg_bf, wu_bf, wd_bf)

    @pl.when(active & (s % 2 == 0))
    def _even():
        _moe_step(tile_g, tile_s, *common, g0_s, g1_s, y1_s, y0_s, *weights)

    @pl.when(active & (s % 2 == 1))
    def _odd():
        _moe_step(tile_g, tile_s, *common, g1_s, g0_s, y0_s, y1_s, *weights)

    @pl.when(s == pl.num_programs(0) - 1)
    def _epilogue():
        out_copy = pltpu.make_async_copy(acc_v.at[pl.ds(0, n_tok * CHUNKS)], out_hbm, sem.at[2])
        out_copy.start()
        out_copy.wait()


def _moe_call(x, wpick, eidx, rank, cnt, w_gate, w_up, w_down, layer):
    m = x.shape[0] // CHUNKS
    n_rows = TOP_K * m
    n_tiles_max = n_rows // MOE_TM + N_EXPERTS
    n_slots = n_tiles_max * MOE_TM
    pos, tables = _dispatch_plan(eidx, rank, cnt, n_tiles_max)
    wflat = jnp.concatenate([wpick.T.reshape(-1), jnp.zeros((TOP_K,), F32)])
    anyspec = pl.BlockSpec(memory_space=pl.ANY)
    smem = pl.BlockSpec(memory_space=pltpu.SMEM)
    tile_buf = pltpu.VMEM((MOE_TM * CHUNKS, LANES), F32)
    grid_spec = pltpu.PrefetchScalarGridSpec(
        num_scalar_prefetch=len(tables),
        grid=(n_tiles_max + 2,),
        in_specs=[smem, smem, anyspec, anyspec, anyspec, anyspec, anyspec],
        out_specs=anyspec,
        scratch_shapes=[
            pltpu.SMEM((n_slots,), jnp.int32),
            pltpu.VMEM(((m + 1) * CHUNKS, LANES), F32),
            pltpu.VMEM(((m + 1) * CHUNKS, LANES), F32),
            tile_buf, tile_buf, tile_buf, tile_buf,
            pltpu.VMEM((2, D_MODEL, D_EXPERT), F32),
            pltpu.VMEM((2, D_MODEL, D_EXPERT), F32),
            pltpu.VMEM((2, D_EXPERT, D_MODEL), F32),
            pltpu.VMEM((D_MODEL, D_EXPERT), BF16),
            pltpu.VMEM((D_MODEL, D_EXPERT), BF16),
            pltpu.VMEM((D_EXPERT, D_MODEL), BF16),
            pltpu.SemaphoreType.DMA((3,)),
            pltpu.SemaphoreType.DMA((2, 3)),
        ],
    )
    return pl.pallas_call(
        functools.partial(_moe_kernel, layer=layer),
        grid_spec=grid_spec,
        out_shape=jax.ShapeDtypeStruct((m * CHUNKS, LANES), F32),
        compiler_params=_cparams(("arbitrary",)),
        name="moe",
    )(*tables, pos, wflat, x, jnp.full((n_slots,), n_rows, jnp.int32), w_gate, w_up, w_down)


def _moe_finish_kernel(x_ref, routed_ref, h_ref, wsg_ref, wsu_ref, wsd_ref, g2_ref, gf_ref, o_ref, *, last):
    rows = h_ref.shape[0]
    x = _load_token_tiles(x_ref, rows).astype(BF16)
    hid = _silu(_dot(x, wsg_ref[...])) * _dot(x, wsu_ref[...])
    shared = _dot(hid.astype(BF16), wsd_ref[...])
    h = h_ref[...] + g2_ref[0] * (_load_token_tiles(routed_ref, rows) + shared)
    o_ref[...] = _rms(h) * gf_ref[...] if last else h


def _moe_finish_call(x, routed, h, wsg, wsu, wsd, ada, row0, tiles_per_row, final_g, last):
    m = h.shape[0]
    tok = pl.BlockSpec((TOK_TILE, D_MODEL), lambda i: (i, 0))
    tiles = pl.BlockSpec((TOK_TILE * CHUNKS, LANES), lambda i: (i, 0))
    full = lambda a: pl.BlockSpec(a.shape, lambda i: (0,) * a.ndim)
    return pl.pallas_call(
        functools.partial(_moe_finish_kernel, last=last),
        grid=(m // TOK_TILE,),
        in_specs=[tiles, tiles, tok, full(wsg), full(wsu), full(wsd), _ada_spec(5, row0, tiles_per_row),
                  full(final_g)],
        out_specs=tok,
        out_shape=jax.ShapeDtypeStruct((m, D_MODEL), F32),
        compiler_params=_cparams(("arbitrary",)),
        name="moe_finish",
    )(x, routed, h, wsg, wsu, wsd, ada, final_g)


def _stream(h, p, ada, row0, bsz, seq, attend, states):
    toks_per_row = seq if row0 > 0 else bsz * seq
    tiles_per_row = toks_per_row // TOK_TILE
    keys, vals, sts = [], [], []
    for l in range(DEPTH):
        ada_l = ada[l].reshape(8, 1, N_MOD * D_MODEL)
        n = _norm_mod_call(h, p['norm1_g'][l:l + 1], ada_l, 1, 0, row0, tiles_per_row)
        z = _in_proj_call(n, p['w_in'], l)
        o_a = attend(z, l)
        o_c, st = _hgrn_call(z, p['lb_fwd'], p['lb_bwd'], p['hgrn_norm_g'][l:l + 1], states[l], bsz, seq, l)
        h, n2, lgt = _mix_call(o_a, o_c, z, h, p['w_conv'][l], p['w_br_a'][l], p['w_br_b'][l], p['w_br_c'][l],
                               p['w_out'][l], p['w_router_t'][l], p['norm2_g'][l:l + 1], ada_l,
                               row0, toks_per_row, seq)
        wpick, eidx, rank, cnt = _routing_call(lgt, p['b_router'][l])
        routed = _moe_call(n2, wpick, eidx, rank, cnt, p['w_gate'], p['w_up'], p['w_down'], l)
        h = _moe_finish_call(n2, routed, h, p['w_sh_gate'][l], p['w_sh_up'][l], p['w_sh_down'][l],
                             ada_l, row0, tiles_per_row, p['final_norm_g'], l == DEPTH - 1)
        keys.append(z[:, B_KA * COL_BLK:(B_KA + 1) * COL_BLK])
        vals.append(z[:, B_VA * COL_BLK:(B_VA + 1) * COL_BLK])
        sts.append(st)
    return h, keys, vals, sts


def kernel(x_prompt, x_sample, cache_k, cache_v, state_hgrn, c, c_ctx, norm1_g, norm2_g, w_ada, b_ada,
           w_in, rpb, w_conv, lb_fwd, lb_bwd, hgrn_norm_g, w_br_a, w_br_b, w_br_c, w_out, w_router,
           b_router, w_gate, w_up, w_down, w_sh_gate, w_sh_up, w_sh_down, final_norm_g):
    batch, seq, _ = x_prompt.shape
    dec_batch, dec_seq, _ = x_sample.shape
    past = cache_k.shape[2]
    p = dict(norm1_g=norm1_g, norm2_g=norm2_g, w_in=w_in, w_conv=w_conv, lb_fwd=lb_fwd, lb_bwd=lb_bwd,
             hgrn_norm_g=hgrn_norm_g,
             w_br_a=w_br_a.astype(BF16), w_br_b=w_br_b.astype(BF16), w_br_c=w_br_c.astype(BF16),
             w_out=w_out.astype(BF16), w_router_t=jnp.swapaxes(w_router, 1, 2), b_router=b_router,
             w_gate=w_gate, w_up=w_up, w_down=w_down,
             w_sh_gate=w_sh_gate.astype(BF16), w_sh_up=w_sh_up.astype(BF16), w_sh_down=w_sh_down.astype(BF16),
             final_norm_g=final_norm_g.reshape(1, D_MODEL))

    c8 = jnp.zeros((8, D_MODEL), F32).at[0].set(c_ctx).at[1:1 + dec_batch].set(c)
    ada = _ada_call(c8, w_ada, b_ada)

    zero_states = [jnp.zeros((batch, 2, H_C, DK_C, DK_C), F32)] * DEPTH
    ctx_attend = lambda z, l: _ctx_attn_call(z, seq)
    h_ctx, keys, vals, sts = _stream(x_prompt.reshape(batch * seq, D_MODEL), p, ada, 0, batch, seq,
                                     ctx_attend, zero_states)
    y_prompt = h_ctx.reshape(batch, seq, D_MODEL)
    new_cache_k = jnp.stack([k.reshape(batch, seq, H_A, DH_A) for k in keys], axis=1)
    new_cache_v = jnp.stack([v.reshape(batch, seq, H_A, DH_A) for v in vals], axis=1)
    new_state = jnp.stack(sts, axis=1)

    rows = dec_seq // GRID_W
    ck = cache_k.reshape(dec_batch, DEPTH, past, D_A)
    cv = cache_v.reshape(dec_batch, DEPTH, past, D_A)
    lat_states = [state_hgrn[:, l].astype(F32) for l in range(DEPTH)]

    bias = _na_bias_tables(rpb.reshape(DEPTH * H_A, 2 * WIN_H - 1, 2 * WIN_W - 1), rows)

    def lat_attend(z, l):
        return _na_attn_call(z, ck[:, l], cv[:, l], bias, dec_batch, dec_seq, l)

    h_lat, _, _, _ = _stream(x_sample.reshape(dec_batch * dec_seq, D_MODEL), p, ada, 1, dec_batch, dec_seq,
                             lat_attend, lat_states)
    y_sample = h_lat.reshape(dec_batch, dec_seq, D_MODEL)
    return (y_prompt, y_sample, new_cache_k, new_cache_v, new_state)
```

```python
import functools

import numpy as np
import jax
import jax.numpy as jnp
from jax import lax
from jax.experimental import pallas as pl
from jax.experimental.pallas import tpu as pltpu

F32 = jnp.float32
BF16 = jnp.bfloat16

D_MODEL = 1024
DEPTH = 2
GRID_W = 64
H_A = 8
DH_A = 64
D_A = H_A * DH_A
WIN_H = 8
WIN_W = 16
D_CONV = 512
H_C = 4
DK_C = 128
CHUNK = 16
N_EXPERTS = 64
N_GROUPS = 8
GROUP_SIZE = N_EXPERTS // N_GROUPS
TOPK_GROUPS = 4
TOP_K = 8
D_EXPERT = 256
ROUTED_SCALE = 2.5
N_MOD = 6
EPS = 1e-6
NEG_BIG = -1e30
LOG_FLOOR = 1e-30

D_IN = 8704
COL_BLK = 512
N_COL_BLKS = D_IN // COL_BLK
SRC_GATE_BLK = 11
N_GATE_BLKS = 6
B_GA, B_GBR, B_GCR = 0, 2, 4
B_QA, B_KA, B_VA, B_U, B_GB, B_GC, B_QC, B_ZF, B_ZB, B_IC, B_OC = range(6, 17)

NA_QROWS = 4
NA_KROWS = 12
NA_TQ = NA_QROWS * GRID_W
NA_TK = NA_KROWS * GRID_W

TOK_TILE = 256
VMEM_LIMIT = 56 * 1024 * 1024


def _cparams(sem):
    return pltpu.CompilerParams(dimension_semantics=sem, vmem_limit_bytes=VMEM_LIMIT)


def _sigmoid(x):
    return 1.0 / (1.0 + jnp.exp(-x))


def _silu(x):
    return x * _sigmoid(x)


def _dot(a, b):
    return jnp.dot(a, b, preferred_element_type=F32)


def _dot_nt(a, b, precision=None):
    return lax.dot_general(a, b, (((1,), (1,)), ((), ())), preferred_element_type=F32,
                           precision=precision)


LANES = 128
CHUNKS = D_MODEL // LANES


def _store_token_tiles(ref, val):
    for rb in range(val.shape[0] // 8):
        for c in range(CHUNKS):
            ref[pl.ds(rb * 8 * CHUNKS + c, 8, stride=CHUNKS), :] = \
                val[rb * 8:(rb + 1) * 8, c * LANES:(c + 1) * LANES]


def _load_token_tiles(ref, rows):
    return jnp.concatenate([ref[pl.ds(c, rows, stride=CHUNKS), :] for c in range(CHUNKS)], axis=1)


def _ada_kernel(c_ref, w_ref, b_ref, o_ref):
    a = _silu(c_ref[...])
    o_ref[0] = _dot(a, w_ref[0]) + b_ref[0]


def _ada_call(c8, w_ada, b_ada):
    tn = 1536
    n_out = N_MOD * D_MODEL
    return pl.pallas_call(
        _ada_kernel,
        grid=(DEPTH, n_out // tn),
        in_specs=[
            pl.BlockSpec((8, D_MODEL), lambda l, j: (0, 0)),
            pl.BlockSpec((1, D_MODEL, tn), lambda l, j: (l, 0, j)),
            pl.BlockSpec((1, 1, tn), lambda l, j: (l, 0, j)),
        ],
        out_specs=pl.BlockSpec((1, 8, tn), lambda l, j: (l, 0, j)),
        out_shape=jax.ShapeDtypeStruct((DEPTH, 8, n_out), F32),
        compiler_params=_cparams(("arbitrary", "arbitrary")),
        name="ada",
    )(c8, w_ada, b_ada.reshape(DEPTH, 1, n_out))


def _rms(x):
    return x * lax.rsqrt(jnp.mean(x * x, axis=-1, keepdims=True) + EPS)


def _norm_mod_kernel(h_ref, g_ref, sc_ref, sh_ref, o_ref):
    y = _rms(h_ref[...]) * g_ref[...]
    o_ref[...] = (y * (1.0 + sc_ref[0]) + sh_ref[0]).astype(o_ref.dtype)


def _ada_spec(mod_idx, row0, tiles_per_row):
    return pl.BlockSpec((1, 1, D_MODEL), lambda i: (row0 + i // tiles_per_row, 0, mod_idx))


def _norm_mod_call(h, g, ada, sc_idx, sh_idx, row0, tiles_per_row):
    m = h.shape[0]
    return pl.pallas_call(
        _norm_mod_kernel,
        grid=(m // TOK_TILE,),
        in_specs=[
            pl.BlockSpec((TOK_TILE, D_MODEL), lambda i: (i, 0)),
            pl.BlockSpec((1, D_MODEL), lambda i: (0, 0)),
            _ada_spec(sc_idx, row0, tiles_per_row),
            _ada_spec(sh_idx, row0, tiles_per_row),
        ],
        out_specs=pl.BlockSpec((TOK_TILE, D_MODEL), lambda i: (i, 0)),
        out_shape=jax.ShapeDtypeStruct((m, D_MODEL), BF16),
        compiler_params=_cparams(("arbitrary",)),
        name="norm_mod",
    )(h, g, ada, ada)


IN_PROJ_ROWS = 1024


def _in_proj_kernel(a_ref, w_ref, o_ref):
    w = w_ref[...].astype(BF16)
    for r in range(a_ref.shape[0] // IN_PROJ_ROWS):
        rows = slice(r * IN_PROJ_ROWS, (r + 1) * IN_PROJ_ROWS)
        o_ref[rows, :] = _dot(a_ref[rows, :], w)


def _src_col_blk(j):
    return jnp.where(j < N_GATE_BLKS, j + SRC_GATE_BLK, j - N_GATE_BLKS)


def _in_proj_call(n, w_in, layer):
    m = n.shape[0]
    return pl.pallas_call(
        _in_proj_kernel,
        grid=(N_COL_BLKS,),
        in_specs=[
            pl.BlockSpec((m, D_MODEL), lambda j: (0, 0)),
            pl.BlockSpec((None, D_MODEL, COL_BLK), lambda j: (layer, 0, _src_col_blk(j))),
        ],
        out_specs=pl.BlockSpec((m, COL_BLK), lambda j: (0, j)),
        out_shape=jax.ShapeDtypeStruct((m, D_IN), F32),
        compiler_params=_cparams(("arbitrary",)),
        name="in_proj",
    )(n, w_in)


def _softmax_pv(s_list, v_list):
    m = s_list[0].max(axis=-1, keepdims=True)
    for s in s_list[1:]:
        m = jnp.maximum(m, s.max(axis=-1, keepdims=True))
    num = None
    den = None
    for s, v in zip(s_list, v_list):
        p = jnp.exp(s - m)
        d = p.sum(axis=-1, keepdims=True)
        o = _dot(p.astype(BF16), v.astype(BF16))
        num = o if num is None else num + o
        den = d if den is None else den + d
    return num / den


assert DH_A == 4 ** (DH_A.bit_length() // 2)


def _ctx_attn_kernel(q_ref, k_ref, v_ref, o_ref):
    scale = DH_A ** -0.5
    for h in range(H_A):
        sl = slice(h * DH_A, (h + 1) * DH_A)
        s = _dot_nt((q_ref[:, sl] * scale).astype(BF16), k_ref[:, sl].astype(BF16))
        o_ref[:, sl] = _softmax_pv([s], [v_ref[:, sl]])


def _ctx_attn_call(z, seq):
    m = z.shape[0]
    spec = lambda blk: pl.BlockSpec((seq, D_A), lambda b: (b, blk))
    return pl.pallas_call(
        _ctx_attn_kernel,
        grid=(m // seq,),
        in_specs=[spec(B_QA), spec(B_KA), spec(B_VA)],
        out_specs=pl.BlockSpec((seq, D_A), lambda b: (b, 0)),
        out_shape=jax.ShapeDtypeStruct((m, D_A), F32),
        compiler_params=_cparams(("arbitrary",)),
        name="ctx_attn",
    )(z, z, z)


def _na_key_row0(rb, rows):
    return jnp.clip(NA_QROWS * rb - (NA_KROWS - WIN_H) , 0, rows - NA_KROWS)


def _na_attn_kernel(q_ref, k_ref, v_ref, ck_ref, cv_ref, bias_ref, o_ref, *, rows):
    scale = DH_A ** -0.5
    rb = pl.program_id(1)
    k0 = pl.multiple_of(_na_key_row0(rb, rows) * GRID_W, GRID_W)
    for h in range(H_A):
        sl = slice(h * DH_A, (h + 1) * DH_A)
        q = (q_ref[:, sl] * scale).astype(BF16)
        kw = k_ref[pl.ds(k0, NA_TK), sl]
        vw = v_ref[pl.ds(k0, NA_TK), sl]
        s_lat = _dot_nt(q, kw.astype(BF16)) + bias_ref[h]
        s_ctx = _dot_nt(q, ck_ref[:, sl].astype(BF16))
        o_ref[:, sl] = _softmax_pv([s_lat, s_ctx], [vw, cv_ref[:, sl]])


def _na_bias_pattern(rb, n_rb):
    return jnp.where(rb == 0, 0, jnp.where(rb == n_rb - 1, 2, 1))


def _na_bias_tables(rpb_l, rows):
    n_heads = rpb_l.shape[0]
    qc = np.arange(GRID_W)
    q_start = np.clip(qc - WIN_W // 2, 0, GRID_W - WIN_W)
    kc = np.arange(GRID_W)
    valid_c = (kc[None, :] >= q_start[:, None]) & (kc[None, :] < q_start[:, None] + WIN_W)
    pad = GRID_W - WIN_W
    rpb_pad = jnp.pad(rpb_l.astype(F32), ((0, 0), (0, 0), (pad, pad)))
    col_tab = jnp.stack([rpb_pad[:, :, GRID_W - 1 - c:2 * GRID_W - 1 - c] for c in range(GRID_W)], axis=2)
    col_tab = jnp.where(valid_c, col_tab, NEG_BIG)
    col_tab = jnp.concatenate([col_tab, jnp.full((n_heads, 1, GRID_W, GRID_W), NEG_BIG, F32)], axis=1)
    n_tab = col_tab.shape[1]
    return pl.pallas_call(
        functools.partial(_na_bias_kernel, rows=rows),
        grid=(3, n_heads),
        in_specs=[pl.BlockSpec((None, n_tab, GRID_W, GRID_W), lambda p, h: (h, 0, 0, 0))],
        out_specs=pl.BlockSpec((None, None, NA_TQ, NA_TK), lambda p, h: (p, h, 0, 0)),
        out_shape=jax.ShapeDtypeStruct((3, n_heads, NA_TQ, NA_TK), F32),
        compiler_params=_cparams(("arbitrary", "arbitrary")),
        name="na_bias",
    )(col_tab)


def _na_bias_kernel(t_ref, o_ref, *, rows):
    n_rb = rows // NA_QROWS
    p = pl.program_id(0)
    rb = jnp.where(p == 0, 0, jnp.where(p == 1, 1, n_rb - 1))
    k_row0 = _na_key_row0(rb, rows)
    masked = t_ref.shape[0] - 1
    for i in range(NA_QROWS):
        r = NA_QROWS * rb + i
        w0 = jnp.clip(r - WIN_H // 2, 0, rows - WIN_H)
        for j in range(NA_KROWS):
            kr = k_row0 + j
            in_window = (kr >= w0) & (kr < w0 + WIN_H)
            tab = jnp.where(in_window, kr - r + WIN_H - 1, masked)
            o_ref[i * GRID_W:(i + 1) * GRID_W, j * GRID_W:(j + 1) * GRID_W] = t_ref[tab]


def _na_attn_call(z, ck, cv, bias, bsz, seq, layer):
    rows = seq // GRID_W
    n_rb = rows // NA_QROWS
    m = z.shape[0]
    kv_spec = lambda blk: pl.BlockSpec((seq, D_A), lambda b, r: (b, blk))
    c_spec = pl.BlockSpec((None, ck.shape[1], D_A), lambda b, r: (b, 0, 0))
    return pl.pallas_call(
        functools.partial(_na_attn_kernel, rows=rows),
        grid=(bsz, n_rb),
        in_specs=[
            pl.BlockSpec((NA_TQ, D_A), lambda b, r: (b * n_rb + r, B_QA)),
            kv_spec(B_KA), kv_spec(B_VA), c_spec, c_spec,
            pl.BlockSpec((None, H_A, NA_TQ, NA_TK), lambda b, r: (_na_bias_pattern(r, n_rb), layer, 0, 0)),
        ],
        out_specs=pl.BlockSpec((NA_TQ, D_A), lambda b, r: (b * n_rb + r, 0)),
        out_shape=jax.ShapeDtypeStruct((m, D_A), F32),
        compiler_params=_cparams(("arbitrary", "arbitrary")),
        name="na_attn",
    )(z, z, z, ck, cv, bias)


def _log1p(x):
    return jnp.log1p(x)


def _hgrn_gates(z, lb):
    log_sig = jnp.minimum(z, 0.0) - _log1p(jnp.exp(-jnp.abs(z)))
    a = _log1p(-lb) + log_sig
    b = jnp.log(jnp.maximum(lb, LOG_FLOOR))
    logf = jnp.maximum(a, b) + _log1p(jnp.exp(-jnp.abs(a - b)))
    key = (1.0 - lb) * _sigmoid(-z)
    return logf, key


def _lower_bound(p_ref, layer):
    p = p_ref[...]
    e = jnp.exp(p - jnp.max(p, axis=0, keepdims=True))
    sm = e / jnp.sum(e, axis=0, keepdims=True)
    lb = jnp.zeros((1, DK_C), F32)
    for j in range(1, layer + 1):
        lb = lb + sm[j:j + 1]
    return lb


def _chunk_cumsum(x, seq, reverse):
    pos = lax.broadcasted_iota(jnp.int32, x.shape, 0) % CHUNK
    s = 1
    while s < CHUNK:
        if reverse:
            x = x + jnp.where(pos < CHUNK - s, pltpu.roll(x, seq - s, axis=0), 0.0)
        else:
            x = x + jnp.where(pos >= s, pltpu.roll(x, s, axis=0), 0.0)
        s *= 2
    return x


HALF = CHUNK // 2
LOG2_E = 1.4426950408889634
HGRN_UNROLL = 4


def _hgrn_chunk(q_ref, v_ref, b_s, k_s, o_s, r0, st, reverse):
    b = b_s[pl.ds(r0, CHUNK), :]
    q = q_ref[pl.ds(r0, CHUNK), :]
    k = k_s[pl.ds(r0, CHUNK), :]
    v = v_ref[pl.ds(r0, CHUNK), :]
    o_inter = _dot_nt(q * jnp.exp2(b), st)
    t_iota = lax.broadcasted_iota(jnp.int32, (HALF, 1), 0)
    o_half = [None, None]
    for s in range(CHUNK):
        b_row = b_s[pl.ds(r0 + s, 1), :]
        k_row = k_s[pl.ds(r0 + s, 1), :]
        v_row = v_ref[pl.ds(r0 + s, 1), :]
        for half in range(2):
            t0 = half * HALF
            if (t0 + HALF - 1 < s) if not reverse else (t0 > s):
                continue
            sl = slice(t0, t0 + HALF)
            decay = jnp.exp2(jnp.minimum(b[sl] - b_row, 0.0))
            col = jnp.sum(q[sl] * (k_row * decay), axis=-1, keepdims=True)
            if not ((t0 >= s) if not reverse else (t0 + HALF - 1 <= s)):
                keep = (t_iota + t0 >= s) if not reverse else (t_iota + t0 <= s)
                col = jnp.where(keep, col, 0.0)
            o_half[half] = col * v_row if o_half[half] is None else o_half[half] + col * v_row
    o_s[pl.ds(r0, CHUNK), :] = jnp.concatenate(o_half, axis=0) + o_inter
    edge = 0 if reverse else CHUNK - 1
    b_edge = b[edge:edge + 1, :]
    kh = k * jnp.exp2(b_edge - b)
    kv_t = lax.dot_general(v, kh, (((0,), (0,)), ((), ())), preferred_element_type=F32)
    return st * jnp.exp2(b_edge) + kv_t


def _hgrn_kernel(q_ref, zf_ref, zb_ref, i_ref, og_ref, lbf_ref, lbb_ref, g_ref, s0_ref,
                 o_ref, sfin_ref, of_s, ob_s, bf_s, bb_s, kf_s, kb_s, *, seq, layer):
    n_chunks = seq // CHUNK
    logf, key = _hgrn_gates(zf_ref[...], _lower_bound(lbf_ref, layer))
    bf_s[...] = _chunk_cumsum(logf * LOG2_E, seq, False)
    kf_s[...] = key
    logf, key = _hgrn_gates(zb_ref[...], _lower_bound(lbb_ref, layer))
    bb_s[...] = _chunk_cumsum(logf * LOG2_E, seq, True)
    kb_s[...] = key

    def body(it, carry):
        st_f, st_b = carry
        for u in range(HGRN_UNROLL):
            n = it * HGRN_UNROLL + u
            st_f = _hgrn_chunk(q_ref, i_ref, bf_s, kf_s, of_s, pl.multiple_of(n * CHUNK, CHUNK), st_f, False)
            st_b = _hgrn_chunk(q_ref, i_ref, bb_s, kb_s, ob_s,
                               pl.multiple_of((n_chunks - 1 - n) * CHUNK, CHUNK), st_b, True)
        return st_f, st_b

    st_f, st_b = lax.fori_loop(0, n_chunks // HGRN_UNROLL, body, (s0_ref[0].T, s0_ref[1].T))
    sfin_ref[0] = st_f.T
    sfin_ref[1] = st_b.T
    o_ref[...] = _rms(of_s[...] + ob_s[...]) * g_ref[...] * _silu(og_ref[...])


def _hgrn_call(z, lb_fwd, lb_bwd, norm_g, s0, bsz, seq, layer):
    m = z.shape[0]
    w = DK_C
    per = COL_BLK // w
    zspec = lambda blk: pl.BlockSpec((seq, w), lambda b, h: (b, blk * per + h))
    st_spec = pl.BlockSpec((None, 2, None, DK_C, DK_C), lambda b, h: (b, 0, h, 0, 0))
    return pl.pallas_call(
        functools.partial(_hgrn_kernel, seq=seq, layer=layer),
        grid=(bsz, H_C),
        in_specs=[
            zspec(B_QC), zspec(B_ZF), zspec(B_ZB), zspec(B_IC), zspec(B_OC),
            pl.BlockSpec((DEPTH, w), lambda b, h: (0, h)),
            pl.BlockSpec((DEPTH, w), lambda b, h: (0, h)),
            pl.BlockSpec((1, w), lambda b, h: (0, h)),
            st_spec,
        ],
        out_specs=[pl.BlockSpec((seq, w), lambda b, h: (b, h)), st_spec],
        out_shape=[jax.ShapeDtypeStruct((m, H_C * w), F32),
                   jax.ShapeDtypeStruct((bsz, 2, H_C, DK_C, DK_C), F32)],
        scratch_shapes=[pltpu.VMEM((seq, w), F32)] * 6,
        compiler_params=_cparams(("arbitrary", "arbitrary")),
        name="hgrn",
    )(z, z, z, z, z, lb_fwd, lb_bwd, norm_g, s0)


def _mix_kernel(oa_ref, oc_ref, u_ref, gb_ref, gc_ref, up_ref, gcp_ref, un_ref, gcn_ref,
                ga_ref, gbr_ref, gcr_ref, h_ref, wconv_ref, wa_ref, wb_ref, wc_ref, wout_ref,
                wrt_ref, g1_ref, n2g_ref, sc2_ref, sh2_ref,
                hout_ref, n2_ref, lgt_ref, *, seq):
    tm = u_ref.shape[0]
    zc = gc_ref[...] * u_ref[...]
    row = lax.broadcasted_iota(jnp.int32, zc.shape, 0)
    pos = (pl.program_id(0) * tm + row) % seq
    zp = jnp.where(row == 0, gcp_ref[7:8, :] * up_ref[7:8, :], pltpu.roll(zc, 1, axis=0))
    zn = jnp.where(row == tm - 1, gcn_ref[0:1, :] * un_ref[0:1, :], pltpu.roll(zc, tm - 1, axis=0))
    zp = jnp.where(pos == 0, 0.0, zp)
    zn = jnp.where(pos == seq - 1, 0.0, zn)
    w = wconv_ref[...]
    ob = gb_ref[...] * (w[0:1] * zp + w[1:2] * zc + w[2:3] * zn)
    ya = _dot(oa_ref[...].astype(BF16), wa_ref[...])
    yb = _dot(ob.astype(BF16), wb_ref[...])
    yc = _dot(oc_ref[...].astype(BF16), wc_ref[...])
    pre = _sigmoid(ga_ref[...]) * ya + _sigmoid(gbr_ref[...]) * yb + _sigmoid(gcr_ref[...]) * yc
    hn = h_ref[...] + g1_ref[0] * _dot(pre.astype(BF16), wout_ref[...])
    hout_ref[...] = hn
    n2 = _rms(hn) * n2g_ref[...] * (1.0 + sc2_ref[0]) + sh2_ref[0]
    _store_token_tiles(n2_ref, n2)
    lgt_ref[...] = _dot_nt(wrt_ref[...], n2, precision=lax.Precision.HIGHEST)


MIX_TILE = 512


def _mix_call(o_a, o_c, z, h, w_conv, wa, wb, wc, wout, wrt, n2g, ada, row0, toks_per_row, seq):
    m = h.shape[0]
    tm = MIX_TILE
    nt = m // tm
    tiles_per_row = toks_per_row // tm
    r8 = tm // 8
    zblk = lambda blk: pl.BlockSpec((tm, COL_BLK), lambda i: (i, blk))
    zprev = lambda blk: pl.BlockSpec((8, COL_BLK), lambda i: (jnp.maximum(i * r8 - 1, 0), blk))
    znext = lambda blk: pl.BlockSpec((8, COL_BLK), lambda i: (jnp.minimum((i + 1) * r8, m // 8 - 1), blk))
    zgate = lambda blk: pl.BlockSpec((tm, D_MODEL), lambda i: (i, blk // 2))
    full = lambda a: pl.BlockSpec(a.shape, lambda i: (0,) * a.ndim)
    tok = lambda wdt: pl.BlockSpec((tm, wdt), lambda i: (i, 0))
    return pl.pallas_call(
        functools.partial(_mix_kernel, seq=seq),
        grid=(nt,),
        in_specs=[
            tok(D_A), tok(D_A), zblk(B_U), zblk(B_GB), zblk(B_GC),
            zprev(B_U), zprev(B_GC), znext(B_U), znext(B_GC),
            zgate(B_GA), zgate(B_GBR), zgate(B_GCR), tok(D_MODEL),
            full(w_conv), full(wa), full(wb), full(wc), full(wout), full(wrt),
            _ada_spec(2, row0, tiles_per_row), full(n2g),
            _ada_spec(4, row0, tiles_per_row), _ada_spec(3, row0, tiles_per_row),
        ],
        out_specs=[tok(D_MODEL), pl.BlockSpec((tm * CHUNKS, LANES), lambda i: (i, 0)),
                   pl.BlockSpec((N_EXPERTS, tm), lambda i: (0, i))],
        out_shape=[jax.ShapeDtypeStruct((m, D_MODEL), F32),
                   jax.ShapeDtypeStruct((m * CHUNKS, LANES), F32),
                   jax.ShapeDtypeStruct((N_EXPERTS, m), F32)],
        compiler_params=_cparams(("arbitrary",)),
        name="mix",
    )(o_a, o_c, z, z, z, z, z, z, z, z, z, z, h, w_conv, wa, wb, wc, wout, wrt, ada, n2g, ada, ada)


def _first_max(x, axes_iota, size):
    m = x
    for ax in range(x.ndim - 1):
        m = jnp.max(m, axis=ax, keepdims=True)
    first = jnp.where(x == m, axes_iota, size)
    f = first
    for ax in range(x.ndim - 1):
        f = jnp.min(f, axis=ax, keepdims=True)
    return axes_iota == f, f


def _first_max_mask(x, axes_iota, size):
    return _first_max(x, axes_iota, size)[0]


def _routing_kernel(lgt_ref, bias_ref, wpick_ref, eidx_ref, rank_ref, cnt_ref, carry_s):
    tn = lgt_ref.shape[-1]

    @pl.when(pl.program_id(0) == 0)
    def _():
        carry_s[...] = jnp.zeros(carry_s.shape, F32)

    s = _sigmoid(lgt_ref[...])
    sel = (s + bias_ref[...]).reshape(N_GROUPS, GROUP_SIZE, tn)
    s = s.reshape(N_GROUPS, GROUP_SIZE, tn)
    neg_inf = -jnp.inf
    mem_iota = lax.broadcasted_iota(jnp.int32, sel.shape, 1)
    m1 = jnp.max(sel, axis=1, keepdims=True)
    first = jnp.min(jnp.where(sel == m1, mem_iota, GROUP_SIZE), axis=1, keepdims=True)
    m2 = jnp.max(jnp.where(mem_iota == first, neg_inf, sel), axis=1, keepdims=True)
    grp = (m1 + m2).reshape(N_GROUPS, tn)
    g_iota = lax.broadcasted_iota(jnp.int32, grp.shape, 0)
    gmask = jnp.zeros(grp.shape, jnp.bool_)
    for _ in range(TOPK_GROUPS):
        pick = _first_max_mask(grp, g_iota, N_GROUPS)
        gmask = gmask | pick
        grp = jnp.where(pick, neg_inf, grp)
    cand = jnp.where(gmask.reshape(N_GROUPS, 1, tn), sel, NEG_BIG)
    e_iota = lax.broadcasted_iota(jnp.int32, sel.shape, 0) * GROUP_SIZE + mem_iota
    chosen = jnp.zeros(sel.shape, jnp.bool_)
    picked = []
    for _ in range(TOP_K):
        pick, idx = _first_max(cand, e_iota, N_EXPERTS)
        chosen = chosen | pick
        cand = jnp.where(pick, neg_inf, cand)
        picked.append(idx)
    w = jnp.where(chosen, s, 0.0)
    tot = jnp.sum(jnp.sum(w, axis=1, keepdims=True), axis=0, keepdims=True)
    w = w / tot * ROUTED_SCALE
    chosen_f = jnp.where(chosen, 1.0, 0.0).reshape(N_EXPERTS, tn)
    earlier = (lax.broadcasted_iota(jnp.int32, (tn, tn), 0) < lax.broadcasted_iota(jnp.int32, (tn, tn), 1))
    rank = _dot(chosen_f.astype(BF16), jnp.where(earlier, 1.0, 0.0).astype(BF16)) + carry_s[:, 0:1]
    rank = rank.reshape(N_GROUPS, GROUP_SIZE, tn)
    of_pick = lambda v, idx: jnp.sum(jnp.sum(jnp.where(e_iota == idx, v, 0.0), axis=1, keepdims=True),
                                     axis=0, keepdims=True).reshape(1, tn)
    eidx_ref[...] = jnp.concatenate([idx.reshape(1, tn) for idx in picked], axis=0)
    rank_ref[...] = jnp.concatenate([of_pick(rank, idx) for idx in picked], axis=0).astype(jnp.int32)
    wpick_ref[...] = jnp.concatenate([of_pick(w, idx) for idx in picked], axis=0)
    carry_s[...] = carry_s[...] + jnp.sum(chosen_f, axis=1, keepdims=True)
    cnt_ref[...] = carry_s[...]


def _routing_call(lgt, b_router_l):
    m = lgt.shape[1]
    tn = 256
    pick_spec = pl.BlockSpec((TOP_K, tn), lambda i: (0, i))
    cnt_spec = pl.BlockSpec((N_EXPERTS, 128), lambda i: (0, 0))
    return pl.pallas_call(
        _routing_kernel,
        grid=(m // tn,),
        in_specs=[pl.BlockSpec((N_EXPERTS, tn), lambda i: (0, i)),
                  pl.BlockSpec((N_EXPERTS, tn), lambda i: (0, 0))],
        out_specs=[pick_spec, pick_spec, pick_spec, cnt_spec],
        out_shape=[jax.ShapeDtypeStruct((TOP_K, m), F32),
                   jax.ShapeDtypeStruct((TOP_K, m), jnp.int32),
                   jax.ShapeDtypeStruct((TOP_K, m), jnp.int32),
                   jax.ShapeDtypeStruct((N_EXPERTS, 128), F32)],
        scratch_shapes=[pltpu.VMEM((N_EXPERTS, 128), F32)],
        compiler_params=_cparams(("arbitrary",)),
        name="routing",
    )(lgt, jnp.broadcast_to(b_router_l[:, None], (N_EXPERTS, tn)))


MOE_TM = 256
ROW_GROUP = 16
assert TOP_K == CHUNKS and TOP_K & (TOP_K - 1) == 0


def _token_row(idx):
    return pl.multiple_of(idx & ~(TOP_K - 1), CHUNKS)


def _dispatch_plan(eidx, rank, cnt, n_tiles_max):
    counts = cnt[:, 0].astype(jnp.int32)
    tiles = (counts + MOE_TM - 1) // MOE_TM
    tile_end = jnp.cumsum(tiles)
    tile_start = tile_end - tiles
    n_tiles = tile_end[-1]
    experts = jnp.arange(N_EXPERTS, dtype=jnp.int32)
    base = jnp.sum(jnp.where(eidx[:, :, None] == experts, tile_start * MOE_TM, 0), axis=-1)
    pos = (rank + base).T.reshape(-1)
    ti = jnp.arange(n_tiles_max, dtype=jnp.int32)
    tile_e = jnp.sum((ti[:, None] >= tile_end[None, :]).astype(jnp.int32), axis=1)
    last_e = jnp.sum(jnp.where(ti == n_tiles - 1, tile_e, 0))
    tile_e = jnp.where(ti < n_tiles, tile_e, last_e).astype(jnp.int32)
    first = ((ti == 0) | (tile_e != jnp.roll(tile_e, 1))) & (ti < n_tiles)
    slot = (jnp.cumsum(first.astype(jnp.int32)) - 1) % 2
    next_tile = jnp.sum(jnp.where(tile_e[:, None] == experts, tile_end, 0), axis=1)
    next_e = jnp.sum(jnp.where(next_tile[:, None] == ti[None, :], tile_e[None, :], 0), axis=1)
    next_e = jnp.where(next_tile < n_tiles, next_e, -1)
    i32 = lambda v: v.astype(jnp.int32)
    return pos, (tile_e, i32(first), i32(next_e), i32(slot), i32(n_tiles).reshape(1))


def _moe_step(tile_g, tile_s, rowidx_s, wpick_ref, x_v, acc_v, g_w, g_r, y_w, y_r, wg_bf, wu_bf, wd_bf):
    row_g = tile_g * MOE_TM
    for r in range(MOE_TM):
        g_w[pl.ds(r * CHUNKS, CHUNKS), :] = x_v[pl.ds(_token_row(rowidx_s[row_g + r]), CHUNKS), :]

    x = _load_token_tiles(g_r, MOE_TM).astype(BF16)
    hid = _silu(_dot(x, wg_bf[...])) * _dot(x, wu_bf[...])
    y = _dot(hid.astype(BF16), wd_bf[...])
    for rb in range(MOE_TM // 8):
        for c in range(CHUNKS):
            y_w[pl.ds((rb * CHUNKS + c) * 8, 8), :] = y[rb * 8:(rb + 1) * 8, c * LANES:(c + 1) * LANES]

    row_s = tile_s * MOE_TM
    for j in range(MOE_TM // ROW_GROUP):
        slots, rows_v = [], []
        for u in range(ROW_GROUP):
            r = j * ROW_GROUP + u
            idx = rowidx_s[row_s + r]
            slots.append(_token_row(idx))
            rows_v.append(y_r[pl.ds((r // 8) * (8 * CHUNKS) + r % 8, CHUNKS, stride=8), :] * wpick_ref[idx])
        olds = [acc_v[pl.ds(s, CHUNKS), :] for s in slots]
        for s, o, v in zip(slots, olds, rows_v):
            acc_v[pl.ds(s, CHUNKS), :] = o + v


def _moe_kernel(te_ref, first_ref, next_ref, slot_ref, nt_ref, pos_ref, wpick_ref, x_hbm, fill_hbm,
                wg_hbm, wu_hbm, wd_hbm, out_hbm,
                rowidx_s, x_v, acc_v, g0_s, g1_s, y0_s, y1_s, wg_f, wu_f, wd_f, wg_bf, wu_bf, wd_bf, sem, wsem,
                *, layer):
    def weight_copies(e, slot):
        return [pltpu.make_async_copy(hbm.at[layer, e], buf.at[slot], wsem.at[slot, k])
                for k, (hbm, buf) in enumerate(((wg_hbm, wg_f), (wu_hbm, wu_f), (wd_hbm, wd_f)))]

    s = pl.program_id(0)
    n_tok = x_hbm.shape[0] // CHUNKS
    n_tiles_max = te_ref.shape[0]
    zero_blk = 512

    @pl.when(s == 0)
    def _prologue():
        x_copy = pltpu.make_async_copy(x_hbm, x_v.at[pl.ds(0, n_tok * CHUNKS)], sem.at[0])
        fill_copy = pltpu.make_async_copy(fill_hbm, rowidx_s, sem.at[1])
        x_copy.start()
        fill_copy.start()
        for cp in weight_copies(te_ref[0], 0):
            cp.start()

        def zero(j, carry):
            acc_v[pl.ds(pl.multiple_of(j * zero_blk, zero_blk), zero_blk), :] = jnp.zeros((zero_blk, LANES), F32)
            return carry

        lax.fori_loop(0, n_tok * CHUNKS // zero_blk, zero, 0)
        acc_v[pl.ds(n_tok * CHUNKS, CHUNKS), :] = jnp.zeros((CHUNKS, LANES), F32)
        x_v[pl.ds(n_tok * CHUNKS, CHUNKS), :] = jnp.zeros((CHUNKS, LANES), F32)
        g1_s[...] = jnp.zeros(g1_s.shape, F32)
        y0_s[...] = jnp.zeros(y0_s.shape, F32)
        fill_copy.wait()

        def invert(j, carry):
            for u in range(ROW_GROUP):
                idx = j * ROW_GROUP + u
                rowidx_s[pos_ref[idx]] = idx
            return carry

        lax.fori_loop(0, n_tok * TOP_K // ROW_GROUP, invert, 0)
        x_copy.wait()

    active = s < nt_ref[0] + 2
    ffn_tile = jnp.clip(s - 1, 0, n_tiles_max - 1)

    @pl.when(active & (first_ref[ffn_tile] == 1) & (s != 1))
    def _next_expert():
        slot = slot_ref[ffn_tile]
        for cp in weight_copies(te_ref[ffn_tile], slot):
            cp.wait()
        wg_bf[...] = wg_f[slot].astype(BF16)
        wu_bf[...] = wu_f[slot].astype(BF16)
        wd_bf[...] = wd_f[slot].astype(BF16)

        @pl.when(next_ref[ffn_tile] >= 0)
        def _prefetch():
            for cp in weight_copies(next_ref[ffn_tile], 1 - slot):
                cp.start()

    tile_g = jnp.minimum(s, n_tiles_max - 1)
    tile_s = jnp.maximum(s - 2, 0)
    common = (rowidx_s, wpick_ref, x_v, acc_v)
    weights = (wg_bf, wu_bf, wd_bf)

    @pl.when(active & (s % 2 == 0))
    def _even():
        _moe_step(tile_g, tile_s, *common, g0_s, g1_s, y1_s, y0_s, *weights)

    @pl.when(active & (s % 2 == 1))
    def _odd():
        _moe_step(tile_g, tile_s, *common, g1_s, g0_s, y0_s, y1_s, *weights)

    @pl.when(s == pl.num_programs(0) - 1)
    def _epilogue():
        out_copy = pltpu.make_async_copy(acc_v.at[pl.ds(0, n_tok * CHUNKS)], out_hbm, sem.at[2])
        out_copy.start()
        out_copy.wait()


def _moe_call(x, wpick, eidx, rank, cnt, w_gate, w_up, w_down, layer):
    m = x.shape[0] // CHUNKS
    n_rows = TOP_K * m
    n_tiles_max = n_rows // MOE_TM + N_EXPERTS
    n_slots = n_tiles_max * MOE_TM
    pos, tables = _dispatch_plan(eidx, rank, cnt, n_tiles_max)
    wflat = jnp.concatenate([wpick.T.reshape(-1), jnp.zeros((TOP_K,), F32)])
    anyspec = pl.BlockSpec(memory_space=pl.ANY)
    smem = pl.BlockSpec(memory_space=pltpu.SMEM)
    tile_buf = pltpu.VMEM((MOE_TM * CHUNKS, LANES), F32)
    grid_spec = pltpu.PrefetchScalarGridSpec(
        num_scalar_prefetch=len(tables),
        grid=(n_tiles_max + 2,),
        in_specs=[smem, smem, anyspec, anyspec, anyspec, anyspec, anyspec],
        out_specs=anyspec,
        scratch_shapes=[
            pltpu.SMEM((n_slots,), jnp.int32),
            pltpu.VMEM(((m + 1) * CHUNKS, LANES), F32),
            pltpu.VMEM(((m + 1) * CHUNKS, LANES), F32),
            tile_buf, tile_buf, tile_buf, tile_buf,
            pltpu.VMEM((2, D_MODEL, D_EXPERT), F32),
            pltpu.VMEM((2, D_MODEL, D_EXPERT), F32),
            pltpu.VMEM((2, D_EXPERT, D_MODEL), F32),
            pltpu.VMEM((D_MODEL, D_EXPERT), BF16),
            pltpu.VMEM((D_MODEL, D_EXPERT), BF16),
            pltpu.VMEM((D_EXPERT, D_MODEL), BF16),
            pltpu.SemaphoreType.DMA((3,)),
            pltpu.SemaphoreType.DMA((2, 3)),
        ],
    )
    return pl.pallas_call(
        functools.partial(_moe_kernel, layer=layer),
        grid_spec=grid_spec,
        out_shape=jax.ShapeDtypeStruct((m * CHUNKS, LANES), F32),
        compiler_params=_cparams(("arbitrary",)),
        name="moe",
    )(*tables, pos, wflat, x, jnp.full((n_slots,), n_rows, jnp.int32), w_gate, w_up, w_down)


def _moe_finish_kernel(x_ref, routed_ref, h_ref, wsg_ref, wsu_ref, wsd_ref, g2_ref, gf_ref, o_ref, *, last):
    rows = h_ref.shape[0]
    x = _load_token_tiles(x_ref, rows).astype(BF16)
    hid = _silu(_dot(x, wsg_ref[...])) * _dot(x, wsu_ref[...])
    shared = _dot(hid.astype(BF16), wsd_ref[...])
    h = h_ref[...] + g2_ref[0] * (_load_token_tiles(routed_ref, rows) + shared)
    o_ref[...] = _rms(h) * gf_ref[...] if last else h


def _moe_finish_call(x, routed, h, wsg, wsu, wsd, ada, row0, tiles_per_row, final_g, last):
    m = h.shape[0]
    tok = pl.BlockSpec((TOK_TILE, D_MODEL), lambda i: (i, 0))
    tiles = pl.BlockSpec((TOK_TILE * CHUNKS, LANES), lambda i: (i, 0))
    full = lambda a: pl.BlockSpec(a.shape, lambda i: (0,) * a.ndim)
    return pl.pallas_call(
        functools.partial(_moe_finish_kernel, last=last),
        grid=(m // TOK_TILE,),
        in_specs=[tiles, tiles, tok, full(wsg), full(wsu), full(wsd), _ada_spec(5, row0, tiles_per_row),
                  full(final_g)],
        out_specs=tok,
        out_shape=jax.ShapeDtypeStruct((m, D_MODEL), F32),
        compiler_params=_cparams(("arbitrary",)),
        name="moe_finish",
    )(x, routed, h, wsg, wsu, wsd, ada, final_g)


def _stream(h, p, ada, row0, bsz, seq, attend, states):
    toks_per_row = seq if row0 > 0 else bsz * seq
    tiles_per_row = toks_per_row // TOK_TILE
    keys, vals, sts = [], [], []
    for l in range(DEPTH):
        ada_l = ada[l].reshape(8, 1, N_MOD * D_MODEL)
        n = _norm_mod_call(h, p['norm1_g'][l:l + 1], ada_l, 1, 0, row0, tiles_per_row)
        z = _in_proj_call(n, p['w_in'], l)
        o_a = attend(z, l)
        o_c, st = _hgrn_call(z, p['lb_fwd'], p['lb_bwd'], p['hgrn_norm_g'][l:l + 1], states[l], bsz, seq, l)
        h, n2, lgt = _mix_call(o_a, o_c, z, h, p['w_conv'][l], p['w_br_a'][l], p['w_br_b'][l], p['w_br_c'][l],
                               p['w_out'][l], p['w_router_t'][l], p['norm2_g'][l:l + 1], ada_l,
                               row0, toks_per_row, seq)
        wpick, eidx, rank, cnt = _routing_call(lgt, p['b_router'][l])
        routed = _moe_call(n2, wpick, eidx, rank, cnt, p['w_gate'], p['w_up'], p['w_down'], l)
        h = _moe_finish_call(n2, routed, h, p['w_sh_gate'][l], p['w_sh_up'][l], p['w_sh_down'][l],
                             ada_l, row0, tiles_per_row, p['final_norm_g'], l == DEPTH - 1)
        keys.append(z[:, B_KA * COL_BLK:(B_KA + 1) * COL_BLK])
        vals.append(z[:, B_VA * COL_BLK:(B_VA + 1) * COL_BLK])
        sts.append(st)
    return h, keys, vals, sts


def kernel(x_prompt, x_sample, cache_k, cache_v, state_hgrn, c, c_ctx, norm1_g, norm2_g, w_ada, b_ada,
           w_in, rpb, w_conv, lb_fwd, lb_bwd, hgrn_norm_g, w_br_a, w_br_b, w_br_c, w_out, w_router,
           b_router, w_gate, w_up, w_down, w_sh_gate, w_sh_up, w_sh_down, final_norm_g):
    batch, seq, _ = x_prompt.shape
    dec_batch, dec_seq, _ = x_sample.shape
    past = cache_k.shape[2]
    p = dict(norm1_g=norm1_g, norm2_g=norm2_g, w_in=w_in, w_conv=w_conv, lb_fwd=lb_fwd, lb_bwd=lb_bwd,
             hgrn_norm_g=hgrn_norm_g,
             w_br_a=w_br_a.astype(BF16), w_br_b=w_br_b.astype(BF16), w_br_c=w_br_c.astype(BF16),
             w_out=w_out.astype(BF16), w_router_t=jnp.swapaxes(w_router, 1, 2), b_router=b_router,
             w_gate=w_gate, w_up=w_up, w_down=w_down,
             w_sh_gate=w_sh_gate.astype(BF16), w_sh_up=w_sh_up.astype(BF16), w_sh_down=w_sh_down.astype(BF16),
             final_norm_g=final_norm_g.reshape(1, D_MODEL))

    c8 = jnp.zeros((8, D_MODEL), F32).at[0].set(c_ctx).at[1:1 + dec_batch].set(c)
    ada = _ada_call(c8, w_ada, b_ada)

    zero_states = [jnp.zeros((batch, 2, H_C, DK_C, DK_C), F32)] * DEPTH
    ctx_attend = lambda z, l: _ctx_attn_call(z, seq)
    h_ctx, keys, vals, sts = _stream(x_prompt.reshape(batch * seq, D_MODEL), p, ada, 0, batch, seq,
                                     ctx_attend, zero_states)
    y_prompt = h_ctx.reshape(batch, seq, D_MODEL)
    new_cache_k = jnp.stack([k.reshape(batch, seq, H_A, DH_A) for k in keys], axis=1)
    new_cache_v = jnp.stack([v.reshape(batch, seq, H_A, DH_A) for v in vals], axis=1)
    new_state = jnp.stack(sts, axis=1)

    rows = dec_seq // GRID_W
    ck = cache_k.reshape(dec_batch, DEPTH, past, D_A)
    cv = cache_v.reshape(dec_batch, DEPTH, past, D_A)
    lat_states = [state_hgrn[:, l].astype(F32) for l in range(DEPTH)]

    bias = _na_bias_tables(rpb.reshape(DEPTH * H_A, 2 * WIN_H - 1, 2 * WIN_W - 1), rows)

    def lat_attend(z, l):
        return _na_attn_call(z, ck[:, l], cv[:, l], bias, dec_batch, dec_seq, l)

    h_lat, _, _, _ = _stream(x_sample.reshape(dec_batch * dec_seq, D_MODEL), p, ada, 1, dec_batch, dec_seq,
                             lat_attend, lat_states)
    y_sample = h_lat.reshape(dec_batch, dec_seq, D_MODEL)
    return (y_prompt, y_sample, new_cache_k, new_cache_v, new_state)
```

```python
import functools

import numpy as np
import jax
import jax.numpy as jnp
from jax import lax
from jax.experimental import pallas as pl
from jax.experimental.pallas import tpu as pltpu

F32 = jnp.float32
BF16 = jnp.bfloat16

D_MODEL = 1024
DEPTH = 2
GRID_W = 64
H_A = 8
DH_A = 64
D_A = H_A * DH_A
WIN_H = 8
WIN_W = 16
D_CONV = 512
H_C = 4
DK_C = 128
CHUNK = 16
N_EXPERTS = 64
N_GROUPS = 8
GROUP_SIZE = N_EXPERTS // N_GROUPS
TOPK_GROUPS = 4
TOP_K = 8
D_EXPERT = 256
ROUTED_SCALE = 2.5
N_MOD = 6
EPS = 1e-6
NEG_BIG = -1e30
LOG_FLOOR = 1e-30

D_IN = 8704
COL_BLK = 512
N_COL_BLKS = D_IN // COL_BLK
SRC_GATE_BLK = 11
N_GATE_BLKS = 6
B_GA, B_GBR, B_GCR = 0, 2, 4
B_QA, B_KA, B_VA, B_U, B_GB, B_GC, B_QC, B_ZF, B_ZB, B_IC, B_OC = range(6, 17)

NA_QROWS = 4
NA_KROWS = 12
NA_TQ = NA_QROWS * GRID_W
NA_TK = NA_KROWS * GRID_W

TOK_TILE = 256
VMEM_LIMIT = 56 * 1024 * 1024


def _cparams(sem):
    return pltpu.CompilerParams(dimension_semantics=sem, vmem_limit_bytes=VMEM_LIMIT)


def _sigmoid(x):
    return 1.0 / (1.0 + jnp.exp(-x))


def _silu(x):
    return x * _sigmoid(x)


def _dot(a, b):
    return jnp.dot(a, b, preferred_element_type=F32)


def _dot_nt(a, b, precision=None):
    return lax.dot_general(a, b, (((1,), (1,)), ((), ())), preferred_element_type=F32,
                           precision=precision)


LANES = 128
CHUNKS = D_MODEL // LANES


def _store_token_tiles(ref, val):
    for rb in range(val.shape[0] // 8):
        for c in range(CHUNKS):
            ref[pl.ds(rb * 8 * CHUNKS + c, 8, stride=CHUNKS), :] = \
                val[rb * 8:(rb + 1) * 8, c * LANES:(c + 1) * LANES]


def _load_token_tiles(ref, rows):
    return jnp.concatenate([ref[pl.ds(c, rows, stride=CHUNKS), :] for c in range(CHUNKS)], axis=1)


def _ada_kernel(c_ref, w_ref, b_ref, o_ref):
    a = _silu(c_ref[...])
    o_ref[0] = _dot(a, w_ref[0]) + b_ref[0]


def _ada_call(c8, w_ada, b_ada):
    tn = 1536
    n_out = N_MOD * D_MODEL
    return pl.pallas_call(
        _ada_kernel,
        grid=(DEPTH, n_out // tn),
        in_specs=[
            pl.BlockSpec((8, D_MODEL), lambda l, j: (0, 0)),
            pl.BlockSpec((1, D_MODEL, tn), lambda l, j: (l, 0, j)),
            pl.BlockSpec((1, 1, tn), lambda l, j: (l, 0, j)),
        ],
        out_specs=pl.BlockSpec((1, 8, tn), lambda l, j: (l, 0, j)),
        out_shape=jax.ShapeDtypeStruct((DEPTH, 8, n_out), F32),
        compiler_params=_cparams(("arbitrary", "arbitrary")),
        name="ada",
    )(c8, w_ada, b_ada.reshape(DEPTH, 1, n_out))


def _rms(x):
    return x * lax.rsqrt(jnp.mean(x * x, axis=-1, keepdims=True) + EPS)


def _norm_mod_kernel(h_ref, g_ref, sc_ref, sh_ref, o_ref):
    y = _rms(h_ref[...]) * g_ref[...]
    o_ref[...] = (y * (1.0 + sc_ref[0]) + sh_ref[0]).astype(o_ref.dtype)


def _ada_spec(mod_idx, row0, tiles_per_row):
    return pl.BlockSpec((1, 1, D_MODEL), lambda i: (row0 + i // tiles_per_row, 0, mod_idx))


def _norm_mod_call(h, g, ada, sc_idx, sh_idx, row0, tiles_per_row):
    m = h.shape[0]
    return pl.pallas_call(
        _norm_mod_kernel,
        grid=(m // TOK_TILE,),
        in_specs=[
            pl.BlockSpec((TOK_TILE, D_MODEL), lambda i: (i, 0)),
            pl.BlockSpec((1, D_MODEL), lambda i: (0, 0)),
            _ada_spec(sc_idx, row0, tiles_per_row),
            _ada_spec(sh_idx, row0, tiles_per_row),
        ],
        out_specs=pl.BlockSpec((TOK_TILE, D_MODEL), lambda i: (i, 0)),
        out_shape=jax.ShapeDtypeStruct((m, D_MODEL), BF16),
        compiler_params=_cparams(("arbitrary",)),
        name="norm_mod",
    )(h, g, ada, ada)


IN_PROJ_ROWS = 1024


def _in_proj_kernel(a_ref, w_ref, o_ref):
    w = w_ref[...].astype(BF16)
    for r in range(a_ref.shape[0] // IN_PROJ_ROWS):
        rows = slice(r * IN_PROJ_ROWS, (r + 1) * IN_PROJ_ROWS)
        o_ref[rows, :] = _dot(a_ref[rows, :], w)


def _src_col_blk(j):
    return jnp.where(j < N_GATE_BLKS, j + SRC_GATE_BLK, j - N_GATE_BLKS)


def _in_proj_call(n, w_in, layer):
    m = n.shape[0]
    return pl.pallas_call(
        _in_proj_kernel,
        grid=(N_COL_BLKS,),
        in_specs=[
            pl.BlockSpec((m, D_MODEL), lambda j: (0, 0)),
            pl.BlockSpec((None, D_MODEL, COL_BLK), lambda j: (layer, 0, _src_col_blk(j))),
        ],
        out_specs=pl.BlockSpec((m, COL_BLK), lambda j: (0, j)),
        out_shape=jax.ShapeDtypeStruct((m, D_IN), F32),
        compiler_params=_cparams(("arbitrary",)),
        name="in_proj",
    )(n, w_in)


def _softmax_pv(s_list, v_list):
    m = s_list[0].max(axis=-1, keepdims=True)
    for s in s_list[1:]:
        m = jnp.maximum(m, s.max(axis=-1, keepdims=True))
    num = None
    den = None
    for s, v in zip(s_list, v_list):
        p = jnp.exp(s - m)
        d = p.sum(axis=-1, keepdims=True)
        o = _dot(p.astype(BF16), v.astype(BF16))
        num = o if num is None else num + o
        den = d if den is None else den + d
    return num / den


assert DH_A == 4 ** (DH_A.bit_length() // 2)


def _ctx_attn_kernel(q_ref, k_ref, v_ref, o_ref):
    scale = DH_A ** -0.5
    for h in range(H_A):
        sl = slice(h * DH_A, (h + 1) * DH_A)
        s = _dot_nt((q_ref[:, sl] * scale).astype(BF16), k_ref[:, sl].astype(BF16))
        o_ref[:, sl] = _softmax_pv([s], [v_ref[:, sl]])


def _ctx_attn_call(z, seq):
    m = z.shape[0]
    spec = lambda blk: pl.BlockSpec((seq, D_A), lambda b: (b, blk))
    return pl.pallas_call(
        _ctx_attn_kernel,
        grid=(m // seq,),
        in_specs=[spec(B_QA), spec(B_KA), spec(B_VA)],
        out_specs=pl.BlockSpec((seq, D_A), lambda b: (b, 0)),
        out_shape=jax.ShapeDtypeStruct((m, D_A), F32),
        compiler_params=_cparams(("arbitrary",)),
        name="ctx_attn",
    )(z, z, z)


def _na_key_row0(rb, rows):
    return jnp.clip(NA_QROWS * rb - (NA_KROWS - WIN_H) , 0, rows - NA_KROWS)


def _na_attn_kernel(q_ref, k_ref, v_ref, ck_ref, cv_ref, bias_ref, o_ref, *, rows):
    scale = DH_A ** -0.5
    rb = pl.program_id(1)
    k0 = pl.multiple_of(_na_key_row0(rb, rows) * GRID_W, GRID_W)
    for h in range(H_A):
        sl = slice(h * DH_A, (h + 1) * DH_A)
        q = (q_ref[:, sl] * scale).astype(BF16)
        kw = k_ref[pl.ds(k0, NA_TK), sl]
        vw = v_ref[pl.ds(k0, NA_TK), sl]
        s_lat = _dot_nt(q, kw.astype(BF16)) + bias_ref[h]
        s_ctx = _dot_nt(q, ck_ref[:, sl].astype(BF16))
        o_ref[:, sl] = _softmax_pv([s_lat, s_ctx], [vw, cv_ref[:, sl]])


def _na_bias_pattern(rb, n_rb):
    return jnp.where(rb == 0, 0, jnp.where(rb == n_rb - 1, 2, 1))


def _na_bias_tables(rpb_l, rows):
    n_heads = rpb_l.shape[0]
    qc = np.arange(GRID_W)
    q_start = np.clip(qc - WIN_W // 2, 0, GRID_W - WIN_W)
    kc = np.arange(GRID_W)
    valid_c = (kc[None, :] >= q_start[:, None]) & (kc[None, :] < q_start[:, None] + WIN_W)
    pad = GRID_W - WIN_W
    rpb_pad = jnp.pad(rpb_l.astype(F32), ((0, 0), (0, 0), (pad, pad)))
    col_tab = jnp.stack([rpb_pad[:, :, GRID_W - 1 - c:2 * GRID_W - 1 - c] for c in range(GRID_W)], axis=2)
    col_tab = jnp.where(valid_c, col_tab, NEG_BIG)
    col_tab = jnp.concatenate([col_tab, jnp.full((n_heads, 1, GRID_W, GRID_W), NEG_BIG, F32)], axis=1)
    n_tab = col_tab.shape[1]
    return pl.pallas_call(
        functools.partial(_na_bias_kernel, rows=rows),
        grid=(3, n_heads),
        in_specs=[pl.BlockSpec((None, n_tab, GRID_W, GRID_W), lambda p, h: (h, 0, 0, 0))],
        out_specs=pl.BlockSpec((None, None, NA_TQ, NA_TK), lambda p, h: (p, h, 0, 0)),
        out_shape=jax.ShapeDtypeStruct((3, n_heads, NA_TQ, NA_TK), F32),
        compiler_params=_cparams(("arbitrary", "arbitrary")),
        name="na_bias",
    )(col_tab)


def _na_bias_kernel(t_ref, o_ref, *, rows):
    n_rb = rows // NA_QROWS
    p = pl.program_id(0)
    rb = jnp.where(p == 0, 0, jnp.where(p == 1, 1, n_rb - 1))
    k_row0 = _na_key_row0(rb, rows)
    masked = t_ref.shape[0] - 1
    for i in range(NA_QROWS):
        r = NA_QROWS * rb + i
        w0 = jnp.clip(r - WIN_H // 2, 0, rows - WIN_H)
        for j in range(NA_KROWS):
            kr = k_row0 + j
            in_window = (kr >= w0) & (kr < w0 + WIN_H)
            tab = jnp.where(in_window, kr - r + WIN_H - 1, masked)
            o_ref[i * GRID_W:(i + 1) * GRID_W, j * GRID_W:(j + 1) * GRID_W] = t_ref[tab]


def _na_attn_call(z, ck, cv, bias, bsz, seq, layer):
    rows = seq // GRID_W
    n_rb = rows // NA_QROWS
    m = z.shape[0]
    kv_spec = lambda blk: pl.BlockSpec((seq, D_A), lambda b, r: (b, blk))
    c_spec = pl.BlockSpec((None, ck.shape[1], D_A), lambda b, r: (b, 0, 0))
    return pl.pallas_call(
        functools.partial(_na_attn_kernel, rows=rows),
        grid=(bsz, n_rb),
        in_specs=[
            pl.BlockSpec((NA_TQ, D_A), lambda b, r: (b * n_rb + r, B_QA)),
            kv_spec(B_KA), kv_spec(B_VA), c_spec, c_spec,
            pl.BlockSpec((None, H_A, NA_TQ, NA_TK), lambda b, r: (_na_bias_pattern(r, n_rb), layer, 0, 0)),
        ],
        out_specs=pl.BlockSpec((NA_TQ, D_A), lambda b, r: (b * n_rb + r, 0)),
        out_shape=jax.ShapeDtypeStruct((m, D_A), F32),
        compiler_params=_cparams(("arbitrary", "arbitrary")),
        name="na_attn",
    )(z, z, z, ck, cv, bias)


def _log1p(x):
    return jnp.log1p(x)


def _hgrn_gates(z, lb):
    log_sig = jnp.minimum(z, 0.0) - _log1p(jnp.exp(-jnp.abs(z)))
    a = _log1p(-lb) + log_sig
    b = jnp.log(jnp.maximum(lb, LOG_FLOOR))
    logf = jnp.maximum(a, b) + _log1p(jnp.exp(-jnp.abs(a - b)))
    key = (1.0 - lb) * _sigmoid(-z)
    return logf, key


def _lower_bound(p_ref, layer):
    p = p_ref[...]
    e = jnp.exp(p - jnp.max(p, axis=0, keepdims=True))
    sm = e / jnp.sum(e, axis=0, keepdims=True)
    lb = jnp.zeros((1, DK_C), F32)
    for j in range(1, layer + 1):
        lb = lb + sm[j:j + 1]
    return lb


def _chunk_cumsum(x, seq, reverse):
    pos = lax.broadcasted_iota(jnp.int32, x.shape, 0) % CHUNK
    s = 1
    while s < CHUNK:
        if reverse:
            x = x + jnp.where(pos < CHUNK - s, pltpu.roll(x, seq - s, axis=0), 0.0)
        else:
            x = x + jnp.where(pos >= s, pltpu.roll(x, s, axis=0), 0.0)
        s *= 2
    return x


HALF = CHUNK // 2
LOG2_E = 1.4426950408889634
HGRN_UNROLL = 8


def _hgrn_chunk(q_ref, v_ref, b_s, k_s, o_s, r0, st, reverse):
    b = b_s[pl.ds(r0, CHUNK), :]
    q = q_ref[pl.ds(r0, CHUNK), :]
    k = k_s[pl.ds(r0, CHUNK), :]
    v = v_ref[pl.ds(r0, CHUNK), :]
    o_inter = _dot_nt(q * jnp.exp2(b), st)
    t_iota = lax.broadcasted_iota(jnp.int32, (HALF, 1), 0)
    o_half = [None, None]
    for s in range(CHUNK):
        b_row = b_s[pl.ds(r0 + s, 1), :]
        k_row = k_s[pl.ds(r0 + s, 1), :]
        v_row = v_ref[pl.ds(r0 + s, 1), :]
        for half in range(2):
            t0 = half * HALF
            if (t0 + HALF - 1 < s) if not reverse else (t0 > s):
                continue
            sl = slice(t0, t0 + HALF)
            decay = jnp.exp2(jnp.minimum(b[sl] - b_row, 0.0))
            col = jnp.sum(q[sl] * (k_row * decay), axis=-1, keepdims=True)
            if not ((t0 >= s) if not reverse else (t0 + HALF - 1 <= s)):
                keep = (t_iota + t0 >= s) if not reverse else (t_iota + t0 <= s)
                col = jnp.where(keep, col, 0.0)
            o_half[half] = col * v_row if o_half[half] is None else o_half[half] + col * v_row
    o_s[pl.ds(r0, CHUNK), :] = jnp.concatenate(o_half, axis=0) + o_inter
    edge = 0 if reverse else CHUNK - 1
    b_edge = b[edge:edge + 1, :]
    kh = k * jnp.exp2(b_edge - b)
    kv_t = lax.dot_general(v, kh, (((0,), (0,)), ((), ())), preferred_element_type=F32)
    return st * jnp.exp2(b_edge) + kv_t


def _hgrn_kernel(q_ref, zf_ref, zb_ref, i_ref, og_ref, lbf_ref, lbb_ref, g_ref, s0_ref,
                 o_ref, sfin_ref, of_s, ob_s, bf_s, bb_s, kf_s, kb_s, *, seq, layer):
    n_chunks = seq // CHUNK
    logf, key = _hgrn_gates(zf_ref[...], _lower_bound(lbf_ref, layer))
    bf_s[...] = _chunk_cumsum(logf * LOG2_E, seq, False)
    kf_s[...] = key
    logf, key = _hgrn_gates(zb_ref[...], _lower_bound(lbb_ref, layer))
    bb_s[...] = _chunk_cumsum(logf * LOG2_E, seq, True)
    kb_s[...] = key

    def body(it, carry):
        st_f, st_b = carry
        for u in range(HGRN_UNROLL):
            n = it * HGRN_UNROLL + u
            st_f = _hgrn_chunk(q_ref, i_ref, bf_s, kf_s, of_s, pl.multiple_of(n * CHUNK, CHUNK), st_f, False)
            st_b = _hgrn_chunk(q_ref, i_ref, bb_s, kb_s, ob_s,
                               pl.multiple_of((n_chunks - 1 - n) * CHUNK, CHUNK), st_b, True)
        return st_f, st_b

    st_f, st_b = lax.fori_loop(0, n_chunks // HGRN_UNROLL, body, (s0_ref[0].T, s0_ref[1].T))
    sfin_ref[0] = st_f.T
    sfin_ref[1] = st_b.T
    o_ref[...] = _rms(of_s[...] + ob_s[...]) * g_ref[...] * _silu(og_ref[...])


def _hgrn_call(z, lb_fwd, lb_bwd, norm_g, s0, bsz, seq, layer):
    m = z.shape[0]
    w = DK_C
    per = COL_BLK // w
    zspec = lambda blk: pl.BlockSpec((seq, w), lambda b, h: (b, blk * per + h))
    st_spec = pl.BlockSpec((None, 2, None, DK_C, DK_C), lambda b, h: (b, 0, h, 0, 0))
    return pl.pallas_call(
        functools.partial(_hgrn_kernel, seq=seq, layer=layer),
        grid=(bsz, H_C),
        in_specs=[
            zspec(B_QC), zspec(B_ZF), zspec(B_ZB), zspec(B_IC), zspec(B_OC),
            pl.BlockSpec((DEPTH, w), lambda b, h: (0, h)),
            pl.BlockSpec((DEPTH, w), lambda b, h: (0, h)),
            pl.BlockSpec((1, w), lambda b, h: (0, h)),
            st_spec,
        ],
        out_specs=[pl.BlockSpec((seq, w), lambda b, h: (b, h)), st_spec],
        out_shape=[jax.ShapeDtypeStruct((m, H_C * w), F32),
                   jax.ShapeDtypeStruct((bsz, 2, H_C, DK_C, DK_C), F32)],
        scratch_shapes=[pltpu.VMEM((seq, w), F32)] * 6,
        compiler_params=_cparams(("arbitrary", "arbitrary")),
        name="hgrn",
    )(z, z, z, z, z, lb_fwd, lb_bwd, norm_g, s0)


def _mix_kernel(oa_ref, oc_ref, u_ref, gb_ref, gc_ref, up_ref, gcp_ref, un_ref, gcn_ref,
                ga_ref, gbr_ref, gcr_ref, h_ref, wconv_ref, wa_ref, wb_ref, wc_ref, wout_ref,
                wrt_ref, g1_ref, n2g_ref, sc2_ref, sh2_ref,
                hout_ref, n2_ref, lgt_ref, *, seq):
    tm = u_ref.shape[0]
    zc = gc_ref[...] * u_ref[...]
    row = lax.broadcasted_iota(jnp.int32, zc.shape, 0)
    pos = (pl.program_id(0) * tm + row) % seq
    zp = jnp.where(row == 0, gcp_ref[7:8, :] * up_ref[7:8, :], pltpu.roll(zc, 1, axis=0))
    zn = jnp.where(row == tm - 1, gcn_ref[0:1, :] * un_ref[0:1, :], pltpu.roll(zc, tm - 1, axis=0))
    zp = jnp.where(pos == 0, 0.0, zp)
    zn = jnp.where(pos == seq - 1, 0.0, zn)
    w = wconv_ref[...]
    ob = gb_ref[...] * (w[0:1] * zp + w[1:2] * zc + w[2:3] * zn)
    ya = _dot(oa_ref[...].astype(BF16), wa_ref[...])
    yb = _dot(ob.astype(BF16), wb_ref[...])
    yc = _dot(oc_ref[...].astype(BF16), wc_ref[...])
    pre = _sigmoid(ga_ref[...]) * ya + _sigmoid(gbr_ref[...]) * yb + _sigmoid(gcr_ref[...]) * yc
    hn = h_ref[...] + g1_ref[0] * _dot(pre.astype(BF16), wout_ref[...])
    hout_ref[...] = hn
    n2 = _rms(hn) * n2g_ref[...] * (1.0 + sc2_ref[0]) + sh2_ref[0]
    _store_token_tiles(n2_ref, n2)
    lgt_ref[...] = _dot_nt(wrt_ref[...], n2, precision=lax.Precision.HIGHEST)


MIX_TILE = 512


def _mix_call(o_a, o_c, z, h, w_conv, wa, wb, wc, wout, wrt, n2g, ada, row0, toks_per_row, seq):
    m = h.shape[0]
    tm = MIX_TILE
    nt = m // tm
    tiles_per_row = toks_per_row // tm
    r8 = tm // 8
    zblk = lambda blk: pl.BlockSpec((tm, COL_BLK), lambda i: (i, blk))
    zprev = lambda blk: pl.BlockSpec((8, COL_BLK), lambda i: (jnp.maximum(i * r8 - 1, 0), blk))
    znext = lambda blk: pl.BlockSpec((8, COL_BLK), lambda i: (jnp.minimum((i + 1) * r8, m // 8 - 1), blk))
    zgate = lambda blk: pl.BlockSpec((tm, D_MODEL), lambda i: (i, blk // 2))
    full = lambda a: pl.BlockSpec(a.shape, lambda i: (0,) * a.ndim)
    tok = lambda wdt: pl.BlockSpec((tm, wdt), lambda i: (i, 0))
    return pl.pallas_call(
        functools.partial(_mix_kernel, seq=seq),
        grid=(nt,),
        in_specs=[
            tok(D_A), tok(D_A), zblk(B_U), zblk(B_GB), zblk(B_GC),
            zprev(B_U), zprev(B_GC), znext(B_U), znext(B_GC),
            zgate(B_GA), zgate(B_GBR), zgate(B_GCR), tok(D_MODEL),
            full(w_conv), full(wa), full(wb), full(wc), full(wout), full(wrt),
            _ada_spec(2, row0, tiles_per_row), full(n2g),
            _ada_spec(4, row0, tiles_per_row), _ada_spec(3, row0, tiles_per_row),
        ],
        out_specs=[tok(D_MODEL), pl.BlockSpec((tm * CHUNKS, LANES), lambda i: (i, 0)),
                   pl.BlockSpec((N_EXPERTS, tm), lambda i: (0, i))],
        out_shape=[jax.ShapeDtypeStruct((m, D_MODEL), F32),
                   jax.ShapeDtypeStruct((m * CHUNKS, LANES), F32),
                   jax.ShapeDtypeStruct((N_EXPERTS, m), F32)],
        compiler_params=_cparams(("arbitrary",)),
        name="mix",
    )(o_a, o_c, z, z, z, z, z, z, z, z, z, z, h, w_conv, wa, wb, wc, wout, wrt, ada, n2g, ada, ada)


def _first_max(x, axes_iota, size):
    m = x
    for ax in range(x.ndim - 1):
        m = jnp.max(m, axis=ax, keepdims=True)
    first = jnp.where(x == m, axes_iota, size)
    f = first
    for ax in range(x.ndim - 1):
        f = jnp.min(f, axis=ax, keepdims=True)
    return axes_iota == f, f


def _first_max_mask(x, axes_iota, size):
    return _first_max(x, axes_iota, size)[0]


def _routing_kernel(lgt_ref, bias_ref, wpick_ref, eidx_ref, rank_ref, cnt_ref, carry_s):
    tn = lgt_ref.shape[-1]

    @pl.when(pl.program_id(0) == 0)
    def _():
        carry_s[...] = jnp.zeros(carry_s.shape, F32)

    s = _sigmoid(lgt_ref[...])
    sel = (s + bias_ref[...]).reshape(N_GROUPS, GROUP_SIZE, tn)
    s = s.reshape(N_GROUPS, GROUP_SIZE, tn)
    neg_inf = -jnp.inf
    mem_iota = lax.broadcasted_iota(jnp.int32, sel.shape, 1)
    m1 = jnp.max(sel, axis=1, keepdims=True)
    first = jnp.min(jnp.where(sel == m1, mem_iota, GROUP_SIZE), axis=1, keepdims=True)
    m2 = jnp.max(jnp.where(mem_iota == first, neg_inf, sel), axis=1, keepdims=True)
    grp = (m1 + m2).reshape(N_GROUPS, tn)
    g_iota = lax.broadcasted_iota(jnp.int32, grp.shape, 0)
    gmask = jnp.zeros(grp.shape, jnp.bool_)
    for _ in range(TOPK_GROUPS):
        pick = _first_max_mask(grp, g_iota, N_GROUPS)
        gmask = gmask | pick
        grp = jnp.where(pick, neg_inf, grp)
    cand = jnp.where(gmask.reshape(N_GROUPS, 1, tn), sel, NEG_BIG)
    e_iota = lax.broadcasted_iota(jnp.int32, sel.shape, 0) * GROUP_SIZE + mem_iota
    chosen = jnp.zeros(sel.shape, jnp.bool_)
    picked = []
    for _ in range(TOP_K):
        pick, idx = _first_max(cand, e_iota, N_EXPERTS)
        chosen = chosen | pick
        cand = jnp.where(pick, neg_inf, cand)
        picked.append(idx)
    w = jnp.where(chosen, s, 0.0)
    tot = jnp.sum(jnp.sum(w, axis=1, keepdims=True), axis=0, keepdims=True)
    w = w / tot * ROUTED_SCALE
    chosen_f = jnp.where(chosen, 1.0, 0.0).reshape(N_EXPERTS, tn)
    earlier = (lax.broadcasted_iota(jnp.int32, (tn, tn), 0) < lax.broadcasted_iota(jnp.int32, (tn, tn), 1))
    rank = _dot(chosen_f.astype(BF16), jnp.where(earlier, 1.0, 0.0).astype(BF16)) + carry_s[:, 0:1]
    rank = rank.reshape(N_GROUPS, GROUP_SIZE, tn)
    of_pick = lambda v, idx: jnp.sum(jnp.sum(jnp.where(e_iota == idx, v, 0.0), axis=1, keepdims=True),
                                     axis=0, keepdims=True).reshape(1, tn)
    eidx_ref[...] = jnp.concatenate([idx.reshape(1, tn) for idx in picked], axis=0)
    rank_ref[...] = jnp.concatenate([of_pick(rank, idx) for idx in picked], axis=0).astype(jnp.int32)
    wpick_ref[...] = jnp.concatenate([of_pick(w, idx) for idx in picked], axis=0)
    carry_s[...] = carry_s[...] + jnp.sum(chosen_f, axis=1, keepdims=True)
    cnt_ref[...] = carry_s[...]


def _routing_call(lgt, b_router_l):
    m = lgt.shape[1]
    tn = 256
    pick_spec = pl.BlockSpec((TOP_K, tn), lambda i: (0, i))
    cnt_spec = pl.BlockSpec((N_EXPERTS, 128), lambda i: (0, 0))
    return pl.pallas_call(
        _routing_kernel,
        grid=(m // tn,),
        in_specs=[pl.BlockSpec((N_EXPERTS, tn), lambda i: (0, i)),
                  pl.BlockSpec((N_EXPERTS, tn), lambda i: (0, 0))],
        out_specs=[pick_spec, pick_spec, pick_spec, cnt_spec],
        out_shape=[jax.ShapeDtypeStruct((TOP_K, m), F32),
                   jax.ShapeDtypeStruct((TOP_K, m), jnp.int32),
                   jax.ShapeDtypeStruct((TOP_K, m), jnp.int32),
                   jax.ShapeDtypeStruct((N_EXPERTS, 128), F32)],
        scratch_shapes=[pltpu.VMEM((N_EXPERTS, 128), F32)],
        compiler_params=_cparams(("arbitrary",)),
        name="routing",
    )(lgt, jnp.broadcast_to(b_router_l[:, None], (N_EXPERTS, tn)))


MOE_TM = 256
ROW_GROUP = 16
assert TOP_K == CHUNKS and TOP_K & (TOP_K - 1) == 0


def _token_row(idx):
    return pl.multiple_of(idx & ~(TOP_K - 1), CHUNKS)


def _dispatch_plan(eidx, rank, cnt, n_tiles_max):
    counts = cnt[:, 0].astype(jnp.int32)
    tiles = (counts + MOE_TM - 1) // MOE_TM
    tile_end = jnp.cumsum(tiles)
    tile_start = tile_end - tiles
    n_tiles = tile_end[-1]
    experts = jnp.arange(N_EXPERTS, dtype=jnp.int32)
    base = jnp.sum(jnp.where(eidx[:, :, None] == experts, tile_start * MOE_TM, 0), axis=-1)
    pos = (rank + base).T.reshape(-1)
    ti = jnp.arange(n_tiles_max, dtype=jnp.int32)
    tile_e = jnp.sum((ti[:, None] >= tile_end[None, :]).astype(jnp.int32), axis=1)
    last_e = jnp.sum(jnp.where(ti == n_tiles - 1, tile_e, 0))
    tile_e = jnp.where(ti < n_tiles, tile_e, last_e).astype(jnp.int32)
    first = ((ti == 0) | (tile_e != jnp.roll(tile_e, 1))) & (ti < n_tiles)
    slot = (jnp.cumsum(first.astype(jnp.int32)) - 1) % 2
    next_tile = jnp.sum(jnp.where(tile_e[:, None] == experts, tile_end, 0), axis=1)
    next_e = jnp.sum(jnp.where(next_tile[:, None] == ti[None, :], tile_e[None, :], 0), axis=1)
    next_e = jnp.where(next_tile < n_tiles, next_e, -1)
    i32 = lambda v: v.astype(jnp.int32)
    return pos, (tile_e, i32(first), i32(next_e), i32(slot), i32(n_tiles).reshape(1))


def _moe_step(tile_g, tile_s, rowidx_s, wpick_ref, x_v, acc_v, g_w, g_r, y_w, y_r, wg_bf, wu_bf, wd_bf):
    row_g = tile_g * MOE_TM
    for r in range(MOE_TM):
        g_w[pl.ds(r * CHUNKS, CHUNKS), :] = x_v[pl.ds(_token_row(rowidx_s[row_g + r]), CHUNKS), :]

    x = _load_token_tiles(g_r, MOE_TM).astype(BF16)
    hid = _silu(_dot(x, wg_bf[...])) * _dot(x, wu_bf[...])
    y = _dot(hid.astype(BF16), wd_bf[...])
    for rb in range(MOE_TM // 8):
        for c in range(CHUNKS):
            y_w[pl.ds((rb * CHUNKS + c) * 8, 8), :] = y[rb * 8:(rb + 1) * 8, c * LANES:(c + 1) * LANES]

    row_s = tile_s * MOE_TM
    for j in range(MOE_TM // ROW_GROUP):
        slots, rows_v = [], []
        for u in range(ROW_GROUP):
            r = j * ROW_GROUP + u
            idx = rowidx_s[row_s + r]
            slots.append(_token_row(idx))
            rows_v.append(y_r[pl.ds((r // 8) * (8 * CHUNKS) + r % 8, CHUNKS, stride=8), :] * wpick_ref[idx])
        olds = [acc_v[pl.ds(s, CHUNKS), :] for s in slots]
        for s, o, v in zip(slots, olds, rows_v):
            acc_v[pl.ds(s, CHUNKS), :] = o + v


def _moe_kernel(te_ref, first_ref, next_ref, slot_ref, nt_ref, pos_ref, wpick_ref, x_hbm, fill_hbm,
                wg_hbm, wu_hbm, wd_hbm, out_hbm,
                rowidx_s, x_v, acc_v, g0_s, g1_s, y0_s, y1_s, wg_f, wu_f, wd_f, wg_bf, wu_bf, wd_bf, sem, wsem,
                *, layer):
    def weight_copies(e, slot):
        return [pltpu.make_async_copy(hbm.at[layer, e], buf.at[slot], wsem.at[slot, k])
                for k, (hbm, buf) in enumerate(((wg_hbm, wg_f), (wu_hbm, wu_f), (wd_hbm, wd_f)))]

    s = pl.program_id(0)
    n_tok = x_hbm.shape[0] // CHUNKS
    n_tiles_max = te_ref.shape[0]
    zero_blk = 512

    @pl.when(s == 0)
    def _prologue():
        x_copy = pltpu.make_async_copy(x_hbm, x_v.at[pl.ds(0, n_tok * CHUNKS)], sem.at[0])
        fill_copy = pltpu.make_async_copy(fill_hbm, rowidx_s, sem.at[1])
        x_copy.start()
        fill_copy.start()
        for cp in weight_copies(te_ref[0], 0):
            cp.start()

        def zero(j, carry):
            acc_v[pl.ds(pl.multiple_of(j * zero_blk, zero_blk), zero_blk), :] = jnp.zeros((zero_blk, LANES), F32)
            return carry

        lax.fori_loop(0, n_tok * CHUNKS // zero_blk, zero, 0)
        acc_v[pl.ds(n_tok * CHUNKS, CHUNKS), :] = jnp.zeros((CHUNKS, LANES), F32)
        x_v[pl.ds(n_tok * CHUNKS, CHUNKS), :] = jnp.zeros((CHUNKS, LANES), F32)
        g1_s[...] = jnp.zeros(g1_s.shape, F32)
        y0_s[...] = jnp.zeros(y0_s.shape, F32)
        fill_copy.wait()

        def invert(j, carry):
            for u in range(ROW_GROUP):
                idx = j * ROW_GROUP + u
                rowidx_s[pos_ref[idx]] = idx
            return carry

        lax.fori_loop(0, n_tok * TOP_K // ROW_GROUP, invert, 0)
        x_copy.wait()

    active = s < nt_ref[0] + 2
    ffn_tile = jnp.clip(s - 1, 0, n_tiles_max - 1)

    @pl.when(active & (first_ref[ffn_tile] == 1) & (s != 1))
    def _next_expert():
        slot = slot_ref[ffn_tile]
        for cp in weight_copies(te_ref[ffn_tile], slot):
            cp.wait()
        wg_bf[...] = wg_f[slot].astype(BF16)
        wu_bf[...] = wu_f[slot].astype(BF16)
        wd_bf[...] = wd_f[slot].astype(BF16)

        @pl.when(next_ref[ffn_tile] >= 0)
        def _prefetch():
            for cp in weight_copies(next_ref[ffn_tile], 1 - slot):
                cp.start()

    tile_g = jnp.minimum(s, n_tiles_max - 1)
    tile_s = jnp.maximum(s - 2, 0)
    common = (rowidx_s, wpick_ref, x_v, acc_v)
    weights = (wg_bf, wu_bf, wd_bf)

    @pl.when(active & (s % 2 == 0))
    def _even():
        _moe_step(tile_g, tile_s, *common, g0_s, g1_s, y1_s, y0_s, *weights)

    @pl.when(active & (s % 2 == 1))
    def _odd():
        _moe_step(tile_g, tile_s, *common, g1_s, g0_s, y0_s, y1_s, *weights)

    @pl.when(s == pl.num_programs(0) - 1)
    def _epilogue():
        out_copy = pltpu.make_async_copy(acc_v.at[pl.ds(0, n_tok * CHUNKS)], out_hbm, sem.at[2])
        out_copy.start()
        out_copy.wait()


def _moe_call(x, wpick, eidx, rank, cnt, w_gate, w_up, w_down, layer):
    m = x.shape[0] // CHUNKS
    n_rows = TOP_K * m
    n_tiles_max = n_rows // MOE_TM + N_EXPERTS
    n_slots = n_tiles_max * MOE_TM
    pos, tables = _dispatch_plan(eidx, rank, cnt, n_tiles_max)
    wflat = jnp.concatenate([wpick.T.reshape(-1), jnp.zeros((TOP_K,), F32)])
    anyspec = pl.BlockSpec(memory_space=pl.ANY)
    smem = pl.BlockSpec(memory_space=pltpu.SMEM)
    tile_buf = pltpu.VMEM((MOE_TM * CHUNKS, LANES), F32)
    grid_spec = pltpu.PrefetchScalarGridSpec(
        num_scalar_prefetch=len(tables),
        grid=(n_tiles_max + 2,),
        in_specs=[smem, smem, anyspec, anyspec, anyspec, anyspec, anyspec],
        out_specs=anyspec,
        scratch_shapes=[
            pltpu.SMEM((n_slots,), jnp.int32),
            pltpu.VMEM(((m + 1) * CHUNKS, LANES), F32),
            pltpu.VMEM(((m + 1) * CHUNKS, LANES), F32),
            tile_buf, tile_buf, tile_buf, tile_buf,
            pltpu.VMEM((2, D_MODEL, D_EXPERT), F32),
            pltpu.VMEM((2, D_MODEL, D_EXPERT), F32),
            pltpu.VMEM((2, D_EXPERT, D_MODEL), F32),
            pltpu.VMEM((D_MODEL, D_EXPERT), BF16),
            pltpu.VMEM((D_MODEL, D_EXPERT), BF16),
            pltpu.VMEM((D_EXPERT, D_MODEL), BF16),
            pltpu.SemaphoreType.DMA((3,)),
            pltpu.SemaphoreType.DMA((2, 3)),
        ],
    )
    return pl.pallas_call(
        functools.partial(_moe_kernel, layer=layer),
        grid_spec=grid_spec,
        out_shape=jax.ShapeDtypeStruct((m * CHUNKS, LANES), F32),
        compiler_params=_cparams(("arbitrary",)),
        name="moe",
    )(*tables, pos, wflat, x, jnp.full((n_slots,), n_rows, jnp.int32), w_gate, w_up, w_down)


def _moe_finish_kernel(x_ref, routed_ref, h_ref, wsg_ref, wsu_ref, wsd_ref, g2_ref, gn_ref, scn_ref, shn_ref,
                       *out_refs, last):
    rows = h_ref.shape[0]
    x = _load_token_tiles(x_ref, rows).astype(BF16)
    hid = _silu(_dot(x, wsg_ref[...])) * _dot(x, wsu_ref[...])
    shared = _dot(hid.astype(BF16), wsd_ref[...])
    h = h_ref[...] + g2_ref[0] * (_load_token_tiles(routed_ref, rows) + shared)
    y = _rms(h) * gn_ref[...]
    if last:
        (y_ref,) = out_refs
        y_ref[...] = y
    else:
        h_out_ref, n_ref = out_refs
        h_out_ref[...] = h
        n_ref[...] = (y * (1.0 + scn_ref[0]) + shn_ref[0]).astype(BF16)


def _moe_finish_call(x, routed, h, wsg, wsu, wsd, ada, row0, tiles_per_row, norm_g, ada_next, last):
    m = h.shape[0]
    tok = pl.BlockSpec((TOK_TILE, D_MODEL), lambda i: (i, 0))
    tiles = pl.BlockSpec((TOK_TILE * CHUNKS, LANES), lambda i: (i, 0))
    full = lambda a: pl.BlockSpec(a.shape, lambda i: (0,) * a.ndim)
    out_f32 = jax.ShapeDtypeStruct((m, D_MODEL), F32)
    return pl.pallas_call(
        functools.partial(_moe_finish_kernel, last=last),
        grid=(m // TOK_TILE,),
        in_specs=[tiles, tiles, tok, full(wsg), full(wsu), full(wsd), _ada_spec(5, row0, tiles_per_row),
                  full(norm_g), _ada_spec(1, row0, tiles_per_row), _ada_spec(0, row0, tiles_per_row)],
        out_specs=[tok] if last else [tok, tok],
        out_shape=[out_f32] if last else [out_f32, jax.ShapeDtypeStruct((m, D_MODEL), BF16)],
        compiler_params=_cparams(("arbitrary",)),
        name="moe_finish",
    )(x, routed, h, wsg, wsu, wsd, ada, norm_g, ada_next, ada_next)


def _stream(h, p, ada, row0, bsz, seq, attend, states):
    toks_per_row = seq if row0 > 0 else bsz * seq
    tiles_per_row = toks_per_row // TOK_TILE
    keys, vals, sts = [], [], []
    ada_rows = [ada[l].reshape(8, 1, N_MOD * D_MODEL) for l in range(DEPTH)]
    n = _norm_mod_call(h, p['norm1_g'][0:1], ada_rows[0], 1, 0, row0, tiles_per_row)
    for l in range(DEPTH):
        ada_l = ada_rows[l]
        last = l == DEPTH - 1
        z = _in_proj_call(n, p['w_in'], l)
        o_a = attend(z, l)
        o_c, st = _hgrn_call(z, p['lb_fwd'], p['lb_bwd'], p['hgrn_norm_g'][l:l + 1], states[l], bsz, seq, l)
        h, n2, lgt = _mix_call(o_a, o_c, z, h, p['w_conv'][l], p['w_br_a'][l], p['w_br_b'][l], p['w_br_c'][l],
                               p['w_out'][l], p['w_router_t'][l], p['norm2_g'][l:l + 1], ada_l,
                               row0, toks_per_row, seq)
        wpick, eidx, rank, cnt = _routing_call(lgt, p['b_router'][l])
        routed = _moe_call(n2, wpick, eidx, rank, cnt, p['w_gate'], p['w_up'], p['w_down'], l)
        outs = _moe_finish_call(n2, routed, h, p['w_sh_gate'][l], p['w_sh_up'][l], p['w_sh_down'][l],
                                ada_l, row0, tiles_per_row,
                                p['final_norm_g'] if last else p['norm1_g'][l + 1:l + 2],
                                ada_l if last else ada_rows[l + 1], last)
        h, n = (outs[0], None) if last else outs
        keys.append(z[:, B_KA * COL_BLK:(B_KA + 1) * COL_BLK])
        vals.append(z[:, B_VA * COL_BLK:(B_VA + 1) * COL_BLK])
        sts.append(st)
    return h, keys, vals, sts


def kernel(x_prompt, x_sample, cache_k, cache_v, state_hgrn, c, c_ctx, norm1_g, norm2_g, w_ada, b_ada,
           w_in, rpb, w_conv, lb_fwd, lb_bwd, hgrn_norm_g, w_br_a, w_br_b, w_br_c, w_out, w_router,
           b_router, w_gate, w_up, w_down, w_sh_gate, w_sh_up, w_sh_down, final_norm_g):
    batch, seq, _ = x_prompt.shape
    dec_batch, dec_seq, _ = x_sample.shape
    past = cache_k.shape[2]
    p = dict(norm1_g=norm1_g, norm2_g=norm2_g, w_in=w_in, w_conv=w_conv, lb_fwd=lb_fwd, lb_bwd=lb_bwd,
             hgrn_norm_g=hgrn_norm_g,
             w_br_a=w_br_a.astype(BF16), w_br_b=w_br_b.astype(BF16), w_br_c=w_br_c.astype(BF16),
             w_out=w_out.astype(BF16), w_router_t=jnp.swapaxes(w_router, 1, 2), b_router=b_router,
             w_gate=w_gate, w_up=w_up, w_down=w_down,
             w_sh_gate=w_sh_gate.astype(BF16), w_sh_up=w_sh_up.astype(BF16), w_sh_down=w_sh_down.astype(BF16),
             final_norm_g=final_norm_g.reshape(1, D_MODEL))

    c8 = jnp.zeros((8, D_MODEL), F32).at[0].set(c_ctx).at[1:1 + dec_batch].set(c)
    ada = _ada_call(c8, w_ada, b_ada)

    zero_states = [jnp.zeros((batch, 2, H_C, DK_C, DK_C), F32)] * DEPTH
    ctx_attend = lambda z, l: _ctx_attn_call(z, seq)
    h_ctx, keys, vals, sts = _stream(x_prompt.reshape(batch * seq, D_MODEL), p, ada, 0, batch, seq,
                                     ctx_attend, zero_states)
    y_prompt = h_ctx.reshape(batch, seq, D_MODEL)
    new_cache_k = jnp.stack([k.reshape(batch, seq, H_A, DH_A) for k in keys], axis=1)
    new_cache_v = jnp.stack([v.reshape(batch, seq, H_A, DH_A) for v in vals], axis=1)
    new_state = jnp.stack(sts, axis=1)

    rows = dec_seq // GRID_W
    ck = cache_k.reshape(dec_batch, DEPTH, past, D_A)
    cv = cache_v.reshape(dec_batch, DEPTH, past, D_A)
    lat_states = [state_hgrn[:, l].astype(F32) for l in range(DEPTH)]

    bias = _na_bias_tables(rpb.reshape(DEPTH * H_A, 2 * WIN_H - 1, 2 * WIN_W - 1), rows)

    def lat_attend(z, l):
        return _na_attn_call(z, ck[:, l], cv[:, l], bias, dec_batch, dec_seq, l)

    h_lat, _, _, _ = _stream(x_sample.reshape(dec_batch * dec_seq, D_MODEL), p, ada, 1, dec_batch, dec_seq,
                             lat_attend, lat_states)
    y_sample = h_lat.reshape(dec_batch, dec_seq, D_MODEL)
    return (y_prompt, y_sample, new_cache_k, new_cache_v, new_state)
```

```python
import functools

import numpy as np
import jax
import jax.numpy as jnp
from jax import lax
from jax.experimental import pallas as pl
from jax.experimental.pallas import tpu as pltpu

F32 = jnp.float32
BF16 = jnp.bfloat16

D_MODEL = 1024
DEPTH = 2
GRID_W = 64
H_A = 8
DH_A = 64
D_A = H_A * DH_A
WIN_H = 8
WIN_W = 16
D_CONV = 512
H_C = 4
DK_C = 128
CHUNK = 16
N_EXPERTS = 64
N_GROUPS = 8
GROUP_SIZE = N_EXPERTS // N_GROUPS
TOPK_GROUPS = 4
TOP_K = 8
D_EXPERT = 256
ROUTED_SCALE = 2.5
N_MOD = 6
EPS = 1e-6
NEG_BIG = -1e30
LOG_FLOOR = 1e-30

D_IN = 8704
COL_BLK = 512
N_COL_BLKS = D_IN // COL_BLK
SRC_GATE_BLK = 11
N_GATE_BLKS = 6
B_GA, B_GBR, B_GCR = 0, 2, 4
B_QA, B_KA, B_VA, B_U, B_GB, B_GC, B_QC, B_ZF, B_ZB, B_IC, B_OC = range(6, 17)

NA_QROWS = 4
NA_KROWS = 12
NA_TQ = NA_QROWS * GRID_W
NA_TK = NA_KROWS * GRID_W

TOK_TILE = 512
VMEM_LIMIT = 56 * 1024 * 1024


def _cparams(sem):
    return pltpu.CompilerParams(dimension_semantics=sem, vmem_limit_bytes=VMEM_LIMIT)


def _sigmoid(x):
    return 1.0 / (1.0 + jnp.exp(-x))


def _silu(x):
    return x * _sigmoid(x)


def _dot(a, b):
    return jnp.dot(a, b, preferred_element_type=F32)


def _dot_nt(a, b, precision=None):
    return lax.dot_general(a, b, (((1,), (1,)), ((), ())), preferred_element_type=F32,
                           precision=precision)


LANES = 128
CHUNKS = D_MODEL // LANES


def _store_token_tiles(ref, val):
    for rb in range(val.shape[0] // 8):
        for c in range(CHUNKS):
            ref[pl.ds(rb * 8 * CHUNKS + c, 8, stride=CHUNKS), :] = \
                val[rb * 8:(rb + 1) * 8, c * LANES:(c + 1) * LANES]


def _load_token_tiles(ref, rows):
    return jnp.concatenate([ref[pl.ds(c, rows, stride=CHUNKS), :] for c in range(CHUNKS)], axis=1)


def _ada_kernel(c_ref, w_ref, b_ref, o_ref):
    a = _silu(c_ref[...])
    o_ref[0] = _dot(a, w_ref[0]) + b_ref[0]


def _ada_call(c8, w_ada, b_ada):
    tn = 1536
    n_out = N_MOD * D_MODEL
    return pl.pallas_call(
        _ada_kernel,
        grid=(DEPTH, n_out // tn),
        in_specs=[
            pl.BlockSpec((8, D_MODEL), lambda l, j: (0, 0)),
            pl.BlockSpec((1, D_MODEL, tn), lambda l, j: (l, 0, j)),
            pl.BlockSpec((1, 1, tn), lambda l, j: (l, 0, j)),
        ],
        out_specs=pl.BlockSpec((1, 8, tn), lambda l, j: (l, 0, j)),
        out_shape=jax.ShapeDtypeStruct((DEPTH, 8, n_out), F32),
        compiler_params=_cparams(("arbitrary", "arbitrary")),
        name="ada",
    )(c8, w_ada, b_ada.reshape(DEPTH, 1, n_out))


def _rms(x):
    return x * lax.rsqrt(jnp.mean(x * x, axis=-1, keepdims=True) + EPS)


def _norm_mod_kernel(h_ref, g_ref, sc_ref, sh_ref, o_ref):
    y = _rms(h_ref[...]) * g_ref[...]
    o_ref[...] = (y * (1.0 + sc_ref[0]) + sh_ref[0]).astype(o_ref.dtype)


def _ada_spec(mod_idx, row0, tiles_per_row):
    return pl.BlockSpec((1, 1, D_MODEL), lambda i: (row0 + i // tiles_per_row, 0, mod_idx))


def _norm_mod_call(h, g, ada, sc_idx, sh_idx, row0, tiles_per_row):
    m = h.shape[0]
    return pl.pallas_call(
        _norm_mod_kernel,
        grid=(m // TOK_TILE,),
        in_specs=[
            pl.BlockSpec((TOK_TILE, D_MODEL), lambda i: (i, 0)),
            pl.BlockSpec((1, D_MODEL), lambda i: (0, 0)),
            _ada_spec(sc_idx, row0, tiles_per_row),
            _ada_spec(sh_idx, row0, tiles_per_row),
        ],
        out_specs=pl.BlockSpec((TOK_TILE, D_MODEL), lambda i: (i, 0)),
        out_shape=jax.ShapeDtypeStruct((m, D_MODEL), BF16),
        compiler_params=_cparams(("arbitrary",)),
        name="norm_mod",
    )(h, g, ada, ada)


IN_PROJ_ROWS = 1024


def _in_proj_kernel(a_ref, w_ref, o_ref):
    w = w_ref[...].astype(BF16)
    for r in range(a_ref.shape[0] // IN_PROJ_ROWS):
        rows = slice(r * IN_PROJ_ROWS, (r + 1) * IN_PROJ_ROWS)
        o_ref[rows, :] = _dot(a_ref[rows, :], w)


def _src_col_blk(j):
    return jnp.where(j < N_GATE_BLKS, j + SRC_GATE_BLK, j - N_GATE_BLKS)


def _in_proj_call(n, w_in, layer):
    m = n.shape[0]
    return pl.pallas_call(
        _in_proj_kernel,
        grid=(N_COL_BLKS,),
        in_specs=[
            pl.BlockSpec((m, D_MODEL), lambda j: (0, 0)),
            pl.BlockSpec((None, D_MODEL, COL_BLK), lambda j: (layer, 0, _src_col_blk(j))),
        ],
        out_specs=pl.BlockSpec((m, COL_BLK), lambda j: (0, j)),
        out_shape=jax.ShapeDtypeStruct((m, D_IN), F32),
        compiler_params=_cparams(("arbitrary",)),
        name="in_proj",
    )(n, w_in)


def _softmax_pv(s_list, v_list):
    m = s_list[0].max(axis=-1, keepdims=True)
    for s in s_list[1:]:
        m = jnp.maximum(m, s.max(axis=-1, keepdims=True))
    num = None
    den = None
    for s, v in zip(s_list, v_list):
        p = jnp.exp(s - m)
        d = p.sum(axis=-1, keepdims=True)
        o = _dot(p.astype(BF16), v.astype(BF16))
        num = o if num is None else num + o
        den = d if den is None else den + d
    return num / den


assert DH_A == 4 ** (DH_A.bit_length() // 2)


def _ctx_attn_kernel(q_ref, k_ref, v_ref, o_ref):
    scale = DH_A ** -0.5
    for h in range(H_A):
        sl = slice(h * DH_A, (h + 1) * DH_A)
        s = _dot_nt((q_ref[:, sl] * scale).astype(BF16), k_ref[:, sl].astype(BF16))
        o_ref[:, sl] = _softmax_pv([s], [v_ref[:, sl]])


def _ctx_attn_call(z, seq):
    m = z.shape[0]
    spec = lambda blk: pl.BlockSpec((seq, D_A), lambda b: (b, blk))
    return pl.pallas_call(
        _ctx_attn_kernel,
        grid=(m // seq,),
        in_specs=[spec(B_QA), spec(B_KA), spec(B_VA)],
        out_specs=pl.BlockSpec((seq, D_A), lambda b: (b, 0)),
        out_shape=jax.ShapeDtypeStruct((m, D_A), F32),
        compiler_params=_cparams(("arbitrary",)),
        name="ctx_attn",
    )(z, z, z)


def _na_key_row0(rb, rows):
    return jnp.clip(NA_QROWS * rb - (NA_KROWS - WIN_H) , 0, rows - NA_KROWS)


def _na_attn_kernel(q_ref, k_ref, v_ref, ck_ref, cv_ref, bias_ref, o_ref, *, rows):
    scale = DH_A ** -0.5
    rb = pl.program_id(1)
    k0 = pl.multiple_of(_na_key_row0(rb, rows) * GRID_W, GRID_W)
    for h in range(H_A):
        sl = slice(h * DH_A, (h + 1) * DH_A)
        q = (q_ref[:, sl] * scale).astype(BF16)
        kw = k_ref[pl.ds(k0, NA_TK), sl]
        vw = v_ref[pl.ds(k0, NA_TK), sl]
        s_lat = _dot_nt(q, kw.astype(BF16)) + bias_ref[h]
        s_ctx = _dot_nt(q, ck_ref[:, sl].astype(BF16))
        o_ref[:, sl] = _softmax_pv([s_lat, s_ctx], [vw, cv_ref[:, sl]])


def _na_bias_pattern(rb, n_rb):
    return jnp.where(rb == 0, 0, jnp.where(rb == n_rb - 1, 2, 1))


def _na_bias_tables(rpb_l, rows):
    n_heads = rpb_l.shape[0]
    qc = np.arange(GRID_W)
    q_start = np.clip(qc - WIN_W // 2, 0, GRID_W - WIN_W)
    kc = np.arange(GRID_W)
    valid_c = (kc[None, :] >= q_start[:, None]) & (kc[None, :] < q_start[:, None] + WIN_W)
    pad = GRID_W - WIN_W
    rpb_pad = jnp.pad(rpb_l.astype(F32), ((0, 0), (0, 0), (pad, pad)))
    col_tab = jnp.stack([rpb_pad[:, :, GRID_W - 1 - c:2 * GRID_W - 1 - c] for c in range(GRID_W)], axis=2)
    col_tab = jnp.where(valid_c, col_tab, NEG_BIG)
    col_tab = jnp.concatenate([col_tab, jnp.full((n_heads, 1, GRID_W, GRID_W), NEG_BIG, F32)], axis=1)
    n_tab = col_tab.shape[1]
    return pl.pallas_call(
        functools.partial(_na_bias_kernel, rows=rows),
        grid=(3, n_heads),
        in_specs=[pl.BlockSpec((None, n_tab, GRID_W, GRID_W), lambda p, h: (h, 0, 0, 0))],
        out_specs=pl.BlockSpec((None, None, NA_TQ, NA_TK), lambda p, h: (p, h, 0, 0)),
        out_shape=jax.ShapeDtypeStruct((3, n_heads, NA_TQ, NA_TK), F32),
        compiler_params=_cparams(("arbitrary", "arbitrary")),
        name="na_bias",
    )(col_tab)


def _na_bias_kernel(t_ref, o_ref, *, rows):
    n_rb = rows // NA_QROWS
    p = pl.program_id(0)
    rb = jnp.where(p == 0, 0, jnp.where(p == 1, 1, n_rb - 1))
    k_row0 = _na_key_row0(rb, rows)
    masked = t_ref.shape[0] - 1
    for i in range(NA_QROWS):
        r = NA_QROWS * rb + i
        w0 = jnp.clip(r - WIN_H // 2, 0, rows - WIN_H)
        for j in range(NA_KROWS):
            kr = k_row0 + j
            in_window = (kr >= w0) & (kr < w0 + WIN_H)
            tab = jnp.where(in_window, kr - r + WIN_H - 1, masked)
            o_ref[i * GRID_W:(i + 1) * GRID_W, j * GRID_W:(j + 1) * GRID_W] = t_ref[tab]


def _na_attn_call(z, ck, cv, bias, bsz, seq, layer):
    rows = seq // GRID_W
    n_rb = rows // NA_QROWS
    m = z.shape[0]
    kv_spec = lambda blk: pl.BlockSpec((seq, D_A), lambda b, r: (b, blk))
    c_spec = pl.BlockSpec((None, ck.shape[1], D_A), lambda b, r: (b, 0, 0))
    return pl.pallas_call(
        functools.partial(_na_attn_kernel, rows=rows),
        grid=(bsz, n_rb),
        in_specs=[
            pl.BlockSpec((NA_TQ, D_A), lambda b, r: (b * n_rb + r, B_QA)),
            kv_spec(B_KA), kv_spec(B_VA), c_spec, c_spec,
            pl.BlockSpec((None, H_A, NA_TQ, NA_TK), lambda b, r: (_na_bias_pattern(r, n_rb), layer, 0, 0)),
        ],
        out_specs=pl.BlockSpec((NA_TQ, D_A), lambda b, r: (b * n_rb + r, 0)),
        out_shape=jax.ShapeDtypeStruct((m, D_A), F32),
        compiler_params=_cparams(("arbitrary", "arbitrary")),
        name="na_attn",
    )(z, z, z, ck, cv, bias)


def _log1p(x):
    return jnp.log1p(x)


def _hgrn_gates(z, lb):
    log_sig = jnp.minimum(z, 0.0) - _log1p(jnp.exp(-jnp.abs(z)))
    a = _log1p(-lb) + log_sig
    b = jnp.log(jnp.maximum(lb, LOG_FLOOR))
    logf = jnp.maximum(a, b) + _log1p(jnp.exp(-jnp.abs(a - b)))
    key = (1.0 - lb) * _sigmoid(-z)
    return logf, key


def _lower_bound(p_ref, layer):
    p = p_ref[...]
    e = jnp.exp(p - jnp.max(p, axis=0, keepdims=True))
    sm = e / jnp.sum(e, axis=0, keepdims=True)
    lb = jnp.zeros((1, DK_C), F32)
    for j in range(1, layer + 1):
        lb = lb + sm[j:j + 1]
    return lb


def _chunk_cumsum(x, seq, reverse):
    pos = lax.broadcasted_iota(jnp.int32, x.shape, 0) % CHUNK
    s = 1
    while s < CHUNK:
        if reverse:
            x = x + jnp.where(pos < CHUNK - s, pltpu.roll(x, seq - s, axis=0), 0.0)
        else:
            x = x + jnp.where(pos >= s, pltpu.roll(x, s, axis=0), 0.0)
        s *= 2
    return x


HALF = CHUNK // 2
LOG2_E = 1.4426950408889634
HGRN_UNROLL = 8


def _hgrn_chunk(q_ref, v_ref, b_s, k_s, o_s, r0, st, reverse):
    b = b_s[pl.ds(r0, CHUNK), :]
    q = q_ref[pl.ds(r0, CHUNK), :]
    k = k_s[pl.ds(r0, CHUNK), :]
    v = v_ref[pl.ds(r0, CHUNK), :]
    o_inter = _dot_nt(q * jnp.exp2(b), st)
    t_iota = lax.broadcasted_iota(jnp.int32, (HALF, 1), 0)
    o_half = [None, None]
    for s in range(CHUNK):
        b_row = b_s[pl.ds(r0 + s, 1), :]
        k_row = k_s[pl.ds(r0 + s, 1), :]
        v_row = v_ref[pl.ds(r0 + s, 1), :]
        for half in range(2):
            t0 = half * HALF
            if (t0 + HALF - 1 < s) if not reverse else (t0 > s):
                continue
            sl = slice(t0, t0 + HALF)
            decay = jnp.exp2(jnp.minimum(b[sl] - b_row, 0.0))
            col = jnp.sum(q[sl] * (k_row * decay), axis=-1, keepdims=True)
            if not ((t0 >= s) if not reverse else (t0 + HALF - 1 <= s)):
                keep = (t_iota + t0 >= s) if not reverse else (t_iota + t0 <= s)
                col = jnp.where(keep, col, 0.0)
            o_half[half] = col * v_row if o_half[half] is None else o_half[half] + col * v_row
    o_s[pl.ds(r0, CHUNK), :] = jnp.concatenate(o_half, axis=0) + o_inter
    edge = 0 if reverse else CHUNK - 1
    b_edge = b[edge:edge + 1, :]
    kh = k * jnp.exp2(b_edge - b)
    kv_t = lax.dot_general(v, kh, (((0,), (0,)), ((), ())), preferred_element_type=F32)
    return st * jnp.exp2(b_edge) + kv_t


def _hgrn_kernel(q_ref, zf_ref, zb_ref, i_ref, og_ref, lbf_ref, lbb_ref, g_ref, s0_ref,
                 o_ref, sfin_ref, of_s, ob_s, bf_s, bb_s, kf_s, kb_s, *, seq, layer):
    n_chunks = seq // CHUNK
    logf, key = _hgrn_gates(zf_ref[...], _lower_bound(lbf_ref, layer))
    bf_s[...] = _chunk_cumsum(logf * LOG2_E, seq, False)
    kf_s[...] = key
    logf, key = _hgrn_gates(zb_ref[...], _lower_bound(lbb_ref, layer))
    bb_s[...] = _chunk_cumsum(logf * LOG2_E, seq, True)
    kb_s[...] = key

    def body(it, carry):
        st_f, st_b = carry
        for u in range(HGRN_UNROLL):
            n = it * HGRN_UNROLL + u
            st_f = _hgrn_chunk(q_ref, i_ref, bf_s, kf_s, of_s, pl.multiple_of(n * CHUNK, CHUNK), st_f, False)
            st_b = _hgrn_chunk(q_ref, i_ref, bb_s, kb_s, ob_s,
                               pl.multiple_of((n_chunks - 1 - n) * CHUNK, CHUNK), st_b, True)
        return st_f, st_b

    st_f, st_b = lax.fori_loop(0, n_chunks // HGRN_UNROLL, body, (s0_ref[0].T, s0_ref[1].T))
    sfin_ref[0] = st_f.T
    sfin_ref[1] = st_b.T
    o_ref[...] = _rms(of_s[...] + ob_s[...]) * g_ref[...] * _silu(og_ref[...])


def _hgrn_call(z, lb_fwd, lb_bwd, norm_g, s0, bsz, seq, layer):
    m = z.shape[0]
    w = DK_C
    per = COL_BLK // w
    zspec = lambda blk: pl.BlockSpec((seq, w), lambda b, h: (b, blk * per + h))
    st_spec = pl.BlockSpec((None, 2, None, DK_C, DK_C), lambda b, h: (b, 0, h, 0, 0))
    return pl.pallas_call(
        functools.partial(_hgrn_kernel, seq=seq, layer=layer),
        grid=(bsz, H_C),
        in_specs=[
            zspec(B_QC), zspec(B_ZF), zspec(B_ZB), zspec(B_IC), zspec(B_OC),
            pl.BlockSpec((DEPTH, w), lambda b, h: (0, h)),
            pl.BlockSpec((DEPTH, w), lambda b, h: (0, h)),
            pl.BlockSpec((1, w), lambda b, h: (0, h)),
            st_spec,
        ],
        out_specs=[pl.BlockSpec((seq, w), lambda b, h: (b, h)), st_spec],
        out_shape=[jax.ShapeDtypeStruct((m, H_C * w), F32),
                   jax.ShapeDtypeStruct((bsz, 2, H_C, DK_C, DK_C), F32)],
        scratch_shapes=[pltpu.VMEM((seq, w), F32)] * 6,
        compiler_params=_cparams(("arbitrary", "arbitrary")),
        name="hgrn",
    )(z, z, z, z, z, lb_fwd, lb_bwd, norm_g, s0)


def _mix_kernel(oa_ref, oc_ref, u_ref, gb_ref, gc_ref, up_ref, gcp_ref, un_ref, gcn_ref,
                ga_ref, gbr_ref, gcr_ref, h_ref, wconv_ref, wa_ref, wb_ref, wc_ref, wout_ref,
                wrt_ref, g1_ref, n2g_ref, sc2_ref, sh2_ref,
                hout_ref, n2_ref, lgt_ref, *, seq):
    tm = u_ref.shape[0]
    zc = gc_ref[...] * u_ref[...]
    row = lax.broadcasted_iota(jnp.int32, zc.shape, 0)
    pos = (pl.program_id(0) * tm + row) % seq
    zp = jnp.where(row == 0, gcp_ref[7:8, :] * up_ref[7:8, :], pltpu.roll(zc, 1, axis=0))
    zn = jnp.where(row == tm - 1, gcn_ref[0:1, :] * un_ref[0:1, :], pltpu.roll(zc, tm - 1, axis=0))
    zp = jnp.where(pos == 0, 0.0, zp)
    zn = jnp.where(pos == seq - 1, 0.0, zn)
    w = wconv_ref[...]
    ob = gb_ref[...] * (w[0:1] * zp + w[1:2] * zc + w[2:3] * zn)
    ya = _dot(oa_ref[...].astype(BF16), wa_ref[...])
    yb = _dot(ob.astype(BF16), wb_ref[...])
    yc = _dot(oc_ref[...].astype(BF16), wc_ref[...])
    pre = _sigmoid(ga_ref[...]) * ya + _sigmoid(gbr_ref[...]) * yb + _sigmoid(gcr_ref[...]) * yc
    hn = h_ref[...] + g1_ref[0] * _dot(pre.astype(BF16), wout_ref[...])
    hout_ref[...] = hn
    n2 = _rms(hn) * n2g_ref[...] * (1.0 + sc2_ref[0]) + sh2_ref[0]
    _store_token_tiles(n2_ref, n2)
    lgt_ref[...] = _dot_nt(wrt_ref[...], n2, precision=lax.Precision.HIGHEST)


MIX_TILE = 512


def _mix_call(o_a, o_c, z, h, w_conv, wa, wb, wc, wout, wrt, n2g, ada, row0, toks_per_row, seq):
    m = h.shape[0]
    tm = MIX_TILE
    nt = m // tm
    tiles_per_row = toks_per_row // tm
    r8 = tm // 8
    zblk = lambda blk: pl.BlockSpec((tm, COL_BLK), lambda i: (i, blk))
    zprev = lambda blk: pl.BlockSpec((8, COL_BLK), lambda i: (jnp.maximum(i * r8 - 1, 0), blk))
    znext = lambda blk: pl.BlockSpec((8, COL_BLK), lambda i: (jnp.minimum((i + 1) * r8, m // 8 - 1), blk))
    zgate = lambda blk: pl.BlockSpec((tm, D_MODEL), lambda i: (i, blk // 2))
    full = lambda a: pl.BlockSpec(a.shape, lambda i: (0,) * a.ndim)
    tok = lambda wdt: pl.BlockSpec((tm, wdt), lambda i: (i, 0))
    return pl.pallas_call(
        functools.partial(_mix_kernel, seq=seq),
        grid=(nt,),
        in_specs=[
            tok(D_A), tok(D_A), zblk(B_U), zblk(B_GB), zblk(B_GC),
            zprev(B_U), zprev(B_GC), znext(B_U), znext(B_GC),
            zgate(B_GA), zgate(B_GBR), zgate(B_GCR), tok(D_MODEL),
            full(w_conv), full(wa), full(wb), full(wc), full(wout), full(wrt),
            _ada_spec(2, row0, tiles_per_row), full(n2g),
            _ada_spec(4, row0, tiles_per_row), _ada_spec(3, row0, tiles_per_row),
        ],
        out_specs=[tok(D_MODEL), pl.BlockSpec((tm * CHUNKS, LANES), lambda i: (i, 0)),
                   pl.BlockSpec((N_EXPERTS, tm), lambda i: (0, i))],
        out_shape=[jax.ShapeDtypeStruct((m, D_MODEL), F32),
                   jax.ShapeDtypeStruct((m * CHUNKS, LANES), F32),
                   jax.ShapeDtypeStruct((N_EXPERTS, m), F32)],
        compiler_params=_cparams(("arbitrary",)),
        name="mix",
    )(o_a, o_c, z, z, z, z, z, z, z, z, z, z, h, w_conv, wa, wb, wc, wout, wrt, ada, n2g, ada, ada)


def _first_max(x, axes_iota, size):
    m = x
    for ax in range(x.ndim - 1):
        m = jnp.max(m, axis=ax, keepdims=True)
    first = jnp.where(x == m, axes_iota, size)
    f = first
    for ax in range(x.ndim - 1):
        f = jnp.min(f, axis=ax, keepdims=True)
    return axes_iota == f, f


def _first_max_mask(x, axes_iota, size):
    return _first_max(x, axes_iota, size)[0]


def _routing_kernel(lgt_ref, bias_ref, wpick_ref, eidx_ref, rank_ref, cnt_ref, carry_s):
    tn = lgt_ref.shape[-1]

    @pl.when(pl.program_id(0) == 0)
    def _():
        carry_s[...] = jnp.zeros(carry_s.shape, F32)

    s = _sigmoid(lgt_ref[...])
    sel = (s + bias_ref[...]).reshape(N_GROUPS, GROUP_SIZE, tn)
    s = s.reshape(N_GROUPS, GROUP_SIZE, tn)
    neg_inf = -jnp.inf
    mem_iota = lax.broadcasted_iota(jnp.int32, sel.shape, 1)
    m1 = jnp.max(sel, axis=1, keepdims=True)
    first = jnp.min(jnp.where(sel == m1, mem_iota, GROUP_SIZE), axis=1, keepdims=True)
    m2 = jnp.max(jnp.where(mem_iota == first, neg_inf, sel), axis=1, keepdims=True)
    grp = (m1 + m2).reshape(N_GROUPS, tn)
    g_iota = lax.broadcasted_iota(jnp.int32, grp.shape, 0)
    gmask = jnp.zeros(grp.shape, jnp.bool_)
    for _ in range(TOPK_GROUPS):
        pick = _first_max_mask(grp, g_iota, N_GROUPS)
        gmask = gmask | pick
        grp = jnp.where(pick, neg_inf, grp)
    cand = jnp.where(gmask.reshape(N_GROUPS, 1, tn), sel, NEG_BIG)
    e_iota = lax.broadcasted_iota(jnp.int32, sel.shape, 0) * GROUP_SIZE + mem_iota
    chosen = jnp.zeros(sel.shape, jnp.bool_)
    picked = []
    for _ in range(TOP_K):
        pick, idx = _first_max(cand, e_iota, N_EXPERTS)
        chosen = chosen | pick
        cand = jnp.where(pick, neg_inf, cand)
        picked.append(idx)
    w = jnp.where(chosen, s, 0.0)
    tot = jnp.sum(jnp.sum(w, axis=1, keepdims=True), axis=0, keepdims=True)
    w = w / tot * ROUTED_SCALE
    chosen_f = jnp.where(chosen, 1.0, 0.0).reshape(N_EXPERTS, tn)
    earlier = (lax.broadcasted_iota(jnp.int32, (tn, tn), 0) < lax.broadcasted_iota(jnp.int32, (tn, tn), 1))
    rank = _dot(chosen_f.astype(BF16), jnp.where(earlier, 1.0, 0.0).astype(BF16)) + carry_s[:, 0:1]
    rank = rank.reshape(N_GROUPS, GROUP_SIZE, tn)
    of_pick = lambda v, idx: jnp.sum(jnp.sum(jnp.where(e_iota == idx, v, 0.0), axis=1, keepdims=True),
                                     axis=0, keepdims=True).reshape(1, tn)
    eidx_ref[...] = jnp.concatenate([idx.reshape(1, tn) for idx in picked], axis=0)
    rank_ref[...] = jnp.concatenate([of_pick(rank, idx) for idx in picked], axis=0).astype(jnp.int32)
    wpick_ref[...] = jnp.concatenate([of_pick(w, idx) for idx in picked], axis=0)
    carry_s[...] = carry_s[...] + jnp.sum(chosen_f, axis=1, keepdims=True)
    cnt_ref[...] = carry_s[...]


def _routing_call(lgt, b_router_l):
    m = lgt.shape[1]
    tn = 256
    pick_spec = pl.BlockSpec((TOP_K, tn), lambda i: (0, i))
    cnt_spec = pl.BlockSpec((N_EXPERTS, 128), lambda i: (0, 0))
    return pl.pallas_call(
        _routing_kernel,
        grid=(m // tn,),
        in_specs=[pl.BlockSpec((N_EXPERTS, tn), lambda i: (0, i)),
                  pl.BlockSpec((N_EXPERTS, tn), lambda i: (0, 0))],
        out_specs=[pick_spec, pick_spec, pick_spec, cnt_spec],
        out_shape=[jax.ShapeDtypeStruct((TOP_K, m), F32),
                   jax.ShapeDtypeStruct((TOP_K, m), jnp.int32),
                   jax.ShapeDtypeStruct((TOP_K, m), jnp.int32),
                   jax.ShapeDtypeStruct((N_EXPERTS, 128), F32)],
        scratch_shapes=[pltpu.VMEM((N_EXPERTS, 128), F32)],
        compiler_params=_cparams(("arbitrary",)),
        name="routing",
    )(lgt, jnp.broadcast_to(b_router_l[:, None], (N_EXPERTS, tn)))


MOE_TM = 256
ROW_GROUP = 16
assert TOP_K == CHUNKS and TOP_K & (TOP_K - 1) == 0


def _token_row(idx):
    return pl.multiple_of(idx & ~(TOP_K - 1), CHUNKS)


def _dispatch_plan(eidx, rank, cnt, n_tiles_max):
    counts = cnt[:, 0].astype(jnp.int32)
    tiles = (counts + MOE_TM - 1) // MOE_TM
    tile_end = jnp.cumsum(tiles)
    tile_start = tile_end - tiles
    n_tiles = tile_end[-1]
    experts = jnp.arange(N_EXPERTS, dtype=jnp.int32)
    base = jnp.sum(jnp.where(eidx[:, :, None] == experts, tile_start * MOE_TM, 0), axis=-1)
    pos = (rank + base).T.reshape(-1)
    ti = jnp.arange(n_tiles_max, dtype=jnp.int32)
    tile_e = jnp.sum((ti[:, None] >= tile_end[None, :]).astype(jnp.int32), axis=1)
    last_e = jnp.sum(jnp.where(ti == n_tiles - 1, tile_e, 0))
    tile_e = jnp.where(ti < n_tiles, tile_e, last_e).astype(jnp.int32)
    first = ((ti == 0) | (tile_e != jnp.roll(tile_e, 1))) & (ti < n_tiles)
    slot = (jnp.cumsum(first.astype(jnp.int32)) - 1) % 2
    next_tile = jnp.sum(jnp.where(tile_e[:, None] == experts, tile_end, 0), axis=1)
    next_e = jnp.sum(jnp.where(next_tile[:, None] == ti[None, :], tile_e[None, :], 0), axis=1)
    next_e = jnp.where(next_tile < n_tiles, next_e, -1)
    i32 = lambda v: v.astype(jnp.int32)
    return pos, (tile_e, i32(first), i32(next_e), i32(slot), i32(n_tiles).reshape(1))


def _moe_step(tile_g, tile_s, rowidx_s, wpick_ref, x_v, acc_v, g_w, g_r, y_w, y_r, wg_bf, wu_bf, wd_bf):
    row_g = tile_g * MOE_TM
    for r in range(MOE_TM):
        g_w[pl.ds(r * CHUNKS, CHUNKS), :] = x_v[pl.ds(_token_row(rowidx_s[row_g + r]), CHUNKS), :]

    x = _load_token_tiles(g_r, MOE_TM).astype(BF16)
    hid = _silu(_dot(x, wg_bf[...])) * _dot(x, wu_bf[...])
    y = _dot(hid.astype(BF16), wd_bf[...])
    for rb in range(MOE_TM // 8):
        for c in range(CHUNKS):
            y_w[pl.ds((rb * CHUNKS + c) * 8, 8), :] = y[rb * 8:(rb + 1) * 8, c * LANES:(c + 1) * LANES]

    row_s = tile_s * MOE_TM
    for j in range(MOE_TM // ROW_GROUP):
        slots, rows_v = [], []
        for u in range(ROW_GROUP):
            r = j * ROW_GROUP + u
            idx = rowidx_s[row_s + r]
            slots.append(_token_row(idx))
            rows_v.append(y_r[pl.ds((r // 8) * (8 * CHUNKS) + r % 8, CHUNKS, stride=8), :] * wpick_ref[idx])
        olds = [acc_v[pl.ds(s, CHUNKS), :] for s in slots]
        for s, o, v in zip(slots, olds, rows_v):
            acc_v[pl.ds(s, CHUNKS), :] = o + v


def _moe_kernel(te_ref, first_ref, next_ref, slot_ref, nt_ref, pos_ref, wpick_ref, x_hbm, fill_hbm,
                wg_hbm, wu_hbm, wd_hbm, out_hbm,
                rowidx_s, x_v, acc_v, g0_s, g1_s, y0_s, y1_s, wg_f, wu_f, wd_f, wg_bf, wu_bf, wd_bf, sem, wsem,
                *, layer):
    def weight_copies(e, slot):
        return [pltpu.make_async_copy(hbm.at[layer, e], buf.at[slot], wsem.at[slot, k])
                for k, (hbm, buf) in enumerate(((wg_hbm, wg_f), (wu_hbm, wu_f), (wd_hbm, wd_f)))]

    s = pl.program_id(0)
    n_tok = x_hbm.shape[0] // CHUNKS
    n_tiles_max = te_ref.shape[0]
    zero_blk = 512

    @pl.when(s == 0)
    def _prologue():
        x_copy = pltpu.make_async_copy(x_hbm, x_v.at[pl.ds(0, n_tok * CHUNKS)], sem.at[0])
        fill_copy = pltpu.make_async_copy(fill_hbm, rowidx_s, sem.at[1])
        x_copy.start()
        fill_copy.start()
        for cp in weight_copies(te_ref[0], 0):
            cp.start()

        def zero(j, carry):
            acc_v[pl.ds(pl.multiple_of(j * zero_blk, zero_blk), zero_blk), :] = jnp.zeros((zero_blk, LANES), F32)
            return carry

        lax.fori_loop(0, n_tok * CHUNKS // zero_blk, zero, 0)
        acc_v[pl.ds(n_tok * CHUNKS, CHUNKS), :] = jnp.zeros((CHUNKS, LANES), F32)
        x_v[pl.ds(n_tok * CHUNKS, CHUNKS), :] = jnp.zeros((CHUNKS, LANES), F32)
        g1_s[...] = jnp.zeros(g1_s.shape, F32)
        y0_s[...] = jnp.zeros(y0_s.shape, F32)
        fill_copy.wait()

        def invert(j, carry):
            for u in range(ROW_GROUP):
                idx = j * ROW_GROUP + u
                rowidx_s[pos_ref[idx]] = idx
            return carry

        lax.fori_loop(0, n_tok * TOP_K // ROW_GROUP, invert, 0)
        x_copy.wait()

    active = s < nt_ref[0] + 2
    ffn_tile = jnp.clip(s - 1, 0, n_tiles_max - 1)

    @pl.when(active & (first_ref[ffn_tile] == 1) & (s != 1))
    def _next_expert():
        slot = slot_ref[ffn_tile]
        for cp in weight_copies(te_ref[ffn_tile], slot):
            cp.wait()
        wg_bf[...] = wg_f[slot].astype(BF16)
        wu_bf[...] = wu_f[slot].astype(BF16)
        wd_bf[...] = wd_f[slot].astype(BF16)

        @pl.when(next_ref[ffn_tile] >= 0)
        def _prefetch():
            for cp in weight_copies(next_ref[ffn_tile], 1 - slot):
                cp.start()

    tile_g = jnp.minimum(s, n_tiles_max - 1)
    tile_s = jnp.maximum(s - 2, 0)
    common = (rowidx_s, wpick_ref, x_v, acc_v)
    weights = (wg_bf, wu_bf, wd_bf)

    @pl.when(active & (s % 2 == 0))
    def _even():
        _moe_step(tile_g, tile_s, *common, g0_s, g1_s, y1_s, y0_s, *weights)

    @pl.when(active & (s % 2 == 1))
    def _odd():
        _moe_step(tile_g, tile_s, *common, g1_s, g0_s, y0_s, y1_s, *weights)

    @pl.when(s == pl.num_programs(0) - 1)
    def _epilogue():
        out_copy = pltpu.make_async_copy(acc_v.at[pl.ds(0, n_tok * CHUNKS)], out_hbm, sem.at[2])
        out_copy.start()
        out_copy.wait()


def _moe_call(x, wpick, eidx, rank, cnt, w_gate, w_up, w_down, layer):
    m = x.shape[0] // CHUNKS
    n_rows = TOP_K * m
    n_tiles_max = n_rows // MOE_TM + N_EXPERTS
    n_slots = n_tiles_max * MOE_TM
    pos, tables = _dispatch_plan(eidx, rank, cnt, n_tiles_max)
    wflat = jnp.concatenate([wpick.T.reshape(-1), jnp.zeros((TOP_K,), F32)])
    anyspec = pl.BlockSpec(memory_space=pl.ANY)
    smem = pl.BlockSpec(memory_space=pltpu.SMEM)
    tile_buf = pltpu.VMEM((MOE_TM * CHUNKS, LANES), F32)
    grid_spec = pltpu.PrefetchScalarGridSpec(
        num_scalar_prefetch=len(tables),
        grid=(n_tiles_max + 2,),
        in_specs=[smem, smem, anyspec, anyspec, anyspec, anyspec, anyspec],
        out_specs=anyspec,
        scratch_shapes=[
            pltpu.SMEM((n_slots,), jnp.int32),
            pltpu.VMEM(((m + 1) * CHUNKS, LANES), F32),
            pltpu.VMEM(((m + 1) * CHUNKS, LANES), F32),
            tile_buf, tile_buf, tile_buf, tile_buf,
            pltpu.VMEM((2, D_MODEL, D_EXPERT), F32),
            pltpu.VMEM((2, D_MODEL, D_EXPERT), F32),
            pltpu.VMEM((2, D_EXPERT, D_MODEL), F32),
            pltpu.VMEM((D_MODEL, D_EXPERT), BF16),
            pltpu.VMEM((D_MODEL, D_EXPERT), BF16),
            pltpu.VMEM((D_EXPERT, D_MODEL), BF16),
            pltpu.SemaphoreType.DMA((3,)),
            pltpu.SemaphoreType.DMA((2, 3)),
        ],
    )
    return pl.pallas_call(
        functools.partial(_moe_kernel, layer=layer),
        grid_spec=grid_spec,
        out_shape=jax.ShapeDtypeStruct((m * CHUNKS, LANES), F32),
        compiler_params=_cparams(("arbitrary",)),
        name="moe",
    )(*tables, pos, wflat, x, jnp.full((n_slots,), n_rows, jnp.int32), w_gate, w_up, w_down)


def _moe_finish_kernel(x_ref, routed_ref, h_ref, wsg_ref, wsu_ref, wsd_ref, g2_ref, gn_ref, scn_ref, shn_ref,
                       *out_refs, last):
    rows = h_ref.shape[0]
    x = _load_token_tiles(x_ref, rows).astype(BF16)
    hid = _silu(_dot(x, wsg_ref[...])) * _dot(x, wsu_ref[...])
    shared = _dot(hid.astype(BF16), wsd_ref[...])
    h = h_ref[...] + g2_ref[0] * (_load_token_tiles(routed_ref, rows) + shared)
    y = _rms(h) * gn_ref[...]
    if last:
        (y_ref,) = out_refs
        y_ref[...] = y
    else:
        h_out_ref, n_ref = out_refs
        h_out_ref[...] = h
        n_ref[...] = (y * (1.0 + scn_ref[0]) + shn_ref[0]).astype(BF16)


def _moe_finish_call(x, routed, h, wsg, wsu, wsd, ada, row0, tiles_per_row, norm_g, ada_next, last):
    m = h.shape[0]
    tok = pl.BlockSpec((TOK_TILE, D_MODEL), lambda i: (i, 0))
    tiles = pl.BlockSpec((TOK_TILE * CHUNKS, LANES), lambda i: (i, 0))
    full = lambda a: pl.BlockSpec(a.shape, lambda i: (0,) * a.ndim)
    out_f32 = jax.ShapeDtypeStruct((m, D_MODEL), F32)
    return pl.pallas_call(
        functools.partial(_moe_finish_kernel, last=last),
        grid=(m // TOK_TILE,),
        in_specs=[tiles, tiles, tok, full(wsg), full(wsu), full(wsd), _ada_spec(5, row0, tiles_per_row),
                  full(norm_g), _ada_spec(1, row0, tiles_per_row), _ada_spec(0, row0, tiles_per_row)],
        out_specs=[tok] if last else [tok, tok],
        out_shape=[out_f32] if last else [out_f32, jax.ShapeDtypeStruct((m, D_MODEL), BF16)],
        compiler_params=_cparams(("arbitrary",)),
        name="moe_finish",
    )(x, routed, h, wsg, wsu, wsd, ada, norm_g, ada_next, ada_next)


def _stream(h, p, ada, row0, bsz, seq, attend, states):
    toks_per_row = seq if row0 > 0 else bsz * seq
    tiles_per_row = toks_per_row // TOK_TILE
    keys, vals, sts = [], [], []
    ada_rows = [ada[l].reshape(8, 1, N_MOD * D_MODEL) for l in range(DEPTH)]
    n = _norm_mod_call(h, p['norm1_g'][0:1], ada_rows[0], 1, 0, row0, tiles_per_row)
    for l in range(DEPTH):
        ada_l = ada_rows[l]
        last = l == DEPTH - 1
        z = _in_proj_call(n, p['w_in'], l)
        o_a = attend(z, l)
        o_c, st = _hgrn_call(z, p['lb_fwd'], p['lb_bwd'], p['hgrn_norm_g'][l:l + 1], states[l], bsz, seq, l)
        h, n2, lgt = _mix_call(o_a, o_c, z, h, p['w_conv'][l], p['w_br_a'][l], p['w_br_b'][l], p['w_br_c'][l],
                               p['w_out'][l], p['w_router_t'][l], p['norm2_g'][l:l + 1], ada_l,
                               row0, toks_per_row, seq)
        wpick, eidx, rank, cnt = _routing_call(lgt, p['b_router'][l])
        routed = _moe_call(n2, wpick, eidx, rank, cnt, p['w_gate'], p['w_up'], p['w_down'], l)
        outs = _moe_finish_call(n2, routed, h, p['w_sh_gate'][l], p['w_sh_up'][l], p['w_sh_down'][l],
                                ada_l, row0, tiles_per_row,
                                p['final_norm_g'] if last else p['norm1_g'][l + 1:l + 2],
                                ada_l if last else ada_rows[l + 1], last)
        h, n = (outs[0], None) if last else outs
        keys.append(z[:, B_KA * COL_BLK:(B_KA + 1) * COL_BLK])
        vals.append(z[:, B_VA * COL_BLK:(B_VA + 1) * COL_BLK])
        sts.append(st)
    return h, keys, vals, sts


def kernel(x_prompt, x_sample, cache_k, cache_v, state_hgrn, c, c_ctx, norm1_g, norm2_g, w_ada, b_ada,
           w_in, rpb, w_conv, lb_fwd, lb_bwd, hgrn_norm_g, w_br_a, w_br_b, w_br_c, w_out, w_router,
           b_router, w_gate, w_up, w_down, w_sh_gate, w_sh_up, w_sh_down, final_norm_g):
    batch, seq, _ = x_prompt.shape
    dec_batch, dec_seq, _ = x_sample.shape
    past = cache_k.shape[2]
    p = dict(norm1_g=norm1_g, norm2_g=norm2_g, w_in=w_in, w_conv=w_conv, lb_fwd=lb_fwd, lb_bwd=lb_bwd,
             hgrn_norm_g=hgrn_norm_g,
             w_br_a=w_br_a.astype(BF16), w_br_b=w_br_b.astype(BF16), w_br_c=w_br_c.astype(BF16),
             w_out=w_out.astype(BF16), w_router_t=jnp.swapaxes(w_router, 1, 2), b_router=b_router,
             w_gate=w_gate, w_up=w_up, w_down=w_down,
             w_sh_gate=w_sh_gate.astype(BF16), w_sh_up=w_sh_up.astype(BF16), w_sh_down=w_sh_down.astype(BF16),
             final_norm_g=final_norm_g.reshape(1, D_MODEL))

    c8 = jnp.zeros((8, D_MODEL), F32).at[0].set(c_ctx).at[1:1 + dec_batch].set(c)
    ada = _ada_call(c8, w_ada, b_ada)

    zero_states = [jnp.zeros((batch, 2, H_C, DK_C, DK_C), F32)] * DEPTH
    ctx_attend = lambda z, l: _ctx_attn_call(z, seq)
    h_ctx, keys, vals, sts = _stream(x_prompt.reshape(batch * seq, D_MODEL), p, ada, 0, batch, seq,
                                     ctx_attend, zero_states)
    y_prompt = h_ctx.reshape(batch, seq, D_MODEL)
    new_cache_k = jnp.stack([k.reshape(batch, seq, H_A, DH_A) for k in keys], axis=1)
    new_cache_v = jnp.stack([v.reshape(batch, seq, H_A, DH_A) for v in vals], axis=1)
    new_state = jnp.stack(sts, axis=1)

    rows = dec_seq // GRID_W
    ck = cache_k.reshape(dec_batch, DEPTH, past, D_A)
    cv = cache_v.reshape(dec_batch, DEPTH, past, D_A)
    lat_states = [state_hgrn[:, l].astype(F32) for l in range(DEPTH)]

    bias = _na_bias_tables(rpb.reshape(DEPTH * H_A, 2 * WIN_H - 1, 2 * WIN_W - 1), rows)

    def lat_attend(z, l):
        return _na_attn_call(z, ck[:, l], cv[:, l], bias, dec_batch, dec_seq, l)

    h_lat, _, _, _ = _stream(x_sample.reshape(dec_batch * dec_seq, D_MODEL), p, ada, 1, dec_batch, dec_seq,
                             lat_attend, lat_states)
    y_sample = h_lat.reshape(dec_batch, dec_seq, D_MODEL)
    return (y_prompt, y_sample, new_cache_k, new_cache_v, new_state)
```

```python
import functools

import numpy as np
import jax
import jax.numpy as jnp
from jax import lax
from jax.experimental import pallas as pl
from jax.experimental.pallas import tpu as pltpu

F32 = jnp.float32
BF16 = jnp.bfloat16

D_MODEL = 1024
DEPTH = 2
GRID_W = 64
H_A = 8
DH_A = 64
D_A = H_A * DH_A
WIN_H = 8
WIN_W = 16
D_CONV = 512
H_C = 4
DK_C = 128
CHUNK = 16
N_EXPERTS = 64
N_GROUPS = 8
GROUP_SIZE = N_EXPERTS // N_GROUPS
TOPK_GROUPS = 4
TOP_K = 8
D_EXPERT = 256
ROUTED_SCALE = 2.5
N_MOD = 6
EPS = 1e-6
NEG_BIG = -1e30
LOG_FLOOR = 1e-30

D_IN = 8704
COL_BLK = 512
N_COL_BLKS = D_IN // COL_BLK
SRC_GATE_BLK = 11
N_GATE_BLKS = 6
B_GA, B_GBR, B_GCR = 0, 2, 4
B_QA, B_KA, B_VA, B_U, B_GB, B_GC, B_QC, B_ZF, B_ZB, B_IC, B_OC = range(6, 17)

NA_QROWS = 4
NA_KROWS = 12
NA_TQ = NA_QROWS * GRID_W
NA_TK = NA_KROWS * GRID_W

TOK_TILE = 512
VMEM_LIMIT = 56 * 1024 * 1024


def _cparams(sem):
    return pltpu.CompilerParams(dimension_semantics=sem, vmem_limit_bytes=VMEM_LIMIT)


def _sigmoid(x):
    return 1.0 / (1.0 + jnp.exp(-x))


def _silu(x):
    return x * _sigmoid(x)


def _dot(a, b):
    return jnp.dot(a, b, preferred_element_type=F32)


def _dot_nt(a, b, precision=None):
    return lax.dot_general(a, b, (((1,), (1,)), ((), ())), preferred_element_type=F32,
                           precision=precision)


LANES = 128
CHUNKS = D_MODEL // LANES


def _store_token_tiles(ref, val):
    for rb in range(val.shape[0] // 8):
        for c in range(CHUNKS):
            ref[pl.ds(rb * 8 * CHUNKS + c, 8, stride=CHUNKS), :] = \
                val[rb * 8:(rb + 1) * 8, c * LANES:(c + 1) * LANES]


def _load_token_tiles(ref, rows):
    return jnp.concatenate([ref[pl.ds(c, rows, stride=CHUNKS), :] for c in range(CHUNKS)], axis=1)


def _ada_kernel(c_ref, w_ref, b_ref, o_ref):
    a = _silu(c_ref[...])
    o_ref[0] = _dot(a, w_ref[0]) + b_ref[0]


def _ada_call(c8, w_ada, b_ada):
    tn = 1536
    n_out = N_MOD * D_MODEL
    return pl.pallas_call(
        _ada_kernel,
        grid=(DEPTH, n_out // tn),
        in_specs=[
            pl.BlockSpec((8, D_MODEL), lambda l, j: (0, 0)),
            pl.BlockSpec((1, D_MODEL, tn), lambda l, j: (l, 0, j)),
            pl.BlockSpec((1, 1, tn), lambda l, j: (l, 0, j)),
        ],
        out_specs=pl.BlockSpec((1, 8, tn), lambda l, j: (l, 0, j)),
        out_shape=jax.ShapeDtypeStruct((DEPTH, 8, n_out), F32),
        compiler_params=_cparams(("arbitrary", "arbitrary")),
        name="ada",
    )(c8, w_ada, b_ada.reshape(DEPTH, 1, n_out))


def _rms(x):
    return x * lax.rsqrt(jnp.mean(x * x, axis=-1, keepdims=True) + EPS)


def _norm_mod_kernel(h_ref, g_ref, sc_ref, sh_ref, o_ref):
    y = _rms(h_ref[...]) * g_ref[...]
    o_ref[...] = (y * (1.0 + sc_ref[0]) + sh_ref[0]).astype(o_ref.dtype)


def _ada_spec(mod_idx, row0, tiles_per_row):
    return pl.BlockSpec((1, 1, D_MODEL), lambda i: (row0 + i // tiles_per_row, 0, mod_idx))


def _norm_mod_call(h, g, ada, sc_idx, sh_idx, row0, tiles_per_row):
    m = h.shape[0]
    return pl.pallas_call(
        _norm_mod_kernel,
        grid=(m // TOK_TILE,),
        in_specs=[
            pl.BlockSpec((TOK_TILE, D_MODEL), lambda i: (i, 0)),
            pl.BlockSpec((1, D_MODEL), lambda i: (0, 0)),
            _ada_spec(sc_idx, row0, tiles_per_row),
            _ada_spec(sh_idx, row0, tiles_per_row),
        ],
        out_specs=pl.BlockSpec((TOK_TILE, D_MODEL), lambda i: (i, 0)),
        out_shape=jax.ShapeDtypeStruct((m, D_MODEL), BF16),
        compiler_params=_cparams(("arbitrary",)),
        name="norm_mod",
    )(h, g, ada, ada)


IN_PROJ_ROWS = 1024


def _in_proj_kernel(a_ref, w_ref, o_ref):
    w = w_ref[...].astype(BF16)
    for r in range(a_ref.shape[0] // IN_PROJ_ROWS):
        rows = slice(r * IN_PROJ_ROWS, (r + 1) * IN_PROJ_ROWS)
        o_ref[rows, :] = _dot(a_ref[rows, :], w)


def _src_col_blk(j):
    return jnp.where(j < N_GATE_BLKS, j + SRC_GATE_BLK, j - N_GATE_BLKS)


def _in_proj_call(n, w_in, layer):
    m = n.shape[0]
    return pl.pallas_call(
        _in_proj_kernel,
        grid=(N_COL_BLKS,),
        in_specs=[
            pl.BlockSpec((m, D_MODEL), lambda j: (0, 0)),
            pl.BlockSpec((None, D_MODEL, COL_BLK), lambda j: (layer, 0, _src_col_blk(j))),
        ],
        out_specs=pl.BlockSpec((m, COL_BLK), lambda j: (0, j)),
        out_shape=jax.ShapeDtypeStruct((m, D_IN), F32),
        compiler_params=_cparams(("arbitrary",)),
        name="in_proj",
    )(n, w_in)


def _softmax_pv(s_list, v_list):
    m = s_list[0].max(axis=-1, keepdims=True)
    for s in s_list[1:]:
        m = jnp.maximum(m, s.max(axis=-1, keepdims=True))
    num = None
    den = None
    for s, v in zip(s_list, v_list):
        p = jnp.exp(s - m)
        d = p.sum(axis=-1, keepdims=True)
        o = _dot(p.astype(BF16), v.astype(BF16))
        num = o if num is None else num + o
        den = d if den is None else den + d
    return num / den


assert DH_A == 4 ** (DH_A.bit_length() // 2)


HEAD_PAIR = 2 * DH_A
assert HEAD_PAIR == LANES


def _pair_queries(q_pair):
    first = lax.broadcasted_iota(jnp.int32, q_pair.shape, 1) < DH_A
    q_pair = q_pair * DH_A ** -0.5
    return [jnp.where(first, q_pair, 0.0).astype(BF16), jnp.where(first, 0.0, q_pair).astype(BF16)], first


def _ctx_attn_kernel(q_ref, k_ref, v_ref, o_ref):
    for g in range(H_A // 2):
        sl = slice(g * HEAD_PAIR, (g + 1) * HEAD_PAIR)
        queries, first = _pair_queries(q_ref[:, sl])
        k = k_ref[:, sl].astype(BF16)
        outs = [_softmax_pv([_dot_nt(q, k)], [v_ref[:, sl]]) for q in queries]
        o_ref[:, sl] = jnp.where(first, outs[0], outs[1])


def _ctx_attn_call(z, seq):
    m = z.shape[0]
    spec = lambda blk: pl.BlockSpec((seq, D_A), lambda b: (b, blk))
    return pl.pallas_call(
        _ctx_attn_kernel,
        grid=(m // seq,),
        in_specs=[spec(B_QA), spec(B_KA), spec(B_VA)],
        out_specs=pl.BlockSpec((seq, D_A), lambda b: (b, 0)),
        out_shape=jax.ShapeDtypeStruct((m, D_A), F32),
        compiler_params=_cparams(("arbitrary",)),
        name="ctx_attn",
    )(z, z, z)


def _na_key_row0(rb, rows):
    return jnp.clip(NA_QROWS * rb - (NA_KROWS - WIN_H) , 0, rows - NA_KROWS)


def _na_attn_kernel(q_ref, k_ref, v_ref, ck_ref, cv_ref, bias_ref, o_ref, *, rows):
    rb = pl.program_id(1)
    k0 = pl.multiple_of(_na_key_row0(rb, rows) * GRID_W, GRID_W)
    for g in range(H_A // 2):
        sl = slice(g * HEAD_PAIR, (g + 1) * HEAD_PAIR)
        queries, first = _pair_queries(q_ref[:, sl])
        kw = k_ref[pl.ds(k0, NA_TK), sl].astype(BF16)
        vw = v_ref[pl.ds(k0, NA_TK), sl]
        ck = ck_ref[:, sl].astype(BF16)
        outs = []
        for hh, q in enumerate(queries):
            s_lat = _dot_nt(q, kw) + bias_ref[2 * g + hh]
            outs.append(_softmax_pv([s_lat, _dot_nt(q, ck)], [vw, cv_ref[:, sl]]))
        o_ref[:, sl] = jnp.where(first, outs[0], outs[1])


def _na_bias_pattern(rb, n_rb):
    return jnp.where(rb == 0, 0, jnp.where(rb == n_rb - 1, 2, 1))


def _na_bias_tables(rpb_l, rows):
    n_heads = rpb_l.shape[0]
    qc = np.arange(GRID_W)
    q_start = np.clip(qc - WIN_W // 2, 0, GRID_W - WIN_W)
    kc = np.arange(GRID_W)
    valid_c = (kc[None, :] >= q_start[:, None]) & (kc[None, :] < q_start[:, None] + WIN_W)
    pad = GRID_W - WIN_W
    rpb_pad = jnp.pad(rpb_l.astype(F32), ((0, 0), (0, 0), (pad, pad)))
    col_tab = jnp.stack([rpb_pad[:, :, GRID_W - 1 - c:2 * GRID_W - 1 - c] for c in range(GRID_W)], axis=2)
    col_tab = jnp.where(valid_c, col_tab, NEG_BIG)
    col_tab = jnp.concatenate([col_tab, jnp.full((n_heads, 1, GRID_W, GRID_W), NEG_BIG, F32)], axis=1)
    n_tab = col_tab.shape[1]
    return pl.pallas_call(
        functools.partial(_na_bias_kernel, rows=rows),
        grid=(3, n_heads),
        in_specs=[pl.BlockSpec((None, n_tab, GRID_W, GRID_W), lambda p, h: (h, 0, 0, 0))],
        out_specs=pl.BlockSpec((None, None, NA_TQ, NA_TK), lambda p, h: (p, h, 0, 0)),
        out_shape=jax.ShapeDtypeStruct((3, n_heads, NA_TQ, NA_TK), F32),
        compiler_params=_cparams(("arbitrary", "arbitrary")),
        name="na_bias",
    )(col_tab)


def _na_bias_kernel(t_ref, o_ref, *, rows):
    n_rb = rows // NA_QROWS
    p = pl.program_id(0)
    rb = jnp.where(p == 0, 0, jnp.where(p == 1, 1, n_rb - 1))
    k_row0 = _na_key_row0(rb, rows)
    masked = t_ref.shape[0] - 1
    for i in range(NA_QROWS):
        r = NA_QROWS * rb + i
        w0 = jnp.clip(r - WIN_H // 2, 0, rows - WIN_H)
        for j in range(NA_KROWS):
            kr = k_row0 + j
            in_window = (kr >= w0) & (kr < w0 + WIN_H)
            tab = jnp.where(in_window, kr - r + WIN_H - 1, masked)
            o_ref[i * GRID_W:(i + 1) * GRID_W, j * GRID_W:(j + 1) * GRID_W] = t_ref[tab]


def _na_attn_call(z, ck, cv, bias, bsz, seq, layer):
    rows = seq // GRID_W
    n_rb = rows // NA_QROWS
    m = z.shape[0]
    kv_spec = lambda blk: pl.BlockSpec((seq, D_A), lambda b, r: (b, blk))
    c_spec = pl.BlockSpec((None, ck.shape[1], D_A), lambda b, r: (b, 0, 0))
    return pl.pallas_call(
        functools.partial(_na_attn_kernel, rows=rows),
        grid=(bsz, n_rb),
        in_specs=[
            pl.BlockSpec((NA_TQ, D_A), lambda b, r: (b * n_rb + r, B_QA)),
            kv_spec(B_KA), kv_spec(B_VA), c_spec, c_spec,
            pl.BlockSpec((None, H_A, NA_TQ, NA_TK), lambda b, r: (_na_bias_pattern(r, n_rb), layer, 0, 0)),
        ],
        out_specs=pl.BlockSpec((NA_TQ, D_A), lambda b, r: (b * n_rb + r, 0)),
        out_shape=jax.ShapeDtypeStruct((m, D_A), F32),
        compiler_params=_cparams(("arbitrary", "arbitrary")),
        name="na_attn",
    )(z, z, z, ck, cv, bias)


def _log1p(x):
    return jnp.log1p(x)


def _hgrn_gates(z, lb):
    log_sig = jnp.minimum(z, 0.0) - _log1p(jnp.exp(-jnp.abs(z)))
    a = _log1p(-lb) + log_sig
    b = jnp.log(jnp.maximum(lb, LOG_FLOOR))
    logf = jnp.maximum(a, b) + _log1p(jnp.exp(-jnp.abs(a - b)))
    key = (1.0 - lb) * _sigmoid(-z)
    return logf, key


def _lower_bound(p_ref, layer):
    p = p_ref[...]
    e = jnp.exp(p - jnp.max(p, axis=0, keepdims=True))
    sm = e / jnp.sum(e, axis=0, keepdims=True)
    lb = jnp.zeros((1, DK_C), F32)
    for j in range(1, layer + 1):
        lb = lb + sm[j:j + 1]
    return lb


def _chunk_cumsum(x, seq, reverse):
    pos = lax.broadcasted_iota(jnp.int32, x.shape, 0) % CHUNK
    s = 1
    while s < CHUNK:
        if reverse:
            x = x + jnp.where(pos < CHUNK - s, pltpu.roll(x, seq - s, axis=0), 0.0)
        else:
            x = x + jnp.where(pos >= s, pltpu.roll(x, s, axis=0), 0.0)
        s *= 2
    return x


HALF = CHUNK // 2
LOG2_E = 1.4426950408889634
HGRN_UNROLL = 8


def _hgrn_chunk(q_ref, v_ref, b_s, k_s, o_s, r0, st, reverse):
    b = b_s[pl.ds(r0, CHUNK), :]
    q = q_ref[pl.ds(r0, CHUNK), :]
    k = k_s[pl.ds(r0, CHUNK), :]
    v = v_ref[pl.ds(r0, CHUNK), :]
    o_inter = _dot_nt(q * jnp.exp2(b), st)
    t_iota = lax.broadcasted_iota(jnp.int32, (HALF, 1), 0)
    o_half = [None, None]
    for s in range(CHUNK):
        b_row = b_s[pl.ds(r0 + s, 1), :]
        k_row = k_s[pl.ds(r0 + s, 1), :]
        v_row = v_ref[pl.ds(r0 + s, 1), :]
        for half in range(2):
            t0 = half * HALF
            if (t0 + HALF - 1 < s) if not reverse else (t0 > s):
                continue
            sl = slice(t0, t0 + HALF)
            decay = jnp.exp2(jnp.minimum(b[sl] - b_row, 0.0))
            col = jnp.sum(q[sl] * (k_row * decay), axis=-1, keepdims=True)
            if not ((t0 >= s) if not reverse else (t0 + HALF - 1 <= s)):
                keep = (t_iota + t0 >= s) if not reverse else (t_iota + t0 <= s)
                col = jnp.where(keep, col, 0.0)
            o_half[half] = col * v_row if o_half[half] is None else o_half[half] + col * v_row
    o_s[pl.ds(r0, CHUNK), :] = jnp.concatenate(o_half, axis=0) + o_inter
    edge = 0 if reverse else CHUNK - 1
    b_edge = b[edge:edge + 1, :]
    kh = k * jnp.exp2(b_edge - b)
    kv_t = lax.dot_general(v, kh, (((0,), (0,)), ((), ())), preferred_element_type=F32)
    return st * jnp.exp2(b_edge) + kv_t


def _hgrn_kernel(q_ref, zf_ref, zb_ref, i_ref, og_ref, lbf_ref, lbb_ref, g_ref, s0_ref,
                 o_ref, sfin_ref, of_s, ob_s, bf_s, bb_s, kf_s, kb_s, *, seq, layer):
    n_chunks = seq // CHUNK
    logf, key = _hgrn_gates(zf_ref[...], _lower_bound(lbf_ref, layer))
    bf_s[...] = _chunk_cumsum(logf * LOG2_E, seq, False)
    kf_s[...] = key
    logf, key = _hgrn_gates(zb_ref[...], _lower_bound(lbb_ref, layer))
    bb_s[...] = _chunk_cumsum(logf * LOG2_E, seq, True)
    kb_s[...] = key

    def body(it, carry):
        st_f, st_b = carry
        for u in range(HGRN_UNROLL):
            n = it * HGRN_UNROLL + u
            st_f = _hgrn_chunk(q_ref, i_ref, bf_s, kf_s, of_s, pl.multiple_of(n * CHUNK, CHUNK), st_f, False)
            st_b = _hgrn_chunk(q_ref, i_ref, bb_s, kb_s, ob_s,
                               pl.multiple_of((n_chunks - 1 - n) * CHUNK, CHUNK), st_b, True)
        return st_f, st_b

    st_f, st_b = lax.fori_loop(0, n_chunks // HGRN_UNROLL, body, (s0_ref[0].T, s0_ref[1].T))
    sfin_ref[0] = st_f.T
    sfin_ref[1] = st_b.T
    o_ref[...] = _rms(of_s[...] + ob_s[...]) * g_ref[...] * _silu(og_ref[...])


def _hgrn_call(z, lb_fwd, lb_bwd, norm_g, s0, bsz, seq, layer):
    m = z.shape[0]
    w = DK_C
    per = COL_BLK // w
    zspec = lambda blk: pl.BlockSpec((seq, w), lambda b, h: (b, blk * per + h))
    st_spec = pl.BlockSpec((None, 2, None, DK_C, DK_C), lambda b, h: (b, 0, h, 0, 0))
    return pl.pallas_call(
        functools.partial(_hgrn_kernel, seq=seq, layer=layer),
        grid=(bsz, H_C),
        in_specs=[
            zspec(B_QC), zspec(B_ZF), zspec(B_ZB), zspec(B_IC), zspec(B_OC),
            pl.BlockSpec((DEPTH, w), lambda b, h: (0, h)),
            pl.BlockSpec((DEPTH, w), lambda b, h: (0, h)),
            pl.BlockSpec((1, w), lambda b, h: (0, h)),
            st_spec,
        ],
        out_specs=[pl.BlockSpec((seq, w), lambda b, h: (b, h)), st_spec],
        out_shape=[jax.ShapeDtypeStruct((m, H_C * w), F32),
                   jax.ShapeDtypeStruct((bsz, 2, H_C, DK_C, DK_C), F32)],
        scratch_shapes=[pltpu.VMEM((seq, w), F32)] * 6,
        compiler_params=_cparams(("arbitrary", "arbitrary")),
        name="hgrn",
    )(z, z, z, z, z, lb_fwd, lb_bwd, norm_g, s0)


def _mix_kernel(oa_ref, oc_ref, u_ref, gb_ref, gc_ref, up_ref, gcp_ref, un_ref, gcn_ref,
                ga_ref, gbr_ref, gcr_ref, h_ref, wconv_ref, wa_ref, wb_ref, wc_ref, wout_ref,
                wrt_ref, g1_ref, n2g_ref, sc2_ref, sh2_ref,
                hout_ref, n2_ref, lgt_ref, *, seq):
    tm = u_ref.shape[0]
    zc = gc_ref[...] * u_ref[...]
    row = lax.broadcasted_iota(jnp.int32, zc.shape, 0)
    pos = (pl.program_id(0) * tm + row) % seq
    zp = jnp.where(row == 0, gcp_ref[7:8, :] * up_ref[7:8, :], pltpu.roll(zc, 1, axis=0))
    zn = jnp.where(row == tm - 1, gcn_ref[0:1, :] * un_ref[0:1, :], pltpu.roll(zc, tm - 1, axis=0))
    zp = jnp.where(pos == 0, 0.0, zp)
    zn = jnp.where(pos == seq - 1, 0.0, zn)
    w = wconv_ref[...]
    ob = gb_ref[...] * (w[0:1] * zp + w[1:2] * zc + w[2:3] * zn)
    ya = _dot(oa_ref[...].astype(BF16), wa_ref[...])
    yb = _dot(ob.astype(BF16), wb_ref[...])
    yc = _dot(oc_ref[...].astype(BF16), wc_ref[...])
    pre = _sigmoid(ga_ref[...]) * ya + _sigmoid(gbr_ref[...]) * yb + _sigmoid(gcr_ref[...]) * yc
    hn = h_ref[...] + g1_ref[0] * _dot(pre.astype(BF16), wout_ref[...])
    hout_ref[...] = hn
    n2 = _rms(hn) * n2g_ref[...] * (1.0 + sc2_ref[0]) + sh2_ref[0]
    _store_token_tiles(n2_ref, n2)
    lgt_ref[...] = _dot_nt(wrt_ref[...], n2, precision=lax.Precision.HIGHEST)


MIX_TILE = 512


def _mix_call(o_a, o_c, z, h, w_conv, wa, wb, wc, wout, wrt, n2g, ada, row0, toks_per_row, seq):
    m = h.shape[0]
    tm = MIX_TILE
    nt = m // tm
    tiles_per_row = toks_per_row // tm
    r8 = tm // 8
    zblk = lambda blk: pl.BlockSpec((tm, COL_BLK), lambda i: (i, blk))
    zprev = lambda blk: pl.BlockSpec((8, COL_BLK), lambda i: (jnp.maximum(i * r8 - 1, 0), blk))
    znext = lambda blk: pl.BlockSpec((8, COL_BLK), lambda i: (jnp.minimum((i + 1) * r8, m // 8 - 1), blk))
    zgate = lambda blk: pl.BlockSpec((tm, D_MODEL), lambda i: (i, blk // 2))
    full = lambda a: pl.BlockSpec(a.shape, lambda i: (0,) * a.ndim)
    tok = lambda wdt: pl.BlockSpec((tm, wdt), lambda i: (i, 0))
    return pl.pallas_call(
        functools.partial(_mix_kernel, seq=seq),
        grid=(nt,),
        in_specs=[
            tok(D_A), tok(D_A), zblk(B_U), zblk(B_GB), zblk(B_GC),
            zprev(B_U), zprev(B_GC), znext(B_U), znext(B_GC),
            zgate(B_GA), zgate(B_GBR), zgate(B_GCR), tok(D_MODEL),
            full(w_conv), full(wa), full(wb), full(wc), full(wout), full(wrt),
            _ada_spec(2, row0, tiles_per_row), full(n2g),
            _ada_spec(4, row0, tiles_per_row), _ada_spec(3, row0, tiles_per_row),
        ],
        out_specs=[tok(D_MODEL), pl.BlockSpec((tm * CHUNKS, LANES), lambda i: (i, 0)),
                   pl.BlockSpec((N_EXPERTS, tm), lambda i: (0, i))],
        out_shape=[jax.ShapeDtypeStruct((m, D_MODEL), F32),
                   jax.ShapeDtypeStruct((m * CHUNKS, LANES), F32),
                   jax.ShapeDtypeStruct((N_EXPERTS, m), F32)],
        compiler_params=_cparams(("arbitrary",)),
        name="mix",
    )(o_a, o_c, z, z, z, z, z, z, z, z, z, z, h, w_conv, wa, wb, wc, wout, wrt, ada, n2g, ada, ada)


def _first_max(x, axes_iota, size):
    m = x
    for ax in range(x.ndim - 1):
        m = jnp.max(m, axis=ax, keepdims=True)
    first = jnp.where(x == m, axes_iota, size)
    f = first
    for ax in range(x.ndim - 1):
        f = jnp.min(f, axis=ax, keepdims=True)
    return axes_iota == f, f


def _first_max_mask(x, axes_iota, size):
    return _first_max(x, axes_iota, size)[0]


def _routing_kernel(lgt_ref, bias_ref, wpick_ref, eidx_ref, rank_ref, cnt_ref, carry_s):
    tn = lgt_ref.shape[-1]

    @pl.when(pl.program_id(0) == 0)
    def _():
        carry_s[...] = jnp.zeros(carry_s.shape, F32)

    s = _sigmoid(lgt_ref[...])
    sel = (s + bias_ref[...]).reshape(N_GROUPS, GROUP_SIZE, tn)
    s = s.reshape(N_GROUPS, GROUP_SIZE, tn)
    neg_inf = -jnp.inf
    mem_iota = lax.broadcasted_iota(jnp.int32, sel.shape, 1)
    m1 = jnp.max(sel, axis=1, keepdims=True)
    first = jnp.min(jnp.where(sel == m1, mem_iota, GROUP_SIZE), axis=1, keepdims=True)
    m2 = jnp.max(jnp.where(mem_iota == first, neg_inf, sel), axis=1, keepdims=True)
    grp = (m1 + m2).reshape(N_GROUPS, tn)
    g_iota = lax.broadcasted_iota(jnp.int32, grp.shape, 0)
    gmask = jnp.zeros(grp.shape, jnp.bool_)
    for _ in range(TOPK_GROUPS):
        pick = _first_max_mask(grp, g_iota, N_GROUPS)
        gmask = gmask | pick
        grp = jnp.where(pick, neg_inf, grp)
    cand = jnp.where(gmask.reshape(N_GROUPS, 1, tn), sel, NEG_BIG)
    e_iota = lax.broadcasted_iota(jnp.int32, sel.shape, 0) * GROUP_SIZE + mem_iota
    chosen = jnp.zeros(sel.shape, jnp.bool_)
    picked = []
    for _ in range(TOP_K):
        pick, idx = _first_max(cand, e_iota, N_EXPERTS)
        chosen = chosen | pick
        cand = jnp.where(pick, neg_inf, cand)
        picked.append(idx)
    w = jnp.where(chosen, s, 0.0)
    tot = jnp.sum(jnp.sum(w, axis=1, keepdims=True), axis=0, keepdims=True)
    w = w / tot * ROUTED_SCALE
    chosen_f = jnp.where(chosen, 1.0, 0.0).reshape(N_EXPERTS, tn)
    earlier = (lax.broadcasted_iota(jnp.int32, (tn, tn), 0) < lax.broadcasted_iota(jnp.int32, (tn, tn), 1))
    rank = _dot(chosen_f.astype(BF16), jnp.where(earlier, 1.0, 0.0).astype(BF16)) + carry_s[:, 0:1]
    rank = rank.reshape(N_GROUPS, GROUP_SIZE, tn)
    of_pick = lambda v, idx: jnp.sum(jnp.sum(jnp.where(e_iota == idx, v, 0.0), axis=1, keepdims=True),
                                     axis=0, keepdims=True).reshape(1, tn)
    eidx_ref[...] = jnp.concatenate([idx.reshape(1, tn) for idx in picked], axis=0)
    rank_ref[...] = jnp.concatenate([of_pick(rank, idx) for idx in picked], axis=0).astype(jnp.int32)
    wpick_ref[...] = jnp.concatenate([of_pick(w, idx) for idx in picked], axis=0)
    carry_s[...] = carry_s[...] + jnp.sum(chosen_f, axis=1, keepdims=True)
    cnt_ref[...] = carry_s[...]


def _routing_call(lgt, b_router_l):
    m = lgt.shape[1]
    tn = 256
    pick_spec = pl.BlockSpec((TOP_K, tn), lambda i: (0, i))
    cnt_spec = pl.BlockSpec((N_EXPERTS, 128), lambda i: (0, 0))
    return pl.pallas_call(
        _routing_kernel,
        grid=(m // tn,),
        in_specs=[pl.BlockSpec((N_EXPERTS, tn), lambda i: (0, i)),
                  pl.BlockSpec((N_EXPERTS, tn), lambda i: (0, 0))],
        out_specs=[pick_spec, pick_spec, pick_spec, cnt_spec],
        out_shape=[jax.ShapeDtypeStruct((TOP_K, m), F32),
                   jax.ShapeDtypeStruct((TOP_K, m), jnp.int32),
                   jax.ShapeDtypeStruct((TOP_K, m), jnp.int32),
                   jax.ShapeDtypeStruct((N_EXPERTS, 128), F32)],
        scratch_shapes=[pltpu.VMEM((N_EXPERTS, 128), F32)],
        compiler_params=_cparams(("arbitrary",)),
        name="routing",
    )(lgt, jnp.broadcast_to(b_router_l[:, None], (N_EXPERTS, tn)))


MOE_TM = 256
ROW_GROUP = 16
assert TOP_K == CHUNKS and TOP_K & (TOP_K - 1) == 0


def _token_row(idx):
    return pl.multiple_of(idx & ~(TOP_K - 1), CHUNKS)


def _dispatch_plan(eidx, rank, cnt, n_tiles_max):
    counts = cnt[:, 0].astype(jnp.int32)
    tiles = (counts + MOE_TM - 1) // MOE_TM
    tile_end = jnp.cumsum(tiles)
    tile_start = tile_end - tiles
    n_tiles = tile_end[-1]
    experts = jnp.arange(N_EXPERTS, dtype=jnp.int32)
    base = jnp.sum(jnp.where(eidx[:, :, None] == experts, tile_start * MOE_TM, 0), axis=-1)
    pos = (rank + base).T.reshape(-1)
    ti = jnp.arange(n_tiles_max, dtype=jnp.int32)
    tile_e = jnp.sum((ti[:, None] >= tile_end[None, :]).astype(jnp.int32), axis=1)
    last_e = jnp.sum(jnp.where(ti == n_tiles - 1, tile_e, 0))
    tile_e = jnp.where(ti < n_tiles, tile_e, last_e).astype(jnp.int32)
    first = ((ti == 0) | (tile_e != jnp.roll(tile_e, 1))) & (ti < n_tiles)
    slot = (jnp.cumsum(first.astype(jnp.int32)) - 1) % 2
    next_tile = jnp.sum(jnp.where(tile_e[:, None] == experts, tile_end, 0), axis=1)
    next_e = jnp.sum(jnp.where(next_tile[:, None] == ti[None, :], tile_e[None, :], 0), axis=1)
    next_e = jnp.where(next_tile < n_tiles, next_e, -1)
    i32 = lambda v: v.astype(jnp.int32)
    return pos, (tile_e, i32(first), i32(next_e), i32(slot), i32(n_tiles).reshape(1))


def _moe_step(tile_g, tile_s, rowidx_s, wpick_ref, x_v, acc_v, g_w, g_r, y_w, y_r, wg_bf, wu_bf, wd_bf):
    row_g = tile_g * MOE_TM
    for r in range(MOE_TM):
        g_w[pl.ds(r * CHUNKS, CHUNKS), :] = x_v[pl.ds(_token_row(rowidx_s[row_g + r]), CHUNKS), :]

    x = _load_token_tiles(g_r, MOE_TM).astype(BF16)
    hid = _silu(_dot(x, wg_bf[...])) * _dot(x, wu_bf[...])
    y = _dot(hid.astype(BF16), wd_bf[...])
    for rb in range(MOE_TM // 8):
        for c in range(CHUNKS):
            y_w[pl.ds((rb * CHUNKS + c) * 8, 8), :] = y[rb * 8:(rb + 1) * 8, c * LANES:(c + 1) * LANES]

    row_s = tile_s * MOE_TM
    for j in range(MOE_TM // ROW_GROUP):
        slots, rows_v = [], []
        for u in range(ROW_GROUP):
            r = j * ROW_GROUP + u
            idx = rowidx_s[row_s + r]
            slots.append(_token_row(idx))
            rows_v.append(y_r[pl.ds((r // 8) * (8 * CHUNKS) + r % 8, CHUNKS, stride=8), :] * wpick_ref[idx])
        olds = [acc_v[pl.ds(s, CHUNKS), :] for s in slots]
        for s, o, v in zip(slots, olds, rows_v):
            acc_v[pl.ds(s, CHUNKS), :] = o + v


def _moe_kernel(te_ref, first_ref, next_ref, slot_ref, nt_ref, pos_ref, wpick_ref, x_hbm, fill_hbm,
                wg_hbm, wu_hbm, wd_hbm, out_hbm,
                rowidx_s, x_v, acc_v, g0_s, g1_s, y0_s, y1_s, wg_f, wu_f, wd_f, wg_bf, wu_bf, wd_bf, sem, wsem,
                *, layer):
    def weight_copies(e, slot):
        return [pltpu.make_async_copy(hbm.at[layer, e], buf.at[slot], wsem.at[slot, k])
                for k, (hbm, buf) in enumerate(((wg_hbm, wg_f), (wu_hbm, wu_f), (wd_hbm, wd_f)))]

    s = pl.program_id(0)
    n_tok = x_hbm.shape[0] // CHUNKS
    n_tiles_max = te_ref.shape[0]
    zero_blk = 512

    @pl.when(s == 0)
    def _prologue():
        x_copy = pltpu.make_async_copy(x_hbm, x_v.at[pl.ds(0, n_tok * CHUNKS)], sem.at[0])
        fill_copy = pltpu.make_async_copy(fill_hbm, rowidx_s, sem.at[1])
        x_copy.start()
        fill_copy.start()
        for cp in weight_copies(te_ref[0], 0):
            cp.start()

        def zero(j, carry):
            acc_v[pl.ds(pl.multiple_of(j * zero_blk, zero_blk), zero_blk), :] = jnp.zeros((zero_blk, LANES), F32)
            return carry

        lax.fori_loop(0, n_tok * CHUNKS // zero_blk, zero, 0)
        acc_v[pl.ds(n_tok * CHUNKS, CHUNKS), :] = jnp.zeros((CHUNKS, LANES), F32)
        x_v[pl.ds(n_tok * CHUNKS, CHUNKS), :] = jnp.zeros((CHUNKS, LANES), F32)
        g1_s[...] = jnp.zeros(g1_s.shape, F32)
        y0_s[...] = jnp.zeros(y0_s.shape, F32)
        fill_copy.wait()

        def invert(j, carry):
            for u in range(ROW_GROUP):
                idx = j * ROW_GROUP + u
                rowidx_s[pos_ref[idx]] = idx
            return carry

        lax.fori_loop(0, n_tok * TOP_K // ROW_GROUP, invert, 0)
        x_copy.wait()

    active = s < nt_ref[0] + 2
    ffn_tile = jnp.clip(s - 1, 0, n_tiles_max - 1)

    @pl.when(active & (first_ref[ffn_tile] == 1) & (s != 1))
    def _next_expert():
        slot = slot_ref[ffn_tile]
        for cp in weight_copies(te_ref[ffn_tile], slot):
            cp.wait()
        wg_bf[...] = wg_f[slot].astype(BF16)
        wu_bf[...] = wu_f[slot].astype(BF16)
        wd_bf[...] = wd_f[slot].astype(BF16)

        @pl.when(next_ref[ffn_tile] >= 0)
        def _prefetch():
            for cp in weight_copies(next_ref[ffn_tile], 1 - slot):
                cp.start()

    tile_g = jnp.minimum(s, n_tiles_max - 1)
    tile_s = jnp.maximum(s - 2, 0)
    common = (rowidx_s, wpick_ref, x_v, acc_v)
    weights = (wg_bf, wu_bf, wd_bf)

    @pl.when(active & (s % 2 == 0))
    def _even():
        _moe_step(tile_g, tile_s, *common, g0_s, g1_s, y1_s, y0_s, *weights)

    @pl.when(active & (s % 2 == 1))
    def _odd():
        _moe_step(tile_g, tile_s, *common, g1_s, g0_s, y0_s, y1_s, *weights)

    @pl.when(s == pl.num_programs(0) - 1)
    def _epilogue():
        out_copy = pltpu.make_async_copy(acc_v.at[pl.ds(0, n_tok * CHUNKS)], out_hbm, sem.at[2])
        out_copy.start()
        out_copy.wait()


def _moe_call(x, wpick, eidx, rank, cnt, w_gate, w_up, w_down, layer):
    m = x.shape[0] // CHUNKS
    n_rows = TOP_K * m
    n_tiles_max = n_rows // MOE_TM + N_EXPERTS
    n_slots = n_tiles_max * MOE_TM
    pos, tables = _dispatch_plan(eidx, rank, cnt, n_tiles_max)
    wflat = jnp.concatenate([wpick.T.reshape(-1), jnp.zeros((TOP_K,), F32)])
    anyspec = pl.BlockSpec(memory_space=pl.ANY)
    smem = pl.BlockSpec(memory_space=pltpu.SMEM)
    tile_buf = pltpu.VMEM((MOE_TM * CHUNKS, LANES), F32)
    grid_spec = pltpu.PrefetchScalarGridSpec(
        num_scalar_prefetch=len(tables),
        grid=(n_tiles_max + 2,),
        in_specs=[smem, smem, anyspec, anyspec, anyspec, anyspec, anyspec],
        out_specs=anyspec,
        scratch_shapes=[
            pltpu.SMEM((n_slots,), jnp.int32),
            pltpu.VMEM(((m + 1) * CHUNKS, LANES), F32),
            pltpu.VMEM(((m + 1) * CHUNKS, LANES), F32),
            tile_buf, tile_buf, tile_buf, tile_buf,
            pltpu.VMEM((2, D_MODEL, D_EXPERT), F32),
            pltpu.VMEM((2, D_MODEL, D_EXPERT), F32),
            pltpu.VMEM((2, D_EXPERT, D_MODEL), F32),
            pltpu.VMEM((D_MODEL, D_EXPERT), BF16),
            pltpu.VMEM((D_MODEL, D_EXPERT), BF16),
            pltpu.VMEM((D_EXPERT, D_MODEL), BF16),
            pltpu.SemaphoreType.DMA((3,)),
            pltpu.SemaphoreType.DMA((2, 3)),
        ],
    )
    return pl.pallas_call(
        functools.partial(_moe_kernel, layer=layer),
        grid_spec=grid_spec,
        out_shape=jax.ShapeDtypeStruct((m * CHUNKS, LANES), F32),
        compiler_params=_cparams(("arbitrary",)),
        name="moe",
    )(*tables, pos, wflat, x, jnp.full((n_slots,), n_rows, jnp.int32), w_gate, w_up, w_down)


def _moe_finish_kernel(x_ref, routed_ref, h_ref, wsg_ref, wsu_ref, wsd_ref, g2_ref, gn_ref, scn_ref, shn_ref,
                       *out_refs, last):
    rows = h_ref.shape[0]
    x = _load_token_tiles(x_ref, rows).astype(BF16)
    hid = _silu(_dot(x, wsg_ref[...])) * _dot(x, wsu_ref[...])
    shared = _dot(hid.astype(BF16), wsd_ref[...])
    h = h_ref[...] + g2_ref[0] * (_load_token_tiles(routed_ref, rows) + shared)
    y = _rms(h) * gn_ref[...]
    if last:
        (y_ref,) = out_refs
        y_ref[...] = y
    else:
        h_out_ref, n_ref = out_refs
        h_out_ref[...] = h
        n_ref[...] = (y * (1.0 + scn_ref[0]) + shn_ref[0]).astype(BF16)


def _moe_finish_call(x, routed, h, wsg, wsu, wsd, ada, row0, tiles_per_row, norm_g, ada_next, last):
    m = h.shape[0]
    tok = pl.BlockSpec((TOK_TILE, D_MODEL), lambda i: (i, 0))
    tiles = pl.BlockSpec((TOK_TILE * CHUNKS, LANES), lambda i: (i, 0))
    full = lambda a: pl.BlockSpec(a.shape, lambda i: (0,) * a.ndim)
    out_f32 = jax.ShapeDtypeStruct((m, D_MODEL), F32)
    return pl.pallas_call(
        functools.partial(_moe_finish_kernel, last=last),
        grid=(m // TOK_TILE,),
        in_specs=[tiles, tiles, tok, full(wsg), full(wsu), full(wsd), _ada_spec(5, row0, tiles_per_row),
                  full(norm_g), _ada_spec(1, row0, tiles_per_row), _ada_spec(0, row0, tiles_per_row)],
        out_specs=[tok] if last else [tok, tok],
        out_shape=[out_f32] if last else [out_f32, jax.ShapeDtypeStruct((m, D_MODEL), BF16)],
        compiler_params=_cparams(("arbitrary",)),
        name="moe_finish",
    )(x, routed, h, wsg, wsu, wsd, ada, norm_g, ada_next, ada_next)


def _stream(h, p, ada, row0, bsz, seq, attend, states):
    toks_per_row = seq if row0 > 0 else bsz * seq
    tiles_per_row = toks_per_row // TOK_TILE
    keys, vals, sts = [], [], []
    ada_rows = [ada[l].reshape(8, 1, N_MOD * D_MODEL) for l in range(DEPTH)]
    n = _norm_mod_call(h, p['norm1_g'][0:1], ada_rows[0], 1, 0, row0, tiles_per_row)
    for l in range(DEPTH):
        ada_l = ada_rows[l]
        last = l == DEPTH - 1
        z = _in_proj_call(n, p['w_in'], l)
        o_a = attend(z, l)
        o_c, st = _hgrn_call(z, p['lb_fwd'], p['lb_bwd'], p['hgrn_norm_g'][l:l + 1], states[l], bsz, seq, l)
        h, n2, lgt = _mix_call(o_a, o_c, z, h, p['w_conv'][l], p['w_br_a'][l], p['w_br_b'][l], p['w_br_c'][l],
                               p['w_out'][l], p['w_router_t'][l], p['norm2_g'][l:l + 1], ada_l,
                               row0, toks_per_row, seq)
        wpick, eidx, rank, cnt = _routing_call(lgt, p['b_router'][l])
        routed = _moe_call(n2, wpick, eidx, rank, cnt, p['w_gate'], p['w_up'], p['w_down'], l)
        outs = _moe_finish_call(n2, routed, h, p['w_sh_gate'][l], p['w_sh_up'][l], p['w_sh_down'][l],
                                ada_l, row0, tiles_per_row,
                                p['final_norm_g'] if last else p['norm1_g'][l + 1:l + 2],
                                ada_l if last else ada_rows[l + 1], last)
        h, n = (outs[0], None) if last else outs
        keys.append(z[:, B_KA * COL_BLK:(B_KA + 1) * COL_BLK])
        vals.append(z[:, B_VA * COL_BLK:(B_VA + 1) * COL_BLK])
        sts.append(st)
    return h, keys, vals, sts


def kernel(x_prompt, x_sample, cache_k, cache_v, state_hgrn, c, c_ctx, norm1_g, norm2_g, w_ada, b_ada,
           w_in, rpb, w_conv, lb_fwd, lb_bwd, hgrn_norm_g, w_br_a, w_br_b, w_br_c, w_out, w_router,
           b_router, w_gate, w_up, w_down, w_sh_gate, w_sh_up, w_sh_down, final_norm_g):
    batch, seq, _ = x_prompt.shape
    dec_batch, dec_seq, _ = x_sample.shape
    past = cache_k.shape[2]
    p = dict(norm1_g=norm1_g, norm2_g=norm2_g, w_in=w_in, w_conv=w_conv, lb_fwd=lb_fwd, lb_bwd=lb_bwd,
             hgrn_norm_g=hgrn_norm_g,
             w_br_a=w_br_a.astype(BF16), w_br_b=w_br_b.astype(BF16), w_br_c=w_br_c.astype(BF16),
             w_out=w_out.astype(BF16), w_router_t=jnp.swapaxes(w_router, 1, 2), b_router=b_router,
             w_gate=w_gate, w_up=w_up, w_down=w_down,
             w_sh_gate=w_sh_gate.astype(BF16), w_sh_up=w_sh_up.astype(BF16), w_sh_down=w_sh_down.astype(BF16),
             final_norm_g=final_norm_g.reshape(1, D_MODEL))

    c8 = jnp.zeros((8, D_MODEL), F32).at[0].set(c_ctx).at[1:1 + dec_batch].set(c)
    ada = _ada_call(c8, w_ada, b_ada)

    zero_states = [jnp.zeros((batch, 2, H_C, DK_C, DK_C), F32)] * DEPTH
    ctx_attend = lambda z, l: _ctx_attn_call(z, seq)
    h_ctx, keys, vals, sts = _stream(x_prompt.reshape(batch * seq, D_MODEL), p, ada, 0, batch, seq,
                                     ctx_attend, zero_states)
    y_prompt = h_ctx.reshape(batch, seq, D_MODEL)
    new_cache_k = jnp.stack([k.reshape(batch, seq, H_A, DH_A) for k in keys], axis=1)
    new_cache_v = jnp.stack([v.reshape(batch, seq, H_A, DH_A) for v in vals], axis=1)
    new_state = jnp.stack(sts, axis=1)

    rows = dec_seq // GRID_W
    ck = cache_k.reshape(dec_batch, DEPTH, past, D_A)
    cv = cache_v.reshape(dec_batch, DEPTH, past, D_A)
    lat_states = [state_hgrn[:, l].astype(F32) for l in range(DEPTH)]

    bias = _na_bias_tables(rpb.reshape(DEPTH * H_A, 2 * WIN_H - 1, 2 * WIN_W - 1), rows)

    def lat_attend(z, l):
        return _na_attn_call(z, ck[:, l], cv[:, l], bias, dec_batch, dec_seq, l)

    h_lat, _, _, _ = _stream(x_sample.reshape(dec_batch * dec_seq, D_MODEL), p, ada, 1, dec_batch, dec_seq,
                             lat_attend, lat_states)
    y_sample = h_lat.reshape(dec_batch, dec_seq, D_MODEL)
    return (y_prompt, y_sample, new_cache_k, new_cache_v, new_state)
```
